```python
import math
import jax, jax.numpy as jnp
from jax import lax
import numpy as np

D_MODEL = 1024
BATCH = 2
SEQ = 16384
DEPTH = 2

N_HEADS = 8
HEAD_DIM = 64
V_HEAD_DIM = 2 * HEAD_DIM
ATTN_QK_W = N_HEADS * 2 * HEAD_DIM
ATTN_V_W = N_HEADS * V_HEAD_DIM
Q_BLOCK = 128
CONV_W = 1024
CONV_K = 3
NUM_BUCKETS = 32
MAX_DISTANCE = 128
FF_DENSE = 2816
N_EXPERTS = 8
TOP_K = 2
FF_EXPERT = 3584
RMS_EPS = 1e-6
IN_W = 2 * ATTN_QK_W + ATTN_V_W + 3 * CONV_W + 2 * D_MODEL
N_DENSE_LAYERS = (DEPTH + 1) // 2
N_MOE_LAYERS = DEPTH // 2

kernel_name = "hybrid_diffattn_shortconv_moe_trunk"


def lambda_init_fn(layer_idx):
    return 0.8 - 0.6 * math.exp(-0.3 * layer_idx)


def rmsnorm(x, g):
    xf = x.astype(jnp.float32)
    y = xf * lax.rsqrt(jnp.mean(xf * xf, axis=-1, keepdims=True) + RMS_EPS)
    return (y * g.astype(jnp.float32)).astype(x.dtype)


def t5_causal_bucket(dist):
    max_exact = NUM_BUCKETS // 2
    is_small = dist < max_exact
    d = jnp.maximum(dist, 1).astype(jnp.float32)
    large = max_exact + (jnp.log(d / max_exact) / math.log(MAX_DISTANCE / max_exact)
                         * (NUM_BUCKETS - max_exact)).astype(jnp.int32)
    large = jnp.minimum(large, NUM_BUCKETS - 1)
    return jnp.where(is_small, dist, large)


def diff_attention(q, k, v, bias_dist, lam, lam_init, subln_g):
    B, S = q.shape[0], q.shape[1]
    nblk = S // Q_BLOCK
    scale = HEAD_DIM ** -0.5
    qb = q.reshape(B, nblk, Q_BLOCK, N_HEADS, 2, HEAD_DIM).swapaxes(0, 1)
    kf = k.astype(jnp.float32)
    vf = v.astype(jnp.float32)
    kpos = jnp.arange(S, dtype=jnp.int32)
    g = subln_g.astype(jnp.float32) * (1.0 - lam_init)

    def one_block(args):
        blk, qblk = args
        qpos = blk * Q_BLOCK + jnp.arange(Q_BLOCK, dtype=jnp.int32)
        dist = qpos[:, None] - kpos[None, :]
        bias = bias_dist[:, jnp.maximum(dist, 0)]
        s = jnp.einsum('bqhmd,bkhmd->bhmqk', qblk.astype(jnp.float32) * scale, kf)
        s = s + bias[None, :, None]
        s = jnp.where(dist >= 0, s, -jnp.inf)
        p = jax.nn.softmax(s, axis=-1)
        w = p[:, :, 0] - lam * p[:, :, 1]
        o = jnp.einsum('bhqk,bkhe->bqhe', w, vf)
        o = o * lax.rsqrt(jnp.mean(o * o, axis=-1, keepdims=True) + RMS_EPS) * g
        return o

    out = lax.map(one_block, (jnp.arange(nblk, dtype=jnp.int32), qb))
    return out.swapaxes(0, 1).reshape(B, S, ATTN_V_W).astype(q.dtype)


def short_gated_conv(h, gate_b, gate_c, conv_w):
    u = gate_c * h
    up = jnp.pad(u, ((0, 0), (CONV_K - 1, 0), (0, 0)))
    S = h.shape[1]
    y = conv_w[0] * up[:, 0:S] + conv_w[1] * up[:, 1:S + 1] + conv_w[2] * up[:, 2:S + 2]
    return gate_b * y


def swiglu(h, w_gu, w_down):
    gu = h @ w_gu
    g, u = jnp.split(gu, 2, axis=-1)
    return (jax.nn.silu(g) * u) @ w_down


def moe_swiglu(h, router_w, w_gu, w_down):
    B, S, D = h.shape
    xf = h.reshape(B * S, D)
    logits = (xf @ router_w).astype(jnp.float32)
    top_v, top_i = lax.top_k(logits, TOP_K)
    gates = jax.nn.softmax(top_v, axis=-1)
    cw = jnp.sum(jax.nn.one_hot(top_i, N_EXPERTS, dtype=jnp.float32) * gates[..., None], axis=1)
    cw = cw.astype(h.dtype)
    out = jnp.zeros_like(xf)
    for e in range(N_EXPERTS):
        out = out + cw[:, e:e + 1] * swiglu(xf, w_gu[e], w_down[e])
    return out.reshape(B, S, D)


def setup_inputs(seed: int = 0) -> dict:
    key = jax.random.key(seed)
    ks = jax.random.split(key, 17)
    f32 = jnp.float32
    nrm = lambda k, shape, s: (jax.random.normal(k, shape, f32) * s)
    x = jax.random.normal(ks[0], (BATCH, SEQ, D_MODEL), f32)
    rel_bias = nrm(ks[1], (NUM_BUCKETS, N_HEADS), 0.5)
    mix_norm_g = 1.0 + nrm(ks[2], (DEPTH, D_MODEL), 0.02)
    w_in = nrm(ks[3], (DEPTH, D_MODEL, IN_W), D_MODEL ** -0.5)
    lam_qk = nrm(ks[4], (DEPTH, 4, HEAD_DIM), 0.1)
    subln_g = 1.0 + nrm(ks[5], (DEPTH, V_HEAD_DIM), 0.02)
    conv_w = nrm(ks[6], (DEPTH, CONV_K, CONV_W), CONV_K ** -0.5)
    w_o_attn = nrm(ks[7], (DEPTH, ATTN_V_W, D_MODEL), ATTN_V_W ** -0.5)
    w_o_conv = nrm(ks[8], (DEPTH, CONV_W, D_MODEL), CONV_W ** -0.5)
    w_o = nrm(ks[9], (DEPTH, D_MODEL, D_MODEL), D_MODEL ** -0.5)
    ffn_norm_g = 1.0 + nrm(ks[10], (DEPTH, D_MODEL), 0.02)
    dense_w_gate_up = nrm(ks[11], (N_DENSE_LAYERS, D_MODEL, 2 * FF_DENSE), D_MODEL ** -0.5)
    dense_w_down = nrm(ks[12], (N_DENSE_LAYERS, FF_DENSE, D_MODEL), FF_DENSE ** -0.5)
    router_w = nrm(ks[13], (N_MOE_LAYERS, D_MODEL, N_EXPERTS), D_MODEL ** -0.5)
    expert_w_gate_up = nrm(ks[14], (N_MOE_LAYERS, N_EXPERTS, D_MODEL, 2 * FF_EXPERT), D_MODEL ** -0.5)
    expert_w_down = nrm(ks[15], (N_MOE_LAYERS, N_EXPERTS, FF_EXPERT, D_MODEL), FF_EXPERT ** -0.5)
    final_norm_g = 1.0 + nrm(ks[16], (D_MODEL,), 0.02)
    return {"x": x, "rel_bias": rel_bias, "mix_norm_g": mix_norm_g, "w_in": w_in,
            "lam_qk": lam_qk, "subln_g": subln_g, "conv_w": conv_w,
            "w_o_attn": w_o_attn, "w_o_conv": w_o_conv, "w_o": w_o,
            "ffn_norm_g": ffn_norm_g, "dense_w_gate_up": dense_w_gate_up,
            "dense_w_down": dense_w_down, "router_w": router_w,
            "expert_w_gate_up": expert_w_gate_up, "expert_w_down": expert_w_down,
            "final_norm_g": final_norm_g}


def reference(x, rel_bias, mix_norm_g, w_in, lam_qk, subln_g, conv_w, w_o_attn, w_o_conv, w_o,
              ffn_norm_g, dense_w_gate_up, dense_w_down, router_w, expert_w_gate_up,
              expert_w_down, final_norm_g):
    B, S, D = x.shape
    dists = jnp.arange(S, dtype=jnp.int32)
    bias_dist = rel_bias.astype(jnp.float32)[t5_causal_bucket(dists)].T
    offs = np.cumsum([ATTN_QK_W, ATTN_QK_W, ATTN_V_W, CONV_W, CONV_W, CONV_W, D_MODEL]).tolist()

    for l in range(DEPTH):
        h = rmsnorm(x, mix_norm_g[l])
        z = h @ w_in[l]
        q, k, v, ch, cb, cc, ga, gc = jnp.split(z, offs, axis=-1)
        q = q.reshape(B, S, N_HEADS, 2, HEAD_DIM)
        k = k.reshape(B, S, N_HEADS, 2, HEAD_DIM)
        v = v.reshape(B, S, N_HEADS, V_HEAD_DIM)
        lq = lam_qk[l].astype(jnp.float32)
        lam_init = lambda_init_fn(l)
        lam = jnp.exp(jnp.sum(lq[0] * lq[1])) - jnp.exp(jnp.sum(lq[2] * lq[3])) + lam_init
        a = diff_attention(q, k, v, bias_dist, lam, lam_init, subln_g[l])
        c = short_gated_conv(ch, cb, cc, conv_w[l])
        merged = jax.nn.sigmoid(ga) * (a @ w_o_attn[l]) + jax.nn.sigmoid(gc) * (c @ w_o_conv[l])
        x = x + merged @ w_o[l]
        h = rmsnorm(x, ffn_norm_g[l])
        if l % 2 == 0:
            f = swiglu(h, dense_w_gate_up[l // 2], dense_w_down[l // 2])
        else:
            f = moe_swiglu(h, router_w[l // 2], expert_w_gate_up[l // 2], expert_w_down[l // 2])
        x = x + f

    return rmsnorm(x, final_norm_g)
```

```python
import functools
import math

import jax
import jax.numpy as jnp
from jax import lax
from jax.experimental import pallas as pl
from jax.experimental.pallas import tpu as pltpu

F32 = jnp.float32
BF16 = jnp.bfloat16

N_HEADS = 8
HEAD_DIM = 64
V_HEAD_DIM = 2 * HEAD_DIM
NUM_BUCKETS = 32
MAX_DISTANCE = 128
N_EXPERTS = 8
RMS_EPS = 1e-6

LANES = 128
VMEM_LIMIT_BYTES = 56 * 1024 * 1024

LOG2E = math.log2(math.e)
NEG_BIG = -1e30

ATTN_TQ = 512
ATTN_TK = 512
ROW_TILE = 512
MOE_TILE = 512


def _cparams(sem):
    return pltpu.CompilerParams(dimension_semantics=sem, vmem_limit_bytes=VMEM_LIMIT_BYTES)


def _rms(x, g):
    ms = jnp.mean(x * x, axis=-1, keepdims=True)
    return x * lax.rsqrt(ms + RMS_EPS) * g


def _norm_mm_kernel(x_ref, g_ref, w_ref, o_ref, h_ref):
    @pl.when(pl.program_id(1) == 0)
    def _():
        h_ref[...] = _rms(x_ref[...], g_ref[...]).astype(BF16)

    o_ref[...] = jnp.dot(h_ref[...], w_ref[...], preferred_element_type=F32).astype(o_ref.dtype)


def _norm_matmul(x, g, w, tm, tn):
    t, d = x.shape
    n = w.shape[1]
    return pl.pallas_call(
        _norm_mm_kernel,
        grid=(t // tm, n // tn),
        in_specs=[
            pl.BlockSpec((tm, d), lambda i, j: (i, 0)),
            pl.BlockSpec((1, d), lambda i, j: (0, 0)),
            pl.BlockSpec((d, tn), lambda i, j: (0, j)),
        ],
        out_specs=pl.BlockSpec((tm, tn), lambda i, j: (i, j)),
        out_shape=jax.ShapeDtypeStruct((t, n), BF16),
        scratch_shapes=[pltpu.VMEM((tm, d), BF16)],
        compiler_params=_cparams(("parallel", "arbitrary")),
        name="norm_matmul",
    )(x, g.reshape(1, d), w)


def _norm_mm_t_kernel(x_ref, g_ref, wt_ref, o_ref):
    h = _rms(x_ref[...], g_ref[...]).astype(BF16)
    res = lax.dot_general(wt_ref[...], h, (((1,), (1,)), ((), ())), preferred_element_type=F32)
    for hh in range(N_HEADS):
        o_ref[0, hh, 0] = res[hh * V_HEAD_DIM:(hh + 1) * V_HEAD_DIM, :].astype(o_ref.dtype)


def _norm_matmul_t(x, g, wt, batch, seq, tk):
    t, d = x.shape
    nk = seq // tk
    return pl.pallas_call(
        _norm_mm_t_kernel,
        grid=(t // tk,),
        in_specs=[
            pl.BlockSpec((tk, d), lambda i: (i, 0)),
            pl.BlockSpec((1, d), lambda i: (0, 0)),
            pl.BlockSpec(wt.shape, lambda i: (0, 0)),
        ],
        out_specs=pl.BlockSpec((1, N_HEADS, 1, V_HEAD_DIM, tk), lambda i: (i // nk, 0, i % nk, 0, 0)),
        out_shape=jax.ShapeDtypeStruct((batch, N_HEADS, nk, V_HEAD_DIM, tk), BF16),
        compiler_params=_cparams(("parallel",)),
        name="norm_matmul_t",
    )(x, g.reshape(1, d), wt)


def _attn_kernel(lam_ref, q_ref, k_ref, vt_ref, bias_ref, g_ref, o_ref, acc_ref, m_ref, l_ref, *, tq, tk):
    i = pl.program_id(2)
    q = q_ref[0].astype(F32) * (LOG2E * HEAD_DIM ** -0.5)
    lane = lax.broadcasted_iota(jnp.int32, q.shape, 1)
    qs = (jnp.where(lane < HEAD_DIM, q, 0.0).astype(BF16),
          jnp.where(lane >= HEAD_DIM, q, 0.0).astype(BF16))

    m_ref[...] = jnp.full(m_ref.shape, NEG_BIG, F32)
    l_ref[...] = jnp.zeros(l_ref.shape, F32)
    acc_ref[...] = jnp.zeros(acc_ref.shape, F32)

    def tile(j, bias):
        row0 = pl.multiple_of(j * tk, tk)
        k = k_ref[0, pl.ds(row0, tk), :]
        vt = vt_ref[0, 0, j]
        for mp in range(2):
            s = lax.dot_general(k, qs[mp], (((1,), (1,)), ((), ())), preferred_element_type=F32)
            if bias is not None:
                s = s + bias
            m_old = m_ref[mp]
            m_new = jnp.maximum(m_old, jnp.max(s, axis=0, keepdims=True))
            alpha = jnp.exp2(m_old - m_new)
            p = jnp.exp2(s - m_new)
            l_ref[mp] = alpha * l_ref[mp] + jnp.sum(p, axis=0, keepdims=True)
            pv = jnp.dot(vt, p.astype(BF16), preferred_element_type=F32)
            acc_ref[mp] = alpha * acc_ref[mp] + pv
            m_ref[mp] = m_new

    def far_body(j, carry):
        tile(j, None)
        return carry

    lax.fori_loop(0, i - 1, far_body, 0)

    @pl.when(i >= 1)
    def _():
        tile(i - 1, bias_ref[0, 1])

    tile(i, bias_ref[0, 0])

    o = acc_ref[0] / l_ref[0] - lam_ref[0] * (acc_ref[1] / l_ref[1])
    ms = jnp.mean(o * o, axis=0, keepdims=True)
    o = o * lax.rsqrt(ms + RMS_EPS) * g_ref[...]
    o_ref[0] = o.T.astype(o_ref.dtype)


def _attention(z3, vt5, bias_tiles, lam, gfull, q_col0, k_col0, tq, tk):
    batch, seq, _ = z3.shape
    nq = seq // tq
    nk = seq // tk
    kern = functools.partial(_attn_kernel, tq=tq, tk=tk)
    return pl.pallas_call(
        kern,
        grid=(batch, N_HEADS, nq),
        in_specs=[
            pl.BlockSpec(memory_space=pltpu.SMEM),
            pl.BlockSpec((1, tq, LANES), lambda b, h, i: (b, i, q_col0 + h)),
            pl.BlockSpec((1, seq, LANES), lambda b, h, i: (b, 0, k_col0 + h)),
            pl.BlockSpec((1, 1, nk, V_HEAD_DIM, tk), lambda b, h, i: (b, h, 0, 0, 0)),
            pl.BlockSpec((1, 2, tk, tq), lambda b, h, i: (h, 0, 0, 0)),
            pl.BlockSpec((V_HEAD_DIM, tq), lambda b, h, i: (0, 0)),
        ],
        out_specs=pl.BlockSpec((1, tq, V_HEAD_DIM), lambda b, h, i: (b, i, h)),
        out_shape=jax.ShapeDtypeStruct((batch, seq, N_HEADS * V_HEAD_DIM), BF16),
        scratch_shapes=[
            pltpu.VMEM((2, V_HEAD_DIM, tq), F32),
            pltpu.VMEM((2, 1, tq), F32),
            pltpu.VMEM((2, 1, tq), F32),
        ],
        compiler_params=_cparams(("parallel", "parallel", "arbitrary")),
        name="diff_attention",
    )(lam, z3, z3, vt5, bias_tiles, gfull)


def _mixer_out_kernel(ch_ref, cb_ref, cc_ref, ga_ref, gc_ref, chp_ref, ccp_ref, a_ref, cw_ref,
                      woa_ref, woc_ref, wo_ref, x_ref, o_ref, *, tm, seq):
    i = pl.program_id(0)
    u = cc_ref[...].astype(F32) * ch_ref[...].astype(F32)
    up = ccp_ref[...].astype(F32) * chp_ref[...].astype(F32)
    up = jnp.where((i * tm) % seq == 0, 0.0, up)
    row = lax.broadcasted_iota(jnp.int32, u.shape, 0)
    u1 = jnp.where(row == 0, up[7:8], pltpu.roll(u, 1, 0))
    u2 = jnp.where(row == 0, up[6:7], jnp.where(row == 1, up[7:8], pltpu.roll(u, 2, 0)))
    cw = cw_ref[...]
    y = cw[0:1] * u2 + cw[1:2] * u1 + cw[2:3] * u
    c = (cb_ref[...].astype(F32) * y).astype(BF16)
    pa = jnp.dot(a_ref[...], woa_ref[...], preferred_element_type=F32)
    pc = jnp.dot(c, woc_ref[...], preferred_element_type=F32)
    merged = jax.nn.sigmoid(ga_ref[...].astype(F32)) * pa + jax.nn.sigmoid(gc_ref[...].astype(F32)) * pc
    o_ref[...] = x_ref[...] + jnp.dot(merged.astype(BF16), wo_ref[...], preferred_element_type=F32)


def _mixer_out(z, a, conv_w, woa, woc, wo, x, cols, tm, seq):
    t, d = x.shape
    w = conv_w.shape[1]
    c_h, c_b, c_c, c_ga, c_gc = cols
    blk = lambda col: pl.BlockSpec((tm, w), lambda i: (i, col))
    prev = lambda col: pl.BlockSpec((8, w), lambda i: (jnp.maximum(i * (tm // 8) - 1, 0), col))
    full = lambda arr: pl.BlockSpec(arr.shape, lambda i: (0, 0))
    kern = functools.partial(_mixer_out_kernel, tm=tm, seq=seq)
    return pl.pallas_call(
        kern,
        grid=(t // tm,),
        in_specs=[blk(c_h), blk(c_b), blk(c_c), blk(c_ga), blk(c_gc), prev(c_h), prev(c_c),
                  pl.BlockSpec((tm, a.shape[1]), lambda i: (i, 0)),
                  full(conv_w), full(woa), full(woc), full(wo),
                  pl.BlockSpec((tm, d), lambda i: (i, 0))],
        out_specs=pl.BlockSpec((tm, d), lambda i: (i, 0)),
        out_shape=jax.ShapeDtypeStruct((t, d), F32),
        compiler_params=_cparams(("parallel",)),
        name="mixer_out",
    )(z, z, z, z, z, z, z, a, conv_w, woa, woc, wo, x)


def _dense_ffn_kernel(x_ref, g_ref, wg_ref, wu_ref, wd_ref, o_ref):
    x = x_ref[...]
    h = _rms(x, g_ref[...]).astype(BF16)
    gate = jnp.dot(h, wg_ref[...], preferred_element_type=F32)
    up = jnp.dot(h, wu_ref[...], preferred_element_type=F32)
    act = (gate * jax.nn.sigmoid(gate) * up).astype(BF16)
    o_ref[...] = x + jnp.dot(act, wd_ref[...], preferred_element_type=F32)


def _dense_ffn(x, g, wg, wu, wd, tm):
    t, d = x.shape
    resident = lambda arr: pl.BlockSpec(arr.shape, lambda i: (0, 0), pipeline_mode=pl.Buffered(1))
    return pl.pallas_call(
        _dense_ffn_kernel,
        grid=(t // tm,),
        in_specs=[pl.BlockSpec((tm, d), lambda i: (i, 0)),
                  pl.BlockSpec((1, d), lambda i: (0, 0)),
                  resident(wg), resident(wu), resident(wd)],
        out_specs=pl.BlockSpec((tm, d), lambda i: (i, 0)),
        out_shape=jax.ShapeDtypeStruct((t, d), F32),
        compiler_params=_cparams(("parallel",)),
        name="dense_ffn",
    )(x, g.reshape(1, d), wg, wu, wd)


def _split_bf16(v):
    hi = v.astype(BF16)
    lo = (v - hi.astype(F32)).astype(BF16)
    return hi, lo


def _router_kernel(x_ref, g_ref, rw_ref, h_ref, r_ref):
    h = _rms(x_ref[...], g_ref[...])
    h_ref[...] = h.astype(BF16)
    h_hi, h_lo = _split_bf16(h)
    w_hi, w_lo = _split_bf16(rw_ref[...])
    logits = (jnp.dot(h_hi, w_hi, preferred_element_type=F32)
              + jnp.dot(h_hi, w_lo, preferred_element_type=F32)
              + jnp.dot(h_lo, w_hi, preferred_element_type=F32))
    lane = lax.broadcasted_iota(jnp.int32, logits.shape, 1)
    logits = jnp.where(lane < N_EXPERTS, logits, -jnp.inf)
    v1 = jnp.max(logits, axis=-1, keepdims=True)
    i1 = jnp.min(jnp.where(logits == v1, lane, LANES), axis=-1, keepdims=True)
    rest = jnp.where(lane == i1, -jnp.inf, logits)
    v2 = jnp.max(rest, axis=-1, keepdims=True)
    i2 = jnp.min(jnp.where(rest == v2, lane, LANES), axis=-1, keepdims=True)
    e2 = jnp.exp(v2 - v1)
    g1 = 1.0 / (1.0 + e2)
    g2 = e2 / (1.0 + e2)
    out = jnp.where(lane == 0, i1.astype(F32),
                    jnp.where(lane == 1, i2.astype(F32),
                              jnp.where(lane == 2, g1, jnp.where(lane == 3, g2, 0.0))))
    r_ref[...] = out


def _router(x, g, rw_pad, tm):
    t, d = x.shape
    return pl.pallas_call(
        _router_kernel,
        grid=(t // tm,),
        in_specs=[pl.BlockSpec((tm, d), lambda i: (i, 0)),
                  pl.BlockSpec((1, d), lambda i: (0, 0)),
                  pl.BlockSpec(rw_pad.shape, lambda i: (0, 0))],
        out_specs=[pl.BlockSpec((tm, d), lambda i: (i, 0)),
                   pl.BlockSpec((tm, LANES), lambda i: (i, 0))],
        out_shape=[jax.ShapeDtypeStruct((t, d), BF16),
                   jax.ShapeDtypeStruct((t, LANES), F32)],
        compiler_params=_cparams(("parallel",)),
        name="moe_router",
    )(x, g.reshape(1, d), rw_pad)


def _gmm_kernel(te_ref, ta_ref, xs_ref, gate_ref, wg_ref, wu_ref, wd_ref, o_ref, acc_ref, *, nch):
    i = pl.program_id(0)
    c = pl.program_id(1)
    active = ta_ref[i] != 0

    @pl.when(active)
    def _():
        x = xs_ref[...]
        gate = jnp.dot(x, wg_ref[0], preferred_element_type=F32)
        up = jnp.dot(x, wu_ref[0], preferred_element_type=F32)
        act = (gate * jax.nn.sigmoid(gate) * up).astype(BF16)
        contrib = jnp.dot(act, wd_ref[0], preferred_element_type=F32)

        @pl.when(c == 0)
        def _():
            acc_ref[...] = contrib

        @pl.when(c != 0)
        def _():
            acc_ref[...] += contrib

        @pl.when(c == nch - 1)
        def _():
            o_ref[...] = (acc_ref[...] * gate_ref[...]).astype(o_ref.dtype)

    @pl.when(jnp.logical_and(jnp.logical_not(active), c == nch - 1))
    def _():
        o_ref[...] = jnp.zeros(o_ref.shape, o_ref.dtype)


def _grouped_ffn(tile_expert, tile_active, xs, row_gate, w_gu, w_down, tm, tf):
    p, d = xs.shape
    ff = w_down.shape[1]
    nch = ff // tf
    kern = functools.partial(_gmm_kernel, nch=nch)

    def chunk(c, ta, i):
        return jnp.where(ta[i] != 0, c, nch - 1)

    grid_spec = pltpu.PrefetchScalarGridSpec(
        num_scalar_prefetch=2,
        grid=(p // tm, nch),
        in_specs=[
            pl.BlockSpec((tm, d), lambda i, c, te, ta: (i, 0)),
            pl.BlockSpec((tm, 1), lambda i, c, te, ta: (i, 0)),
            pl.BlockSpec((1, d, tf), lambda i, c, te, ta: (te[i], 0, chunk(c, ta, i))),
            pl.BlockSpec((1, d, tf), lambda i, c, te, ta: (te[i], 0, nch + chunk(c, ta, i))),
            pl.BlockSpec((1, tf, d), lambda i, c, te, ta: (te[i], chunk(c, ta, i), 0)),
        ],
        out_specs=pl.BlockSpec((tm, d), lambda i, c, te, ta: (i, 0)),
        scratch_shapes=[pltpu.VMEM((tm, d), F32)],
    )
    return pl.pallas_call(
        kern,
        grid_spec=grid_spec,
        out_shape=jax.ShapeDtypeStruct((p, d), BF16),
        compiler_params=_cparams(("arbitrary", "arbitrary")),
        name="moe_grouped_ffn",
    )(tile_expert, tile_active, xs, row_gate, w_gu, w_gu, w_down)


def _combine_kernel(x_ref, y0_ref, y1_ref, g_ref, o_ref, *, final_norm):
    x = x_ref[...] + y0_ref[...].astype(F32) + y1_ref[...].astype(F32)
    o_ref[...] = _rms(x, g_ref[...]) if final_norm else x


def _combine(x, y01, g, tm, final_norm):
    t, d = x.shape
    kern = functools.partial(_combine_kernel, final_norm=final_norm)
    return pl.pallas_call(
        kern,
        grid=(t // tm,),
        in_specs=[pl.BlockSpec((tm, d), lambda i: (i, 0)),
                  pl.BlockSpec((tm, d), lambda i: (i, 0)),
                  pl.BlockSpec((tm, d), lambda i: (i, 1)),
                  pl.BlockSpec((1, d), lambda i: (0, 0))],
        out_specs=pl.BlockSpec((tm, d), lambda i: (i, 0)),
        out_shape=jax.ShapeDtypeStruct((t, d), F32),
        compiler_params=_cparams(("parallel",)),
        name="moe_combine",
    )(x, y01, y01, g.reshape(1, d))


def _final_norm_kernel(x_ref, g_ref, o_ref):
    o_ref[...] = _rms(x_ref[...], g_ref[...])


def _final_norm(x, g, tm):
    t, d = x.shape
    return pl.pallas_call(
        _final_norm_kernel,
        grid=(t // tm,),
        in_specs=[pl.BlockSpec((tm, d), lambda i: (i, 0)), pl.BlockSpec((1, d), lambda i: (0, 0))],
        out_specs=pl.BlockSpec((tm, d), lambda i: (i, 0)),
        out_shape=jax.ShapeDtypeStruct((t, d), F32),
        compiler_params=_cparams(("parallel",)),
        name="final_norm",
    )(x, g.reshape(1, d))


def _t5_causal_bucket(dist):
    max_exact = NUM_BUCKETS // 2
    is_small = dist < max_exact
    d = jnp.maximum(dist, 1).astype(F32)
    large = max_exact + (jnp.log(d / max_exact) / math.log(MAX_DISTANCE / max_exact)
                         * (NUM_BUCKETS - max_exact)).astype(jnp.int32)
    large = jnp.minimum(large, NUM_BUCKETS - 1)
    return jnp.where(is_small, dist, large)


def _bias_tiles(rel_bias, seq, tq, tk):
    assert tq == tk and tk + 1 >= MAX_DISTANCE
    rb = rel_bias.astype(F32)
    bias_dist = rb[_t5_causal_bucket(jnp.arange(seq, dtype=jnp.int32))].T
    far = rb[NUM_BUCKETS - 1][:, None, None]
    d_diag = jnp.arange(tq, dtype=jnp.int32)[None, :] - jnp.arange(tk, dtype=jnp.int32)[:, None]
    d_sub = jnp.minimum(d_diag + tk, seq - 1)
    b_diag = jnp.where(d_diag >= 0, (bias_dist[:, jnp.maximum(d_diag, 0)] - far) * LOG2E, NEG_BIG)
    b_sub = (bias_dist[:, d_sub] - far) * LOG2E
    return jnp.stack([b_diag, b_sub], axis=1)


def _moe_dispatch(route, tm):
    t = route.shape[0]
    expert = route[:, :2].astype(jnp.int32).reshape(-1)
    gate = route[:, 2:4].reshape(-1)
    onehot = (expert[:, None] == jnp.arange(N_EXPERTS, dtype=jnp.int32)[None, :]).astype(jnp.int32)
    rank = jnp.take_along_axis(jnp.cumsum(onehot, axis=0), expert[:, None], axis=1)[:, 0] - 1
    counts = jnp.sum(onehot, axis=0)
    padded = ((counts + tm - 1) // tm) * tm
    ends = jnp.cumsum(padded)
    starts = ends - padded
    pos = starts[expert] + rank
    p = 2 * t + N_EXPERTS * tm
    row_token = jnp.zeros((p,), jnp.int32).at[pos].set(jnp.arange(2 * t, dtype=jnp.int32) // 2)
    row_gate = jnp.zeros((p,), F32).at[pos].set(gate)
    tile_start = jnp.arange(p // tm, dtype=jnp.int32) * tm
    tile_active = (tile_start < ends[-1]).astype(jnp.int32)
    tile_expert = jnp.minimum(jnp.sum((tile_start[:, None] >= ends[None, :]).astype(jnp.int32), axis=1),
                              N_EXPERTS - 1)
    last_expert = tile_expert[jnp.maximum(ends[-1] // tm - 1, 0)]
    tile_expert = jnp.where(tile_active != 0, tile_expert, last_expert).astype(jnp.int32)
    return pos, row_token, row_gate, tile_expert, tile_active


def kernel(x, rel_bias, mix_norm_g, w_in, lam_qk, subln_g, conv_w, w_o_attn, w_o_conv, w_o, ffn_norm_g,
           dense_w_gate_up, dense_w_down, router_w, expert_w_gate_up, expert_w_down, final_norm_g):
    batch, seq, d = x.shape
    t = batch * seq
    depth = w_in.shape[0]
    qk_w = N_HEADS * 2 * HEAD_DIM
    v_w = N_HEADS * V_HEAD_DIM
    tq = min(ATTN_TQ, seq)
    tk = tq
    tm = min(ROW_TILE, seq)

    xf = x.reshape(t, d)
    bias_tiles = _bias_tiles(rel_bias, seq, tq, tk)
    q_col0, k_col0 = 0, qk_w // LANES
    conv_cols = tuple((2 * qk_w + c * d) // d for c in range(5))

    for l in range(depth):
        w = w_in[l]
        w_rest = jnp.concatenate([w[:, :2 * qk_w], w[:, 2 * qk_w + v_w:]], axis=1).astype(BF16)
        w_vt = w[:, 2 * qk_w:2 * qk_w + v_w].T.astype(BF16)
        z = _norm_matmul(xf, mix_norm_g[l], w_rest, tm=min(1024, seq), tn=1024)
        vt5 = _norm_matmul_t(xf, mix_norm_g[l], w_vt, batch, seq, tk)

        lq = lam_qk[l].astype(F32)
        lam_init = 0.8 - 0.6 * math.exp(-0.3 * l)
        lam = (jnp.exp(jnp.sum(lq[0] * lq[1])) - jnp.exp(jnp.sum(lq[2] * lq[3])) + lam_init).reshape(1)
        gsub = subln_g[l].astype(F32) * (1.0 - lam_init)
        gfull = jnp.broadcast_to(gsub[:, None], (V_HEAD_DIM, tq))
        a = _attention(z.reshape(batch, seq, -1), vt5, bias_tiles, lam, gfull, q_col0, k_col0, tq, tk)

        xf = _mixer_out(z, a.reshape(t, v_w), conv_w[l].astype(F32), w_o_attn[l].astype(BF16),
                        w_o_conv[l].astype(BF16), w_o[l].astype(BF16), xf, conv_cols, tm, seq)

        last = l == depth - 1
        if l % 2 == 0:
            wgu = dense_w_gate_up[l // 2]
            ff = wgu.shape[1] // 2
            xf = _dense_ffn(xf, ffn_norm_g[l], wgu[:, :ff].astype(BF16), wgu[:, ff:].astype(BF16),
                            dense_w_down[l // 2].astype(BF16), tm)
            if last:
                xf = _final_norm(xf, final_norm_g, tm)
        else:
            mt = min(MOE_TILE, seq)
            rw_pad = jnp.zeros((d, LANES), F32).at[:, :N_EXPERTS].set(router_w[l // 2].astype(F32))
            hb, route = _router(xf, ffn_norm_g[l], rw_pad, tm)
            pos, row_token, row_gate, tile_expert, tile_active = _moe_dispatch(route, mt)
            xs = hb[row_token]
            ff = expert_w_down.shape[2]
            ys = _grouped_ffn(tile_expert, tile_active, xs, row_gate[:, None],
                              expert_w_gate_up[l // 2].astype(BF16), expert_w_down[l // 2].astype(BF16),
                              mt, ff // 2)
            y01 = ys[pos].reshape(t, 2 * d)
            xf = _combine(xf, y01, final_norm_g, tm, final_norm=last)

    return xf.reshape(batch, seq, d)
```

```python
import functools
import math

import jax
import jax.numpy as jnp
from jax import lax
from jax.experimental import pallas as pl
from jax.experimental.pallas import tpu as pltpu

F32 = jnp.float32
BF16 = jnp.bfloat16

N_HEADS = 8
HEAD_DIM = 64
V_HEAD_DIM = 2 * HEAD_DIM
NUM_BUCKETS = 32
MAX_DISTANCE = 128
N_EXPERTS = 8
RMS_EPS = 1e-6

LANES = 128
VMEM_LIMIT_BYTES = 56 * 1024 * 1024

LOG2E = math.log2(math.e)
NEG_BIG = -1e30

ATTN_TQ = 512
ATTN_TK = 512
ATTN_QCHAIN = 256
ROW_TILE = 512
MOE_TILE = 512


def _cparams(sem):
    return pltpu.CompilerParams(dimension_semantics=sem, vmem_limit_bytes=VMEM_LIMIT_BYTES)


def _rms(x, g):
    ms = jnp.mean(x * x, axis=-1, keepdims=True)
    return x * lax.rsqrt(ms + RMS_EPS) * g


def _norm_mm_kernel(x_ref, g_ref, w_ref, o_ref, h_ref):
    @pl.when(pl.program_id(1) == 0)
    def _():
        h_ref[...] = _rms(x_ref[...], g_ref[...]).astype(BF16)

    o_ref[...] = jnp.dot(h_ref[...], w_ref[...], preferred_element_type=F32).astype(o_ref.dtype)


def _norm_matmul(x, g, w, tm, tn):
    t, d = x.shape
    n = w.shape[1]
    return pl.pallas_call(
        _norm_mm_kernel,
        grid=(t // tm, n // tn),
        in_specs=[
            pl.BlockSpec((tm, d), lambda i, j: (i, 0)),
            pl.BlockSpec((1, d), lambda i, j: (0, 0)),
            pl.BlockSpec((d, tn), lambda i, j: (0, j)),
        ],
        out_specs=pl.BlockSpec((tm, tn), lambda i, j: (i, j)),
        out_shape=jax.ShapeDtypeStruct((t, n), BF16),
        scratch_shapes=[pltpu.VMEM((tm, d), BF16)],
        compiler_params=_cparams(("parallel", "arbitrary")),
        name="norm_matmul",
    )(x, g.reshape(1, d), w)


def _norm_mm_t_kernel(x_ref, g_ref, wt_ref, o_ref):
    h = _rms(x_ref[...], g_ref[...]).astype(BF16)
    res = lax.dot_general(wt_ref[...], h, (((1,), (1,)), ((), ())), preferred_element_type=F32)
    for hh in range(N_HEADS):
        o_ref[0, hh, 0] = res[hh * V_HEAD_DIM:(hh + 1) * V_HEAD_DIM, :].astype(o_ref.dtype)


def _norm_matmul_t(x, g, wt, batch, seq, tk):
    t, d = x.shape
    nk = seq // tk
    return pl.pallas_call(
        _norm_mm_t_kernel,
        grid=(t // tk,),
        in_specs=[
            pl.BlockSpec((tk, d), lambda i: (i, 0)),
            pl.BlockSpec((1, d), lambda i: (0, 0)),
            pl.BlockSpec(wt.shape, lambda i: (0, 0)),
        ],
        out_specs=pl.BlockSpec((1, N_HEADS, 1, V_HEAD_DIM, tk), lambda i: (i // nk, 0, i % nk, 0, 0)),
        out_shape=jax.ShapeDtypeStruct((batch, N_HEADS, nk, V_HEAD_DIM, tk), BF16),
        compiler_params=_cparams(("parallel",)),
        name="norm_matmul_t",
    )(x, g.reshape(1, d), wt)


def _attn_kernel(lam_ref, q_ref, k_ref, vt_ref, ub_ref, g_ref, o_ref,
                 bias_ref, s0_ref, s1_ref, mc0_ref, mc1_ref, acc_ref, m_ref, l_ref, *, tq, tk):
    s_refs = (s0_ref, s1_ref)
    mc_refs = (mc0_ref, mc1_ref)
    i = pl.program_id(2)

    @pl.when(i == 0)
    def _():
        for t in range(2):
            u = jnp.broadcast_to(ub_ref[0, t], (tk, 2 * tq))
            bias_ref[t] = pltpu.roll(u, 1, 1, stride=1, stride_axis=0)[:, tq:]

    q = q_ref[0].astype(F32) * (LOG2E * HEAD_DIM ** -0.5)
    lane = lax.broadcasted_iota(jnp.int32, q.shape, 1)
    qs = (jnp.where(lane < HEAD_DIM, q, 0.0).astype(BF16),
          jnp.where(lane >= HEAD_DIM, q, 0.0).astype(BF16))

    m_ref[...] = jnp.full(m_ref.shape, NEG_BIG, F32)
    l_ref[...] = jnp.zeros(l_ref.shape, F32)
    acc_ref[...] = jnp.zeros(acc_ref.shape, F32)

    def produce(j, bias, slot):
        row0 = pl.multiple_of(j * tk, tk)
        k = k_ref[0, pl.ds(row0, tk), :]
        for mp in range(2):
            s = lax.dot_general(k, qs[mp], (((1,), (1,)), ((), ())), preferred_element_type=F32)
            if bias is not None:
                s = s + bias
            s_refs[slot][mp] = s
            mc_refs[slot][mp] = jnp.max(s, axis=0, keepdims=True)

    def consume(j, slot):
        vt = vt_ref[0, 0, j]
        for mp in range(2):
            for h in range(tq // ATTN_QCHAIN):
                cols = slice(h * ATTN_QCHAIN, (h + 1) * ATTN_QCHAIN)
                m_old = m_ref[mp, :, cols]
                m_new = jnp.maximum(m_old, mc_refs[slot][mp, :, cols])
                alpha = jnp.exp2(m_old - m_new)
                p = jnp.exp2(s_refs[slot][mp, :, cols] - m_new)
                l_ref[mp, :, cols] = alpha * l_ref[mp, :, cols] + jnp.sum(p, axis=0, keepdims=True)
                pv = jnp.dot(vt, p.astype(BF16), preferred_element_type=F32)
                acc_ref[mp, :, cols] = alpha * acc_ref[mp, :, cols] + pv
                m_ref[mp, :, cols] = m_new

    produce(i, bias_ref[0], 0)

    @pl.when(i >= 1)
    def _():
        produce(i - 1, bias_ref[1], 1)
        consume(i, 0)

    n_far = jnp.maximum(i - 1, 0)

    def far_steps(f0, count):
        for t in range(count):
            f = f0 + t
            produce(f, None, t % 2)
            consume(jnp.where(f == 0, i - 1, f - 1), (t + 1) % 2)

    def far_quad(g, carry):
        far_steps(4 * g, 4)
        return carry

    lax.fori_loop(0, n_far // 4, far_quad, 0)
    rem0 = (n_far // 4) * 4

    @pl.when(n_far - rem0 >= 2)
    def _():
        far_steps(rem0, 2)

    @pl.when(jnp.logical_and(n_far >= 1, n_far % 2 == 1))
    def _():
        far_steps(n_far - 1, 1)

    last = jnp.where(i >= 2, i - 2, 0)

    @pl.when(i % 2 == 0)
    def _():
        consume(last, 0)

    @pl.when(i % 2 == 1)
    def _():
        consume(last, 1)

    o = acc_ref[0] / l_ref[0] - lam_ref[0] * (acc_ref[1] / l_ref[1])
    ms = jnp.mean(o * o, axis=0, keepdims=True)
    o = o * lax.rsqrt(ms + RMS_EPS) * g_ref[...]
    o_ref[0] = o.T.astype(o_ref.dtype)


def _attention(z3, vt5, bias_vecs, lam, gfull, q_col0, k_col0, tq, tk):
    batch, seq, _ = z3.shape
    nq = seq // tq
    nk = seq // tk
    kern = functools.partial(_attn_kernel, tq=tq, tk=tk)
    return pl.pallas_call(
        kern,
        grid=(batch, N_HEADS, nq),
        in_specs=[
            pl.BlockSpec(memory_space=pltpu.SMEM),
            pl.BlockSpec((1, tq, LANES), lambda b, h, i: (b, i, q_col0 + h)),
            pl.BlockSpec((1, seq, LANES), lambda b, h, i: (b, 0, k_col0 + h)),
            pl.BlockSpec((1, 1, nk, V_HEAD_DIM, tk), lambda b, h, i: (b, h, 0, 0, 0)),
            pl.BlockSpec((1, 2, 1, 2 * tq), lambda b, h, i: (h, 0, 0, 0)),
            pl.BlockSpec((V_HEAD_DIM, tq), lambda b, h, i: (0, 0)),
        ],
        out_specs=pl.BlockSpec((1, tq, V_HEAD_DIM), lambda b, h, i: (b, i, h)),
        out_shape=jax.ShapeDtypeStruct((batch, seq, N_HEADS * V_HEAD_DIM), BF16),
        scratch_shapes=[
            pltpu.VMEM((2, tk, tq), F32),
            pltpu.VMEM((2, tk, tq), F32),
            pltpu.VMEM((2, tk, tq), F32),
            pltpu.VMEM((2, 1, tq), F32),
            pltpu.VMEM((2, 1, tq), F32),
            pltpu.VMEM((2, V_HEAD_DIM, tq), F32),
            pltpu.VMEM((2, 1, tq), F32),
            pltpu.VMEM((2, 1, tq), F32),
        ],
        compiler_params=_cparams(("parallel", "parallel", "arbitrary")),
        name="diff_attention",
    )(lam, z3, z3, vt5, bias_vecs, gfull)


def _mixer_out_kernel(ch_ref, cb_ref, cc_ref, ga_ref, gc_ref, chp_ref, ccp_ref, a_ref, cw_ref,
                      woa_ref, woc_ref, wo_ref, x_ref, o_ref, *, tm, seq):
    i = pl.program_id(0)
    u = cc_ref[...].astype(F32) * ch_ref[...].astype(F32)
    up = ccp_ref[...].astype(F32) * chp_ref[...].astype(F32)
    up = jnp.where((i * tm) % seq == 0, 0.0, up)
    row = lax.broadcasted_iota(jnp.int32, u.shape, 0)
    u1 = jnp.where(row == 0, up[7:8], pltpu.roll(u, 1, 0))
    u2 = jnp.where(row == 0, up[6:7], jnp.where(row == 1, up[7:8], pltpu.roll(u, 2, 0)))
    cw = cw_ref[...]
    y = cw[0:1] * u2 + cw[1:2] * u1 + cw[2:3] * u
    c = (cb_ref[...].astype(F32) * y).astype(BF16)
    pa = jnp.dot(a_ref[...], woa_ref[...], preferred_element_type=F32)
    pc = jnp.dot(c, woc_ref[...], preferred_element_type=F32)
    merged = jax.nn.sigmoid(ga_ref[...].astype(F32)) * pa + jax.nn.sigmoid(gc_ref[...].astype(F32)) * pc
    o_ref[...] = x_ref[...] + jnp.dot(merged.astype(BF16), wo_ref[...], preferred_element_type=F32)


def _mixer_out(z, a, conv_w, woa, woc, wo, x, cols, tm, seq):
    t, d = x.shape
    w = conv_w.shape[1]
    c_h, c_b, c_c, c_ga, c_gc = cols
    blk = lambda col: pl.BlockSpec((tm, w), lambda i: (i, col))
    prev = lambda col: pl.BlockSpec((8, w), lambda i: (jnp.maximum(i * (tm // 8) - 1, 0), col))
    full = lambda arr: pl.BlockSpec(arr.shape, lambda i: (0, 0))
    kern = functools.partial(_mixer_out_kernel, tm=tm, seq=seq)
    return pl.pallas_call(
        kern,
        grid=(t // tm,),
        in_specs=[blk(c_h), blk(c_b), blk(c_c), blk(c_ga), blk(c_gc), prev(c_h), prev(c_c),
                  pl.BlockSpec((tm, a.shape[1]), lambda i: (i, 0)),
                  full(conv_w), full(woa), full(woc), full(wo),
                  pl.BlockSpec((tm, d), lambda i: (i, 0))],
        out_specs=pl.BlockSpec((tm, d), lambda i: (i, 0)),
        out_shape=jax.ShapeDtypeStruct((t, d), F32),
        compiler_params=_cparams(("parallel",)),
        name="mixer_out",
    )(z, z, z, z, z, z, z, a, conv_w, woa, woc, wo, x)


def _dense_ffn_kernel(x_ref, g_ref, wg_ref, wu_ref, wd_ref, o_ref):
    x = x_ref[...]
    h = _rms(x, g_ref[...]).astype(BF16)
    gate = jnp.dot(h, wg_ref[...], preferred_element_type=F32)
    up = jnp.dot(h, wu_ref[...], preferred_element_type=F32)
    act = (gate * jax.nn.sigmoid(gate) * up).astype(BF16)
    o_ref[...] = x + jnp.dot(act, wd_ref[...], preferred_element_type=F32)


def _dense_ffn(x, g, wg, wu, wd, tm):
    t, d = x.shape
    resident = lambda arr: pl.BlockSpec(arr.shape, lambda i: (0, 0), pipeline_mode=pl.Buffered(1))
    return pl.pallas_call(
        _dense_ffn_kernel,
        grid=(t // tm,),
        in_specs=[pl.BlockSpec((tm, d), lambda i: (i, 0)),
                  pl.BlockSpec((1, d), lambda i: (0, 0)),
                  resident(wg), resident(wu), resident(wd)],
        out_specs=pl.BlockSpec((tm, d), lambda i: (i, 0)),
        out_shape=jax.ShapeDtypeStruct((t, d), F32),
        compiler_params=_cparams(("parallel",)),
        name="dense_ffn",
    )(x, g.reshape(1, d), wg, wu, wd)


def _split_bf16(v):
    hi = v.astype(BF16)
    lo = (v - hi.astype(F32)).astype(BF16)
    return hi, lo


def _router_kernel(x_ref, g_ref, rw_ref, h_ref, r_ref):
    h = _rms(x_ref[...], g_ref[...])
    h_ref[...] = h.astype(BF16)
    h_hi, h_lo = _split_bf16(h)
    w_hi, w_lo = _split_bf16(rw_ref[...])
    logits = (jnp.dot(h_hi, w_hi, preferred_element_type=F32)
              + jnp.dot(h_hi, w_lo, preferred_element_type=F32)
              + jnp.dot(h_lo, w_hi, preferred_element_type=F32))
    lane = lax.broadcasted_iota(jnp.int32, logits.shape, 1)
    logits = jnp.where(lane < N_EXPERTS, logits, -jnp.inf)
    v1 = jnp.max(logits, axis=-1, keepdims=True)
    i1 = jnp.min(jnp.where(logits == v1, lane, LANES), axis=-1, keepdims=True)
    rest = jnp.where(lane == i1, -jnp.inf, logits)
    v2 = jnp.max(rest, axis=-1, keepdims=True)
    i2 = jnp.min(jnp.where(rest == v2, lane, LANES), axis=-1, keepdims=True)
    e2 = jnp.exp(v2 - v1)
    g1 = 1.0 / (1.0 + e2)
    g2 = e2 / (1.0 + e2)
    out = jnp.where(lane == 0, i1.astype(F32),
                    jnp.where(lane == 1, i2.astype(F32),
                              jnp.where(lane == 2, g1, jnp.where(lane == 3, g2, 0.0))))
    r_ref[...] = out


def _router(x, g, rw_pad, tm):
    t, d = x.shape
    return pl.pallas_call(
        _router_kernel,
        grid=(t // tm,),
        in_specs=[pl.BlockSpec((tm, d), lambda i: (i, 0)),
                  pl.BlockSpec((1, d), lambda i: (0, 0)),
                  pl.BlockSpec(rw_pad.shape, lambda i: (0, 0))],
        out_specs=[pl.BlockSpec((tm, d), lambda i: (i, 0)),
                   pl.BlockSpec((tm, LANES), lambda i: (i, 0))],
        out_shape=[jax.ShapeDtypeStruct((t, d), BF16),
                   jax.ShapeDtypeStruct((t, LANES), F32)],
        compiler_params=_cparams(("parallel",)),
        name="moe_router",
    )(x, g.reshape(1, d), rw_pad)


def _gmm_kernel(te_ref, ta_ref, xs_ref, gate_ref, wg_ref, wu_ref, wd_ref, o_ref, acc_ref, *, nch):
    i = pl.program_id(0)
    c = pl.program_id(1)
    active = ta_ref[i] != 0

    @pl.when(active)
    def _():
        x = xs_ref[...]
        gate = jnp.dot(x, wg_ref[0], preferred_element_type=F32)
        up = jnp.dot(x, wu_ref[0], preferred_element_type=F32)
        act = (gate * jax.nn.sigmoid(gate) * up).astype(BF16)
        contrib = jnp.dot(act, wd_ref[0], preferred_element_type=F32)

        @pl.when(c == 0)
        def _():
            acc_ref[...] = contrib

        @pl.when(c != 0)
        def _():
            acc_ref[...] += contrib

        @pl.when(c == nch - 1)
        def _():
            o_ref[...] = (acc_ref[...] * gate_ref[...]).astype(o_ref.dtype)

    @pl.when(jnp.logical_and(jnp.logical_not(active), c == nch - 1))
    def _():
        o_ref[...] = jnp.zeros(o_ref.shape, o_ref.dtype)


def _grouped_ffn(tile_expert, tile_active, xs, row_gate, w_gu, w_down, tm, tf):
    p, d = xs.shape
    ff = w_down.shape[1]
    nch = ff // tf
    kern = functools.partial(_gmm_kernel, nch=nch)

    def chunk(c, ta, i):
        return jnp.where(ta[i] != 0, c, nch - 1)

    grid_spec = pltpu.PrefetchScalarGridSpec(
        num_scalar_prefetch=2,
        grid=(p // tm, nch),
        in_specs=[
            pl.BlockSpec((tm, d), lambda i, c, te, ta: (i, 0)),
            pl.BlockSpec((tm, 1), lambda i, c, te, ta: (i, 0)),
            pl.BlockSpec((1, d, tf), lambda i, c, te, ta: (te[i], 0, chunk(c, ta, i))),
            pl.BlockSpec((1, d, tf), lambda i, c, te, ta: (te[i], 0, nch + chunk(c, ta, i))),
            pl.BlockSpec((1, tf, d), lambda i, c, te, ta: (te[i], chunk(c, ta, i), 0)),
        ],
        out_specs=pl.BlockSpec((tm, d), lambda i, c, te, ta: (i, 0)),
        scratch_shapes=[pltpu.VMEM((tm, d), F32)],
    )
    return pl.pallas_call(
        kern,
        grid_spec=grid_spec,
        out_shape=jax.ShapeDtypeStruct((p, d), BF16),
        compiler_params=_cparams(("arbitrary", "arbitrary")),
        name="moe_grouped_ffn",
    )(tile_expert, tile_active, xs, row_gate, w_gu, w_gu, w_down)


def _combine_kernel(x_ref, y0_ref, y1_ref, g_ref, o_ref, *, final_norm):
    x = x_ref[...] + y0_ref[...].astype(F32) + y1_ref[...].astype(F32)
    o_ref[...] = _rms(x, g_ref[...]) if final_norm else x


def _combine(x, y01, g, tm, final_norm):
    t, d = x.shape
    kern = functools.partial(_combine_kernel, final_norm=final_norm)
    return pl.pallas_call(
        kern,
        grid=(t // tm,),
        in_specs=[pl.BlockSpec((tm, d), lambda i: (i, 0)),
                  pl.BlockSpec((tm, d), lambda i: (i, 0)),
                  pl.BlockSpec((tm, d), lambda i: (i, 1)),
                  pl.BlockSpec((1, d), lambda i: (0, 0))],
        out_specs=pl.BlockSpec((tm, d), lambda i: (i, 0)),
        out_shape=jax.ShapeDtypeStruct((t, d), F32),
        compiler_params=_cparams(("parallel",)),
        name="moe_combine",
    )(x, y01, y01, g.reshape(1, d))


def _final_norm_kernel(x_ref, g_ref, o_ref):
    o_ref[...] = _rms(x_ref[...], g_ref[...])


def _final_norm(x, g, tm):
    t, d = x.shape
    return pl.pallas_call(
        _final_norm_kernel,
        grid=(t // tm,),
        in_specs=[pl.BlockSpec((tm, d), lambda i: (i, 0)), pl.BlockSpec((1, d), lambda i: (0, 0))],
        out_specs=pl.BlockSpec((tm, d), lambda i: (i, 0)),
        out_shape=jax.ShapeDtypeStruct((t, d), F32),
        compiler_params=_cparams(("parallel",)),
        name="final_norm",
    )(x, g.reshape(1, d))


def _t5_causal_bucket(dist):
    max_exact = NUM_BUCKETS // 2
    is_small = dist < max_exact
    d = jnp.maximum(dist, 1).astype(F32)
    large = max_exact + (jnp.log(d / max_exact) / math.log(MAX_DISTANCE / max_exact)
                         * (NUM_BUCKETS - max_exact)).astype(jnp.int32)
    large = jnp.minimum(large, NUM_BUCKETS - 1)
    return jnp.where(is_small, dist, large)


def _bias_vectors(rel_bias, seq, tq, tk):
    assert tq == tk and tk + 1 >= MAX_DISTANCE
    rb = rel_bias.astype(F32)
    bias_dist = rb[_t5_causal_bucket(jnp.arange(seq, dtype=jnp.int32))].T
    rel = (bias_dist - rb[NUM_BUCKETS - 1][:, None]) * LOG2E
    d_diag = jnp.arange(2 * tq, dtype=jnp.int32) - (tk - 1)
    d_sub = jnp.minimum(d_diag + tk, seq - 1)
    v_diag = jnp.where(d_diag >= 0, rel[:, jnp.clip(d_diag, 0, seq - 1)], NEG_BIG)
    v_sub = rel[:, d_sub]
    return jnp.stack([v_diag, v_sub], axis=1)[:, :, None, :]


def _moe_dispatch(route, tm):
    t = route.shape[0]
    expert = route[:, :2].astype(jnp.int32).reshape(-1)
    gate = route[:, 2:4].reshape(-1)
    onehot = (expert[:, None] == jnp.arange(N_EXPERTS, dtype=jnp.int32)[None, :]).astype(jnp.int32)
    rank = jnp.take_along_axis(jnp.cumsum(onehot, axis=0), expert[:, None], axis=1)[:, 0] - 1
    counts = jnp.sum(onehot, axis=0)
    padded = ((counts + tm - 1) // tm) * tm
    ends = jnp.cumsum(padded)
    starts = ends - padded
    pos = starts[expert] + rank
    p = 2 * t + N_EXPERTS * tm
    row_token = jnp.zeros((p,), jnp.int32).at[pos].set(jnp.arange(2 * t, dtype=jnp.int32) // 2)
    row_gate = jnp.zeros((p,), F32).at[pos].set(gate)
    tile_start = jnp.arange(p // tm, dtype=jnp.int32) * tm
    tile_active = (tile_start < ends[-1]).astype(jnp.int32)
    tile_expert = jnp.minimum(jnp.sum((tile_start[:, None] >= ends[None, :]).astype(jnp.int32), axis=1),
                              N_EXPERTS - 1)
    last_expert = tile_expert[jnp.maximum(ends[-1] // tm - 1, 0)]
    tile_expert = jnp.where(tile_active != 0, tile_expert, last_expert).astype(jnp.int32)
    return pos, row_token, row_gate, tile_expert, tile_active


def kernel(x, rel_bias, mix_norm_g, w_in, lam_qk, subln_g, conv_w, w_o_attn, w_o_conv, w_o, ffn_norm_g,
           dense_w_gate_up, dense_w_down, router_w, expert_w_gate_up, expert_w_down, final_norm_g):
    batch, seq, d = x.shape
    t = batch * seq
    depth = w_in.shape[0]
    qk_w = N_HEADS * 2 * HEAD_DIM
    v_w = N_HEADS * V_HEAD_DIM
    tq = min(ATTN_TQ, seq)
    tk = tq
    tm = min(ROW_TILE, seq)

    xf = x.reshape(t, d)
    bias_vecs = _bias_vectors(rel_bias, seq, tq, tk)
    q_col0, k_col0 = 0, qk_w // LANES
    conv_cols = tuple((2 * qk_w + c * d) // d for c in range(5))

    for l in range(depth):
        w = w_in[l]
        w_rest = jnp.concatenate([w[:, :2 * qk_w], w[:, 2 * qk_w + v_w:]], axis=1).astype(BF16)
        w_vt = w[:, 2 * qk_w:2 * qk_w + v_w].T.astype(BF16)
        z = _norm_matmul(xf, mix_norm_g[l], w_rest, tm=min(1024, seq), tn=1024)
        vt5 = _norm_matmul_t(xf, mix_norm_g[l], w_vt, batch, seq, tk)

        lq = lam_qk[l].astype(F32)
        lam_init = 0.8 - 0.6 * math.exp(-0.3 * l)
        lam = (jnp.exp(jnp.sum(lq[0] * lq[1])) - jnp.exp(jnp.sum(lq[2] * lq[3])) + lam_init).reshape(1)
        gsub = subln_g[l].astype(F32) * (1.0 - lam_init)
        gfull = jnp.broadcast_to(gsub[:, None], (V_HEAD_DIM, tq))
        a = _attention(z.reshape(batch, seq, -1), vt5, bias_vecs, lam, gfull, q_col0, k_col0, tq, tk)

        xf = _mixer_out(z, a.reshape(t, v_w), conv_w[l].astype(F32), w_o_attn[l].astype(BF16),
                        w_o_conv[l].astype(BF16), w_o[l].astype(BF16), xf, conv_cols, tm, seq)

        last = l == depth - 1
        if l % 2 == 0:
            wgu = dense_w_gate_up[l // 2]
            ff = wgu.shape[1] // 2
            xf = _dense_ffn(xf, ffn_norm_g[l], wgu[:, :ff].astype(BF16), wgu[:, ff:].astype(BF16),
                            dense_w_down[l // 2].astype(BF16), tm)
            if last:
                xf = _final_norm(xf, final_norm_g, tm)
        else:
            mt = min(MOE_TILE, seq)
            rw_pad = jnp.zeros((d, LANES), F32).at[:, :N_EXPERTS].set(router_w[l // 2].astype(F32))
            hb, route = _router(xf, ffn_norm_g[l], rw_pad, tm)
            pos, row_token, row_gate, tile_expert, tile_active = _moe_dispatch(route, mt)
            xs = hb[row_token]
            ff = expert_w_down.shape[2]
            ys = _grouped_ffn(tile_expert, tile_active, xs, row_gate[:, None],
                              expert_w_gate_up[l // 2].astype(BF16), expert_w_down[l // 2].astype(BF16),
                              mt, ff // 2)
            y01 = ys[pos].reshape(t, 2 * d)
            xf = _combine(xf, y01, final_norm_g, tm, final_norm=last)

    return xf.reshape(batch, seq, d)
```

```python
import functools
import math

import jax
import jax.numpy as jnp
from jax import lax
from jax.experimental import pallas as pl
from jax.experimental.pallas import tpu as pltpu

F32 = jnp.float32
BF16 = jnp.bfloat16

N_HEADS = 8
HEAD_DIM = 64
V_HEAD_DIM = 2 * HEAD_DIM
V_ROWS = V_HEAD_DIM + 16
NUM_BUCKETS = 32
MAX_DISTANCE = 128
N_EXPERTS = 8
RMS_EPS = 1e-6

LANES = 128
VMEM_LIMIT_BYTES = 56 * 1024 * 1024

LOG2E = math.log2(math.e)
NEG_BIG = -1e30

ATTN_TQ = 512
ATTN_TK = 512
ATTN_QCHAIN = 256
ROW_TILE = 512
MOE_TILE = 512


def _cparams(sem):
    return pltpu.CompilerParams(dimension_semantics=sem, vmem_limit_bytes=VMEM_LIMIT_BYTES)


def _rms(x, g):
    ms = jnp.mean(x * x, axis=-1, keepdims=True)
    return x * lax.rsqrt(ms + RMS_EPS) * g


def _norm_mm_kernel(x_ref, g_ref, w_ref, o_ref, h_ref):
    @pl.when(pl.program_id(1) == 0)
    def _():
        h_ref[...] = _rms(x_ref[...], g_ref[...]).astype(BF16)

    o_ref[...] = jnp.dot(h_ref[...], w_ref[...], preferred_element_type=F32).astype(o_ref.dtype)


def _norm_matmul(x, g, w, tm, tn):
    t, d = x.shape
    n = w.shape[1]
    return pl.pallas_call(
        _norm_mm_kernel,
        grid=(t // tm, n // tn),
        in_specs=[
            pl.BlockSpec((tm, d), lambda i, j: (i, 0)),
            pl.BlockSpec((1, d), lambda i, j: (0, 0)),
            pl.BlockSpec((d, tn), lambda i, j: (0, j)),
        ],
        out_specs=pl.BlockSpec((tm, tn), lambda i, j: (i, j)),
        out_shape=jax.ShapeDtypeStruct((t, n), BF16),
        scratch_shapes=[pltpu.VMEM((tm, d), BF16)],
        compiler_params=_cparams(("parallel", "arbitrary")),
        name="norm_matmul",
    )(x, g.reshape(1, d), w)


def _norm_mm_t_kernel(x_ref, g_ref, wt_ref, o_ref):
    h = _rms(x_ref[...], g_ref[...]).astype(BF16)
    res = lax.dot_general(wt_ref[...], h, (((1,), (1,)), ((), ())), preferred_element_type=F32)
    pad_rows = V_ROWS - V_HEAD_DIM
    row = lax.broadcasted_iota(jnp.int32, (pad_rows, res.shape[1]), 0)
    ones_rows = jnp.where(row == 0, 1.0, 0.0).astype(o_ref.dtype)
    for hh in range(N_HEADS):
        o_ref[0, hh, 0, :V_HEAD_DIM, :] = res[hh * V_HEAD_DIM:(hh + 1) * V_HEAD_DIM, :].astype(o_ref.dtype)
        o_ref[0, hh, 0, V_HEAD_DIM:, :] = ones_rows


def _norm_matmul_t(x, g, wt, batch, seq, tk):
    t, d = x.shape
    nk = seq // tk
    return pl.pallas_call(
        _norm_mm_t_kernel,
        grid=(t // tk,),
        in_specs=[
            pl.BlockSpec((tk, d), lambda i: (i, 0)),
            pl.BlockSpec((1, d), lambda i: (0, 0)),
            pl.BlockSpec(wt.shape, lambda i: (0, 0)),
        ],
        out_specs=pl.BlockSpec((1, N_HEADS, 1, V_ROWS, tk), lambda i: (i // nk, 0, i % nk, 0, 0)),
        out_shape=jax.ShapeDtypeStruct((batch, N_HEADS, nk, V_ROWS, tk), BF16),
        compiler_params=_cparams(("parallel",)),
        name="norm_matmul_t",
    )(x, g.reshape(1, d), wt)


def _attn_kernel(lam_ref, q_ref, k_ref, vt_ref, ub_ref, g_ref, o_ref,
                 bias_ref, s0_ref, s1_ref, mc0_ref, mc1_ref, acc_ref, m_ref, *, tq, tk):
    s_refs = (s0_ref, s1_ref)
    mc_refs = (mc0_ref, mc1_ref)
    i = pl.program_id(2)

    @pl.when(i == 0)
    def _():
        for t in range(2):
            u = jnp.broadcast_to(ub_ref[0, t], (tk, 2 * tq))
            bias_ref[t] = pltpu.roll(u, 1, 1, stride=1, stride_axis=0)[:, tq:]

    q = q_ref[0].astype(F32) * (LOG2E * HEAD_DIM ** -0.5)
    lane = lax.broadcasted_iota(jnp.int32, q.shape, 1)
    qs = (jnp.where(lane < HEAD_DIM, q, 0.0).astype(BF16),
          jnp.where(lane >= HEAD_DIM, q, 0.0).astype(BF16))

    m_ref[...] = jnp.full(m_ref.shape, NEG_BIG, F32)
    acc_ref[...] = jnp.zeros(acc_ref.shape, F32)

    def produce(j, bias, slot):
        row0 = pl.multiple_of(j * tk, tk)
        k = k_ref[0, pl.ds(row0, tk), :]
        for mp in range(2):
            s = lax.dot_general(k, qs[mp], (((1,), (1,)), ((), ())), preferred_element_type=F32)
            if bias is not None:
                s = s + bias
            s_refs[slot][mp] = s
            mc_refs[slot][mp] = jnp.max(s, axis=0, keepdims=True)

    def consume(j, slot):
        vt = vt_ref[0, 0, j]
        for mp in range(2):
            for h in range(tq // ATTN_QCHAIN):
                cols = slice(h * ATTN_QCHAIN, (h + 1) * ATTN_QCHAIN)
                m_old = m_ref[mp, :, cols]
                m_new = jnp.maximum(m_old, mc_refs[slot][mp, :, cols])
                alpha = jnp.exp2(m_old - m_new)
                p = jnp.exp2(s_refs[slot][mp, :, cols] - m_new)
                pv = jnp.dot(vt, p.astype(BF16), preferred_element_type=F32)
                acc_ref[mp, :, cols] = alpha * acc_ref[mp, :, cols] + pv
                m_ref[mp, :, cols] = m_new

    produce(i, bias_ref[0], 0)

    @pl.when(i >= 1)
    def _():
        produce(i - 1, bias_ref[1], 1)
        consume(i, 0)

    n_far = jnp.maximum(i - 1, 0)

    def far_steps(f0, count):
        for t in range(count):
            f = f0 + t
            produce(f, None, t % 2)
            consume(jnp.where(f == 0, i - 1, f - 1), (t + 1) % 2)

    def far_quad(g, carry):
        far_steps(4 * g, 4)
        return carry

    lax.fori_loop(0, n_far // 4, far_quad, 0)
    rem0 = (n_far // 4) * 4

    @pl.when(n_far - rem0 >= 2)
    def _():
        far_steps(rem0, 2)

    @pl.when(jnp.logical_and(n_far >= 1, n_far % 2 == 1))
    def _():
        far_steps(n_far - 1, 1)

    last = jnp.where(i >= 2, i - 2, 0)

    @pl.when(i % 2 == 0)
    def _():
        consume(last, 0)

    @pl.when(i % 2 == 1)
    def _():
        consume(last, 1)

    den = slice(V_HEAD_DIM, V_HEAD_DIM + 1)
    o = (acc_ref[0, :V_HEAD_DIM] / acc_ref[0, den]
         - lam_ref[0] * (acc_ref[1, :V_HEAD_DIM] / acc_ref[1, den]))
    ms = jnp.mean(o * o, axis=0, keepdims=True)
    o = o * lax.rsqrt(ms + RMS_EPS) * g_ref[...]
    o_ref[0] = o.T.astype(o_ref.dtype)


def _attention(z3, vt5, bias_vecs, lam, gfull, q_col0, k_col0, tq, tk):
    batch, seq, _ = z3.shape
    nq = seq // tq
    nk = seq // tk
    kern = functools.partial(_attn_kernel, tq=tq, tk=tk)
    return pl.pallas_call(
        kern,
        grid=(batch, N_HEADS, nq),
        in_specs=[
            pl.BlockSpec(memory_space=pltpu.SMEM),
            pl.BlockSpec((1, tq, LANES), lambda b, h, i: (b, i, q_col0 + h)),
            pl.BlockSpec((1, seq, LANES), lambda b, h, i: (b, 0, k_col0 + h)),
            pl.BlockSpec((1, 1, nk, V_ROWS, tk), lambda b, h, i: (b, h, 0, 0, 0)),
            pl.BlockSpec((1, 2, 1, 2 * tq), lambda b, h, i: (h, 0, 0, 0)),
            pl.BlockSpec((V_HEAD_DIM, tq), lambda b, h, i: (0, 0)),
        ],
        out_specs=pl.BlockSpec((1, tq, V_HEAD_DIM), lambda b, h, i: (b, i, h)),
        out_shape=jax.ShapeDtypeStruct((batch, seq, N_HEADS * V_HEAD_DIM), BF16),
        scratch_shapes=[
            pltpu.VMEM((2, tk, tq), F32),
            pltpu.VMEM((2, tk, tq), F32),
            pltpu.VMEM((2, tk, tq), F32),
            pltpu.VMEM((2, 1, tq), F32),
            pltpu.VMEM((2, 1, tq), F32),
            pltpu.VMEM((2, V_ROWS, tq), F32),
            pltpu.VMEM((2, 1, tq), F32),
        ],
        compiler_params=_cparams(("parallel", "parallel", "arbitrary")),
        name="diff_attention",
    )(lam, z3, z3, vt5, bias_vecs, gfull)


def _mixer_out_kernel(ch_ref, cb_ref, cc_ref, ga_ref, gc_ref, chp_ref, ccp_ref, a_ref, cw_ref,
                      woa_ref, woc_ref, wo_ref, x_ref, o_ref, *, tm, seq):
    i = pl.program_id(0)
    u = cc_ref[...].astype(F32) * ch_ref[...].astype(F32)
    up = ccp_ref[...].astype(F32) * chp_ref[...].astype(F32)
    up = jnp.where((i * tm) % seq == 0, 0.0, up)
    row = lax.broadcasted_iota(jnp.int32, u.shape, 0)
    u1 = jnp.where(row == 0, up[7:8], pltpu.roll(u, 1, 0))
    u2 = jnp.where(row == 0, up[6:7], jnp.where(row == 1, up[7:8], pltpu.roll(u, 2, 0)))
    cw = cw_ref[...]
    y = cw[0:1] * u2 + cw[1:2] * u1 + cw[2:3] * u
    c = (cb_ref[...].astype(F32) * y).astype(BF16)
    pa = jnp.dot(a_ref[...], woa_ref[...], preferred_element_type=F32)
    pc = jnp.dot(c, woc_ref[...], preferred_element_type=F32)
    merged = jax.nn.sigmoid(ga_ref[...].astype(F32)) * pa + jax.nn.sigmoid(gc_ref[...].astype(F32)) * pc
    o_ref[...] = x_ref[...] + jnp.dot(merged.astype(BF16), wo_ref[...], preferred_element_type=F32)


def _mixer_out(z, a, conv_w, woa, woc, wo, x, cols, tm, seq):
    t, d = x.shape
    w = conv_w.shape[1]
    c_h, c_b, c_c, c_ga, c_gc = cols
    blk = lambda col: pl.BlockSpec((tm, w), lambda i: (i, col))
    prev = lambda col: pl.BlockSpec((8, w), lambda i: (jnp.maximum(i * (tm // 8) - 1, 0), col))
    full = lambda arr: pl.BlockSpec(arr.shape, lambda i: (0, 0))
    kern = functools.partial(_mixer_out_kernel, tm=tm, seq=seq)
    return pl.pallas_call(
        kern,
        grid=(t // tm,),
        in_specs=[blk(c_h), blk(c_b), blk(c_c), blk(c_ga), blk(c_gc), prev(c_h), prev(c_c),
                  pl.BlockSpec((tm, a.shape[1]), lambda i: (i, 0)),
                  full(conv_w), full(woa), full(woc), full(wo),
                  pl.BlockSpec((tm, d), lambda i: (i, 0))],
        out_specs=pl.BlockSpec((tm, d), lambda i: (i, 0)),
        out_shape=jax.ShapeDtypeStruct((t, d), F32),
        compiler_params=_cparams(("parallel",)),
        name="mixer_out",
    )(z, z, z, z, z, z, z, a, conv_w, woa, woc, wo, x)


def _dense_ffn_kernel(x_ref, g_ref, wg_ref, wu_ref, wd_ref, o_ref):
    x = x_ref[...]
    h = _rms(x, g_ref[...]).astype(BF16)
    gate = jnp.dot(h, wg_ref[...], preferred_element_type=F32)
    up = jnp.dot(h, wu_ref[...], preferred_element_type=F32)
    act = (gate * jax.nn.sigmoid(gate) * up).astype(BF16)
    o_ref[...] = x + jnp.dot(act, wd_ref[...], preferred_element_type=F32)


def _dense_ffn(x, g, wg, wu, wd, tm):
    t, d = x.shape
    resident = lambda arr: pl.BlockSpec(arr.shape, lambda i: (0, 0), pipeline_mode=pl.Buffered(1))
    return pl.pallas_call(
        _dense_ffn_kernel,
        grid=(t // tm,),
        in_specs=[pl.BlockSpec((tm, d), lambda i: (i, 0)),
                  pl.BlockSpec((1, d), lambda i: (0, 0)),
                  resident(wg), resident(wu), resident(wd)],
        out_specs=pl.BlockSpec((tm, d), lambda i: (i, 0)),
        out_shape=jax.ShapeDtypeStruct((t, d), F32),
        compiler_params=_cparams(("parallel",)),
        name="dense_ffn",
    )(x, g.reshape(1, d), wg, wu, wd)


def _split_bf16(v):
    hi = v.astype(BF16)
    lo = (v - hi.astype(F32)).astype(BF16)
    return hi, lo


def _router_kernel(x_ref, g_ref, rw_ref, h_ref, r_ref):
    h = _rms(x_ref[...], g_ref[...])
    h_ref[...] = h.astype(BF16)
    h_hi, h_lo = _split_bf16(h)
    w_hi, w_lo = _split_bf16(rw_ref[...])
    logits = (jnp.dot(h_hi, w_hi, preferred_element_type=F32)
              + jnp.dot(h_hi, w_lo, preferred_element_type=F32)
              + jnp.dot(h_lo, w_hi, preferred_element_type=F32))
    lane = lax.broadcasted_iota(jnp.int32, logits.shape, 1)
    logits = jnp.where(lane < N_EXPERTS, logits, -jnp.inf)
    v1 = jnp.max(logits, axis=-1, keepdims=True)
    i1 = jnp.min(jnp.where(logits == v1, lane, LANES), axis=-1, keepdims=True)
    rest = jnp.where(lane == i1, -jnp.inf, logits)
    v2 = jnp.max(rest, axis=-1, keepdims=True)
    i2 = jnp.min(jnp.where(rest == v2, lane, LANES), axis=-1, keepdims=True)
    e2 = jnp.exp(v2 - v1)
    g1 = 1.0 / (1.0 + e2)
    g2 = e2 / (1.0 + e2)
    out = jnp.where(lane == 0, i1.astype(F32),
                    jnp.where(lane == 1, i2.astype(F32),
                              jnp.where(lane == 2, g1, jnp.where(lane == 3, g2, 0.0))))
    r_ref[...] = out


def _router(x, g, rw_pad, tm):
    t, d = x.shape
    return pl.pallas_call(
        _router_kernel,
        grid=(t // tm,),
        in_specs=[pl.BlockSpec((tm, d), lambda i: (i, 0)),
                  pl.BlockSpec((1, d), lambda i: (0, 0)),
                  pl.BlockSpec(rw_pad.shape, lambda i: (0, 0))],
        out_specs=[pl.BlockSpec((tm, d), lambda i: (i, 0)),
                   pl.BlockSpec((tm, LANES), lambda i: (i, 0))],
        out_shape=[jax.ShapeDtypeStruct((t, d), BF16),
                   jax.ShapeDtypeStruct((t, LANES), F32)],
        compiler_params=_cparams(("parallel",)),
        name="moe_router",
    )(x, g.reshape(1, d), rw_pad)


def _gmm_kernel(te_ref, ta_ref, xs_ref, wg_ref, wu_ref, wd_ref, o_ref, acc_ref, *, nch):
    i = pl.program_id(0)
    c = pl.program_id(1)
    active = ta_ref[i] != 0

    @pl.when(active)
    def _():
        x = xs_ref[...]
        gate = jnp.dot(x, wg_ref[0], preferred_element_type=F32)
        up = jnp.dot(x, wu_ref[0], preferred_element_type=F32)
        act = (gate * jax.nn.sigmoid(gate) * up).astype(BF16)
        contrib = jnp.dot(act, wd_ref[0], preferred_element_type=F32)

        @pl.when(c == 0)
        def _():
            acc_ref[...] = contrib

        @pl.when(c != 0)
        def _():
            acc_ref[...] += contrib

        @pl.when(c == nch - 1)
        def _():
            o_ref[...] = acc_ref[...].astype(o_ref.dtype)

    @pl.when(jnp.logical_and(jnp.logical_not(active), c == nch - 1))
    def _():
        o_ref[...] = jnp.zeros(o_ref.shape, o_ref.dtype)


def _grouped_ffn(tile_expert, tile_active, xs, w_gu, w_down, tm, tf):
    p, d = xs.shape
    ff = w_down.shape[1]
    nch = ff // tf
    kern = functools.partial(_gmm_kernel, nch=nch)

    def chunk(c, ta, i):
        return jnp.where(ta[i] != 0, c, nch - 1)

    grid_spec = pltpu.PrefetchScalarGridSpec(
        num_scalar_prefetch=2,
        grid=(p // tm, nch),
        in_specs=[
            pl.BlockSpec((tm, d), lambda i, c, te, ta: (i, 0)),
            pl.BlockSpec((1, d, tf), lambda i, c, te, ta: (te[i], 0, chunk(c, ta, i))),
            pl.BlockSpec((1, d, tf), lambda i, c, te, ta: (te[i], 0, nch + chunk(c, ta, i))),
            pl.BlockSpec((1, tf, d), lambda i, c, te, ta: (te[i], chunk(c, ta, i), 0)),
        ],
        out_specs=pl.BlockSpec((tm, d), lambda i, c, te, ta: (i, 0)),
        scratch_shapes=[pltpu.VMEM((tm, d), F32)],
    )
    return pl.pallas_call(
        kern,
        grid_spec=grid_spec,
        out_shape=jax.ShapeDtypeStruct((p, d), BF16),
        compiler_params=_cparams(("arbitrary", "arbitrary")),
        name="moe_grouped_ffn",
    )(tile_expert, tile_active, xs, w_gu, w_gu, w_down)


def _combine_kernel(x_ref, r_ref, y0_ref, y1_ref, g_ref, o_ref, *, final_norm):
    route = r_ref[...]
    x = (x_ref[...] + route[:, 2:3] * y0_ref[...].astype(F32)
         + route[:, 3:4] * y1_ref[...].astype(F32))
    o_ref[...] = _rms(x, g_ref[...]) if final_norm else x


def _combine(x, route, ysel, g, tm, final_norm):
    t, d = x.shape
    nt = t // tm
    kern = functools.partial(_combine_kernel, final_norm=final_norm)
    return pl.pallas_call(
        kern,
        grid=(nt,),
        in_specs=[pl.BlockSpec((tm, d), lambda i: (i, 0)),
                  pl.BlockSpec((tm, LANES), lambda i: (i, 0)),
                  pl.BlockSpec((tm, d), lambda i: (i, 0)),
                  pl.BlockSpec((tm, d), lambda i: (nt + i, 0)),
                  pl.BlockSpec((1, d), lambda i: (0, 0))],
        out_specs=pl.BlockSpec((tm, d), lambda i: (i, 0)),
        out_shape=jax.ShapeDtypeStruct((t, d), F32),
        compiler_params=_cparams(("parallel",)),
        name="moe_combine",
    )(x, route, ysel, ysel, g.reshape(1, d))


def _to_bf16_kernel(w_ref, o_ref):
    o_ref[...] = w_ref[...].astype(o_ref.dtype)


def _to_bf16(w, tr, tc):
    e, r, c = w.shape
    return pl.pallas_call(
        _to_bf16_kernel,
        grid=(e, r // tr, c // tc),
        in_specs=[pl.BlockSpec((1, tr, tc), lambda a, i, j: (a, i, j))],
        out_specs=pl.BlockSpec((1, tr, tc), lambda a, i, j: (a, i, j)),
        out_shape=jax.ShapeDtypeStruct(w.shape, BF16),
        compiler_params=_cparams(("parallel", "parallel", "parallel")),
        name="weights_to_bf16",
    )(w)


def _final_norm_kernel(x_ref, g_ref, o_ref):
    o_ref[...] = _rms(x_ref[...], g_ref[...])


def _final_norm(x, g, tm):
    t, d = x.shape
    return pl.pallas_call(
        _final_norm_kernel,
        grid=(t // tm,),
        in_specs=[pl.BlockSpec((tm, d), lambda i: (i, 0)), pl.BlockSpec((1, d), lambda i: (0, 0))],
        out_specs=pl.BlockSpec((tm, d), lambda i: (i, 0)),
        out_shape=jax.ShapeDtypeStruct((t, d), F32),
        compiler_params=_cparams(("parallel",)),
        name="final_norm",
    )(x, g.reshape(1, d))


def _t5_causal_bucket(dist):
    max_exact = NUM_BUCKETS // 2
    is_small = dist < max_exact
    d = jnp.maximum(dist, 1).astype(F32)
    large = max_exact + (jnp.log(d / max_exact) / math.log(MAX_DISTANCE / max_exact)
                         * (NUM_BUCKETS - max_exact)).astype(jnp.int32)
    large = jnp.minimum(large, NUM_BUCKETS - 1)
    return jnp.where(is_small, dist, large)


def _bias_vectors(rel_bias, seq, tq, tk):
    assert tq == tk and tk + 1 >= MAX_DISTANCE
    rb = rel_bias.astype(F32)
    bias_dist = rb[_t5_causal_bucket(jnp.arange(seq, dtype=jnp.int32))].T
    rel = (bias_dist - rb[NUM_BUCKETS - 1][:, None]) * LOG2E
    d_diag = jnp.arange(2 * tq, dtype=jnp.int32) - (tk - 1)
    d_sub = jnp.minimum(d_diag + tk, seq - 1)
    v_diag = jnp.where(d_diag >= 0, rel[:, jnp.clip(d_diag, 0, seq - 1)], NEG_BIG)
    v_sub = rel[:, d_sub]
    return jnp.stack([v_diag, v_sub], axis=1)[:, :, None, :]


def _moe_dispatch(route, tm):
    t = route.shape[0]
    expert = route[:, :2].astype(jnp.int32).reshape(-1)
    onehot = (expert[:, None] == jnp.arange(N_EXPERTS, dtype=jnp.int32)[None, :]).astype(jnp.int32)
    rank = jnp.take_along_axis(jnp.cumsum(onehot, axis=0), expert[:, None], axis=1)[:, 0] - 1
    counts = jnp.sum(onehot, axis=0)
    padded = ((counts + tm - 1) // tm) * tm
    ends = jnp.cumsum(padded)
    starts = ends - padded
    pos = starts[expert] + rank
    p = 2 * t + N_EXPERTS * tm
    row_token = jnp.zeros((p,), jnp.int32).at[pos].set(jnp.arange(2 * t, dtype=jnp.int32) // 2)
    tile_start = jnp.arange(p // tm, dtype=jnp.int32) * tm
    tile_active = (tile_start < ends[-1]).astype(jnp.int32)
    tile_expert = jnp.minimum(jnp.sum((tile_start[:, None] >= ends[None, :]).astype(jnp.int32), axis=1),
                              N_EXPERTS - 1)
    last_expert = tile_expert[jnp.maximum(ends[-1] // tm - 1, 0)]
    tile_expert = jnp.where(tile_active != 0, tile_expert, last_expert).astype(jnp.int32)
    pos_by_choice = pos.reshape(t, 2).T.reshape(-1)
    return pos_by_choice, row_token, tile_expert, tile_active


def kernel(x, rel_bias, mix_norm_g, w_in, lam_qk, subln_g, conv_w, w_o_attn, w_o_conv, w_o, ffn_norm_g,
           dense_w_gate_up, dense_w_down, router_w, expert_w_gate_up, expert_w_down, final_norm_g):
    batch, seq, d = x.shape
    t = batch * seq
    depth = w_in.shape[0]
    qk_w = N_HEADS * 2 * HEAD_DIM
    v_w = N_HEADS * V_HEAD_DIM
    tq = min(ATTN_TQ, seq)
    tk = tq
    tm = min(ROW_TILE, seq)

    xf = x.reshape(t, d)
    bias_vecs = _bias_vectors(rel_bias, seq, tq, tk)
    q_col0, k_col0 = 0, qk_w // LANES
    conv_cols = tuple((2 * qk_w + c * d) // d for c in range(5))

    for l in range(depth):
        w = w_in[l]
        w_rest = jnp.concatenate([w[:, :2 * qk_w], w[:, 2 * qk_w + v_w:]], axis=1).astype(BF16)
        w_vt = w[:, 2 * qk_w:2 * qk_w + v_w].T.astype(BF16)
        z = _norm_matmul(xf, mix_norm_g[l], w_rest, tm=min(1024, seq), tn=1024)
        vt5 = _norm_matmul_t(xf, mix_norm_g[l], w_vt, batch, seq, tk)

        lq = lam_qk[l].astype(F32)
        lam_init = 0.8 - 0.6 * math.exp(-0.3 * l)
        lam = (jnp.exp(jnp.sum(lq[0] * lq[1])) - jnp.exp(jnp.sum(lq[2] * lq[3])) + lam_init).reshape(1)
        gsub = subln_g[l].astype(F32) * (1.0 - lam_init)
        gfull = jnp.broadcast_to(gsub[:, None], (V_HEAD_DIM, tq))
        a = _attention(z.reshape(batch, seq, -1), vt5, bias_vecs, lam, gfull, q_col0, k_col0, tq, tk)

        xf = _mixer_out(z, a.reshape(t, v_w), conv_w[l].astype(F32), w_o_attn[l].astype(BF16),
                        w_o_conv[l].astype(BF16), w_o[l].astype(BF16), xf, conv_cols, tm, seq)

        last = l == depth - 1
        if l % 2 == 0:
            wgu = dense_w_gate_up[l // 2]
            ff = wgu.shape[1] // 2
            xf = _dense_ffn(xf, ffn_norm_g[l], wgu[:, :ff].astype(BF16), wgu[:, ff:].astype(BF16),
                            dense_w_down[l // 2].astype(BF16), tm)
            if last:
                xf = _final_norm(xf, final_norm_g, tm)
        else:
            mt = min(MOE_TILE, seq)
            rw_pad = jnp.zeros((d, LANES), F32).at[:, :N_EXPERTS].set(router_w[l // 2].astype(F32))
            hb, route = _router(xf, ffn_norm_g[l], rw_pad, tm)
            pos_by_choice, row_token, tile_expert, tile_active = _moe_dispatch(route, mt)
            xs = hb[row_token]
            ff = expert_w_down.shape[2]
            ys = _grouped_ffn(tile_expert, tile_active, xs,
                              _to_bf16(expert_w_gate_up[l // 2], d // 2, ff),
                              _to_bf16(expert_w_down[l // 2], ff // 2, d), mt, ff // 2)
            xf = _combine(xf, route, ys[pos_by_choice], final_norm_g, tm, final_norm=last)

    return xf.reshape(batch, seq, d)
```

```python
import functools
import math

import jax
import jax.numpy as jnp
from jax import lax
from jax.experimental import pallas as pl
from jax.experimental.pallas import tpu as pltpu

F32 = jnp.float32
BF16 = jnp.bfloat16

N_HEADS = 8
HEAD_DIM = 64
V_HEAD_DIM = 2 * HEAD_DIM
V_ROWS = V_HEAD_DIM + 16
NUM_BUCKETS = 32
MAX_DISTANCE = 128
N_EXPERTS = 8
RMS_EPS = 1e-6

LANES = 128
VMEM_LIMIT_BYTES = 56 * 1024 * 1024

LOG2E = math.log2(math.e)
NEG_BIG = -1e30

ATTN_TQ = 512
ATTN_TK = 512
ATTN_QCHAIN = 256
ROW_TILE = 512
MOE_TILE = 512


def _cparams(sem):
    return pltpu.CompilerParams(dimension_semantics=sem, vmem_limit_bytes=VMEM_LIMIT_BYTES)


def _rms(x, g):
    ms = jnp.mean(x * x, axis=-1, keepdims=True)
    return x * lax.rsqrt(ms + RMS_EPS) * g


def _in_proj_kernel(x_ref, g_ref, w_ref, q_ref, k_ref, vt_ref, z_ref):
    h = _rms(x_ref[...], g_ref[...]).astype(BF16)

    def proj(blk):
        return jnp.dot(h, w_ref[blk], preferred_element_type=F32)

    for dst, blk in ((q_ref, 0), (k_ref, 1)):
        y = proj(blk).astype(dst.dtype)
        for hh in range(N_HEADS):
            dst[0, hh] = y[:, hh * LANES:(hh + 1) * LANES]

    yt = proj(2).astype(vt_ref.dtype).T
    pad_rows = V_ROWS - V_HEAD_DIM
    row = lax.broadcasted_iota(jnp.int32, (pad_rows, yt.shape[1]), 0)
    ones_rows = jnp.where(row == 0, 1.0, 0.0).astype(vt_ref.dtype)
    for hh in range(N_HEADS):
        vt_ref[0, hh, 0, :V_HEAD_DIM, :] = yt[hh * V_HEAD_DIM:(hh + 1) * V_HEAD_DIM, :]
        vt_ref[0, hh, 0, V_HEAD_DIM:, :] = ones_rows

    width = w_ref.shape[2]
    for c in range(z_ref.shape[1] // width):
        z_ref[:, c * width:(c + 1) * width] = proj(3 + c).astype(z_ref.dtype)


def _in_proj(x, g, w3, batch, seq, tm):
    t, d = x.shape
    nblk, _, width = w3.shape
    assert width == N_HEADS * 2 * HEAD_DIM == N_HEADS * V_HEAD_DIM
    nt = seq // tm
    n_rest = (nblk - 3) * width
    head_spec = pl.BlockSpec((1, N_HEADS, tm, LANES), lambda i: (i // nt, 0, i % nt, 0))
    return pl.pallas_call(
        _in_proj_kernel,
        grid=(t // tm,),
        in_specs=[
            pl.BlockSpec((tm, d), lambda i: (i, 0)),
            pl.BlockSpec((1, d), lambda i: (0, 0)),
            pl.BlockSpec(w3.shape, lambda i: (0, 0, 0), pipeline_mode=pl.Buffered(1)),
        ],
        out_specs=[
            head_spec, head_spec,
            pl.BlockSpec((1, N_HEADS, 1, V_ROWS, tm), lambda i: (i // nt, 0, i % nt, 0, 0)),
            pl.BlockSpec((tm, n_rest), lambda i: (i, 0)),
        ],
        out_shape=[jax.ShapeDtypeStruct((batch, N_HEADS, seq, LANES), BF16),
                   jax.ShapeDtypeStruct((batch, N_HEADS, seq, LANES), BF16),
                   jax.ShapeDtypeStruct((batch, N_HEADS, nt, V_ROWS, tm), BF16),
                   jax.ShapeDtypeStruct((t, n_rest), BF16)],
        compiler_params=_cparams(("parallel",)),
        name="in_proj",
    )(x, g.reshape(1, d), w3)


def _attn_kernel(lam_ref, q_ref, k_ref, vt_ref, ub_ref, g_ref, o_ref,
                 bias_ref, s0_ref, s1_ref, mc0_ref, mc1_ref, acc_ref, m_ref, *, tq, tk):
    s_refs = (s0_ref, s1_ref)
    mc_refs = (mc0_ref, mc1_ref)
    i = pl.program_id(2)

    @pl.when(i == 0)
    def _():
        for t in range(2):
            u = jnp.broadcast_to(ub_ref[0, t], (tk, 2 * tq))
            bias_ref[t] = pltpu.roll(u, 1, 1, stride=1, stride_axis=0)[:, tq:]

    q = q_ref[0, 0].astype(F32) * (LOG2E * HEAD_DIM ** -0.5)
    lane = lax.broadcasted_iota(jnp.int32, q.shape, 1)
    qs = (jnp.where(lane < HEAD_DIM, q, 0.0).astype(BF16),
          jnp.where(lane >= HEAD_DIM, q, 0.0).astype(BF16))

    m_ref[...] = jnp.full(m_ref.shape, NEG_BIG, F32)
    acc_ref[...] = jnp.zeros(acc_ref.shape, F32)

    def produce(j, bias, slot):
        row0 = pl.multiple_of(j * tk, tk)
        k = k_ref[0, 0, pl.ds(row0, tk), :]
        for mp in range(2):
            s = lax.dot_general(k, qs[mp], (((1,), (1,)), ((), ())), preferred_element_type=F32)
            if bias is not None:
                s = s + bias
            s_refs[slot][mp] = s
            mc_refs[slot][mp] = jnp.max(s, axis=0, keepdims=True)

    def consume(j, slot):
        vt = vt_ref[0, 0, j]
        for mp in range(2):
            for h in range(tq // ATTN_QCHAIN):
                cols = slice(h * ATTN_QCHAIN, (h + 1) * ATTN_QCHAIN)
                m_old = m_ref[mp, :, cols]
                m_new = jnp.maximum(m_old, mc_refs[slot][mp, :, cols])
                alpha = jnp.exp2(m_old - m_new)
                p = jnp.exp2(s_refs[slot][mp, :, cols] - m_new)
                pv = jnp.dot(vt, p.astype(BF16), preferred_element_type=F32)
                acc_ref[mp, :, cols] = alpha * acc_ref[mp, :, cols] + pv
                m_ref[mp, :, cols] = m_new

    produce(i, bias_ref[0], 0)

    @pl.when(i >= 1)
    def _():
        produce(i - 1, bias_ref[1], 1)
        consume(i, 0)

    n_far = jnp.maximum(i - 1, 0)

    def far_steps(f0, count):
        for t in range(count):
            f = f0 + t
            produce(f, None, t % 2)
            consume(jnp.where(f == 0, i - 1, f - 1), (t + 1) % 2)

    def far_quad(g, carry):
        far_steps(4 * g, 4)
        return carry

    lax.fori_loop(0, n_far // 4, far_quad, 0)
    rem0 = (n_far // 4) * 4

    @pl.when(n_far - rem0 >= 2)
    def _():
        far_steps(rem0, 2)

    @pl.when(jnp.logical_and(n_far >= 1, n_far % 2 == 1))
    def _():
        far_steps(n_far - 1, 1)

    last = jnp.where(i >= 2, i - 2, 0)

    @pl.when(i % 2 == 0)
    def _():
        consume(last, 0)

    @pl.when(i % 2 == 1)
    def _():
        consume(last, 1)

    den = slice(V_HEAD_DIM, V_HEAD_DIM + 1)
    o = (acc_ref[0, :V_HEAD_DIM] / acc_ref[0, den]
         - lam_ref[0] * (acc_ref[1, :V_HEAD_DIM] / acc_ref[1, den]))
    ms = jnp.mean(o * o, axis=0, keepdims=True)
    o = o * lax.rsqrt(ms + RMS_EPS) * g_ref[...]
    o_ref[0] = o.T.astype(o_ref.dtype)


def _attention(q4, k4, vt5, bias_vecs, lam, gfull, tq, tk):
    batch, _, seq, _ = q4.shape
    nq = seq // tq
    nk = seq // tk
    kern = functools.partial(_attn_kernel, tq=tq, tk=tk)
    return pl.pallas_call(
        kern,
        grid=(batch, N_HEADS, nq),
        in_specs=[
            pl.BlockSpec(memory_space=pltpu.SMEM),
            pl.BlockSpec((1, 1, tq, LANES), lambda b, h, i: (b, h, i, 0)),
            pl.BlockSpec((1, 1, seq, LANES), lambda b, h, i: (b, h, 0, 0)),
            pl.BlockSpec((1, 1, nk, V_ROWS, tk), lambda b, h, i: (b, h, 0, 0, 0)),
            pl.BlockSpec((1, 2, 1, 2 * tq), lambda b, h, i: (h, 0, 0, 0)),
            pl.BlockSpec((V_HEAD_DIM, tq), lambda b, h, i: (0, 0)),
        ],
        out_specs=pl.BlockSpec((1, tq, V_HEAD_DIM), lambda b, h, i: (b, i, h)),
        out_shape=jax.ShapeDtypeStruct((batch, seq, N_HEADS * V_HEAD_DIM), BF16),
        scratch_shapes=[
            pltpu.VMEM((2, tk, tq), F32),
            pltpu.VMEM((2, tk, tq), F32),
            pltpu.VMEM((2, tk, tq), F32),
            pltpu.VMEM((2, 1, tq), F32),
            pltpu.VMEM((2, 1, tq), F32),
            pltpu.VMEM((2, V_ROWS, tq), F32),
            pltpu.VMEM((2, 1, tq), F32),
        ],
        compiler_params=_cparams(("parallel", "parallel", "arbitrary")),
        name="diff_attention",
    )(lam, q4, k4, vt5, bias_vecs, gfull)


def _mixer_out_kernel(ch_ref, cb_ref, cc_ref, ga_ref, gc_ref, chp_ref, ccp_ref, a_ref, cw_ref,
                      woa_ref, woc_ref, wo_ref, x_ref, o_ref, *, tm, seq):
    i = pl.program_id(0)
    u = cc_ref[...].astype(F32) * ch_ref[...].astype(F32)
    up = ccp_ref[...].astype(F32) * chp_ref[...].astype(F32)
    up = jnp.where((i * tm) % seq == 0, 0.0, up)
    row = lax.broadcasted_iota(jnp.int32, u.shape, 0)
    u1 = jnp.where(row == 0, up[7:8], pltpu.roll(u, 1, 0))
    u2 = jnp.where(row == 0, up[6:7], jnp.where(row == 1, up[7:8], pltpu.roll(u, 2, 0)))
    cw = cw_ref[...]
    y = cw[0:1] * u2 + cw[1:2] * u1 + cw[2:3] * u
    c = (cb_ref[...].astype(F32) * y).astype(BF16)
    pa = jnp.dot(a_ref[...], woa_ref[...], preferred_element_type=F32)
    pc = jnp.dot(c, woc_ref[...], preferred_element_type=F32)
    merged = jax.nn.sigmoid(ga_ref[...].astype(F32)) * pa + jax.nn.sigmoid(gc_ref[...].astype(F32)) * pc
    o_ref[...] = x_ref[...] + jnp.dot(merged.astype(BF16), wo_ref[...], preferred_element_type=F32)


def _mixer_out(z, a, conv_w, woa, woc, wo, x, cols, tm, seq):
    t, d = x.shape
    w = conv_w.shape[1]
    c_h, c_b, c_c, c_ga, c_gc = cols
    blk = lambda col: pl.BlockSpec((tm, w), lambda i: (i, col))
    prev = lambda col: pl.BlockSpec((8, w), lambda i: (jnp.maximum(i * (tm // 8) - 1, 0), col))
    full = lambda arr: pl.BlockSpec(arr.shape, lambda i: (0, 0))
    kern = functools.partial(_mixer_out_kernel, tm=tm, seq=seq)
    return pl.pallas_call(
        kern,
        grid=(t // tm,),
        in_specs=[blk(c_h), blk(c_b), blk(c_c), blk(c_ga), blk(c_gc), prev(c_h), prev(c_c),
                  pl.BlockSpec((tm, a.shape[1]), lambda i: (i, 0)),
                  full(conv_w), full(woa), full(woc), full(wo),
                  pl.BlockSpec((tm, d), lambda i: (i, 0))],
        out_specs=pl.BlockSpec((tm, d), lambda i: (i, 0)),
        out_shape=jax.ShapeDtypeStruct((t, d), F32),
        compiler_params=_cparams(("parallel",)),
        name="mixer_out",
    )(z, z, z, z, z, z, z, a, conv_w, woa, woc, wo, x)


def _dense_ffn_kernel(x_ref, g_ref, wg_ref, wu_ref, wd_ref, o_ref):
    x = x_ref[...]
    h = _rms(x, g_ref[...]).astype(BF16)
    gate = jnp.dot(h, wg_ref[...], preferred_element_type=F32)
    up = jnp.dot(h, wu_ref[...], preferred_element_type=F32)
    act = (gate * jax.nn.sigmoid(gate) * up).astype(BF16)
    o_ref[...] = x + jnp.dot(act, wd_ref[...], preferred_element_type=F32)


def _dense_ffn(x, g, wg, wu, wd, tm):
    t, d = x.shape
    resident = lambda arr: pl.BlockSpec(arr.shape, lambda i: (0, 0), pipeline_mode=pl.Buffered(1))
    return pl.pallas_call(
        _dense_ffn_kernel,
        grid=(t // tm,),
        in_specs=[pl.BlockSpec((tm, d), lambda i: (i, 0)),
                  pl.BlockSpec((1, d), lambda i: (0, 0)),
                  resident(wg), resident(wu), resident(wd)],
        out_specs=pl.BlockSpec((tm, d), lambda i: (i, 0)),
        out_shape=jax.ShapeDtypeStruct((t, d), F32),
        compiler_params=_cparams(("parallel",)),
        name="dense_ffn",
    )(x, g.reshape(1, d), wg, wu, wd)


def _split_bf16(v):
    hi = v.astype(BF16)
    lo = (v - hi.astype(F32)).astype(BF16)
    return hi, lo


def _router_kernel(x_ref, g_ref, rw_ref, h_ref, r_ref):
    h = _rms(x_ref[...], g_ref[...])
    h_ref[...] = h.astype(BF16)
    h_hi, h_lo = _split_bf16(h)
    w_hi, w_lo = _split_bf16(rw_ref[...])
    logits = (jnp.dot(h_hi, w_hi, preferred_element_type=F32)
              + jnp.dot(h_hi, w_lo, preferred_element_type=F32)
              + jnp.dot(h_lo, w_hi, preferred_element_type=F32))
    lane = lax.broadcasted_iota(jnp.int32, logits.shape, 1)
    logits = jnp.where(lane < N_EXPERTS, logits, -jnp.inf)
    v1 = jnp.max(logits, axis=-1, keepdims=True)
    i1 = jnp.min(jnp.where(logits == v1, lane, LANES), axis=-1, keepdims=True)
    rest = jnp.where(lane == i1, -jnp.inf, logits)
    v2 = jnp.max(rest, axis=-1, keepdims=True)
    i2 = jnp.min(jnp.where(rest == v2, lane, LANES), axis=-1, keepdims=True)
    e2 = jnp.exp(v2 - v1)
    g1 = 1.0 / (1.0 + e2)
    g2 = e2 / (1.0 + e2)
    out = jnp.where(lane == 0, i1.astype(F32),
                    jnp.where(lane == 1, i2.astype(F32),
                              jnp.where(lane == 2, g1, jnp.where(lane == 3, g2, 0.0))))
    r_ref[...] = out


def _router(x, g, rw_pad, tm):
    t, d = x.shape
    return pl.pallas_call(
        _router_kernel,
        grid=(t // tm,),
        in_specs=[pl.BlockSpec((tm, d), lambda i: (i, 0)),
                  pl.BlockSpec((1, d), lambda i: (0, 0)),
                  pl.BlockSpec(rw_pad.shape, lambda i: (0, 0))],
        out_specs=[pl.BlockSpec((tm, d), lambda i: (i, 0)),
                   pl.BlockSpec((tm, LANES), lambda i: (i, 0))],
        out_shape=[jax.ShapeDtypeStruct((t, d), BF16),
                   jax.ShapeDtypeStruct((t, LANES), F32)],
        compiler_params=_cparams(("parallel",)),
        name="moe_router",
    )(x, g.reshape(1, d), rw_pad)


def _gmm_kernel(te_ref, ta_ref, xs_ref, wg_ref, wu_ref, wd_ref, o_ref, acc_ref, *, nch):
    i = pl.program_id(0)
    c = pl.program_id(1)
    active = ta_ref[i] != 0

    @pl.when(active)
    def _():
        x = xs_ref[...]
        gate = jnp.dot(x, wg_ref[0], preferred_element_type=F32)
        up = jnp.dot(x, wu_ref[0], preferred_element_type=F32)
        act = (gate * jax.nn.sigmoid(gate) * up).astype(BF16)
        contrib = jnp.dot(act, wd_ref[0], preferred_element_type=F32)

        @pl.when(c == 0)
        def _():
            acc_ref[...] = contrib

        @pl.when(c != 0)
        def _():
            acc_ref[...] += contrib

        @pl.when(c == nch - 1)
        def _():
            o_ref[...] = acc_ref[...].astype(o_ref.dtype)

    @pl.when(jnp.logical_and(jnp.logical_not(active), c == nch - 1))
    def _():
        o_ref[...] = jnp.zeros(o_ref.shape, o_ref.dtype)


def _grouped_ffn(tile_expert, tile_active, xs, w_gu, w_down, tm, tf):
    p, d = xs.shape
    ff = w_down.shape[1]
    nch = ff // tf
    kern = functools.partial(_gmm_kernel, nch=nch)

    def chunk(c, ta, i):
        return jnp.where(ta[i] != 0, c, nch - 1)

    grid_spec = pltpu.PrefetchScalarGridSpec(
        num_scalar_prefetch=2,
        grid=(p // tm, nch),
        in_specs=[
            pl.BlockSpec((tm, d), lambda i, c, te, ta: (i, 0)),
            pl.BlockSpec((1, d, tf), lambda i, c, te, ta: (te[i], 0, chunk(c, ta, i))),
            pl.BlockSpec((1, d, tf), lambda i, c, te, ta: (te[i], 0, nch + chunk(c, ta, i))),
            pl.BlockSpec((1, tf, d), lambda i, c, te, ta: (te[i], chunk(c, ta, i), 0)),
        ],
        out_specs=pl.BlockSpec((tm, d), lambda i, c, te, ta: (i, 0)),
        scratch_shapes=[pltpu.VMEM((tm, d), F32)],
    )
    return pl.pallas_call(
        kern,
        grid_spec=grid_spec,
        out_shape=jax.ShapeDtypeStruct((p, d), BF16),
        compiler_params=_cparams(("arbitrary", "arbitrary")),
        name="moe_grouped_ffn",
    )(tile_expert, tile_active, xs, w_gu, w_gu, w_down)


def _combine_kernel(x_ref, r_ref, y0_ref, y1_ref, g_ref, o_ref, *, final_norm):
    route = r_ref[...]
    x = (x_ref[...] + route[:, 2:3] * y0_ref[...].astype(F32)
         + route[:, 3:4] * y1_ref[...].astype(F32))
    o_ref[...] = _rms(x, g_ref[...]) if final_norm else x


def _combine(x, route, ysel, g, tm, final_norm):
    t, d = x.shape
    nt = t // tm
    kern = functools.partial(_combine_kernel, final_norm=final_norm)
    return pl.pallas_call(
        kern,
        grid=(nt,),
        in_specs=[pl.BlockSpec((tm, d), lambda i: (i, 0)),
                  pl.BlockSpec((tm, LANES), lambda i: (i, 0)),
                  pl.BlockSpec((tm, d), lambda i: (i, 0)),
                  pl.BlockSpec((tm, d), lambda i: (nt + i, 0)),
                  pl.BlockSpec((1, d), lambda i: (0, 0))],
        out_specs=pl.BlockSpec((tm, d), lambda i: (i, 0)),
        out_shape=jax.ShapeDtypeStruct((t, d), F32),
        compiler_params=_cparams(("parallel",)),
        name="moe_combine",
    )(x, route, ysel, ysel, g.reshape(1, d))


CAST_STREAMS = 4


def _to_bf16_kernel(*refs):
    o_ref = refs[-1]
    band = refs[0].shape[1]
    for s, w_ref in enumerate(refs[:-1]):
        o_ref[0, s * band:(s + 1) * band, :] = w_ref[0].astype(o_ref.dtype)


def _to_bf16(w, tr, tc):
    e, r, c = w.shape
    band = tr // CAST_STREAMS
    return pl.pallas_call(
        _to_bf16_kernel,
        grid=(e, r // tr, c // tc),
        in_specs=[pl.BlockSpec((1, band, tc), lambda a, i, j, s=s: (a, i * CAST_STREAMS + s, j))
                  for s in range(CAST_STREAMS)],
        out_specs=pl.BlockSpec((1, tr, tc), lambda a, i, j: (a, i, j)),
        out_shape=jax.ShapeDtypeStruct(w.shape, BF16),
        compiler_params=_cparams(("parallel", "parallel", "parallel")),
        name="weights_to_bf16",
    )(*([w] * CAST_STREAMS))


def _final_norm_kernel(x_ref, g_ref, o_ref):
    o_ref[...] = _rms(x_ref[...], g_ref[...])


def _final_norm(x, g, tm):
    t, d = x.shape
    return pl.pallas_call(
        _final_norm_kernel,
        grid=(t // tm,),
        in_specs=[pl.BlockSpec((tm, d), lambda i: (i, 0)), pl.BlockSpec((1, d), lambda i: (0, 0))],
        out_specs=pl.BlockSpec((tm, d), lambda i: (i, 0)),
        out_shape=jax.ShapeDtypeStruct((t, d), F32),
        compiler_params=_cparams(("parallel",)),
        name="final_norm",
    )(x, g.reshape(1, d))


def _t5_causal_bucket(dist):
    max_exact = NUM_BUCKETS // 2
    is_small = dist < max_exact
    d = jnp.maximum(dist, 1).astype(F32)
    large = max_exact + (jnp.log(d / max_exact) / math.log(MAX_DISTANCE / max_exact)
                         * (NUM_BUCKETS - max_exact)).astype(jnp.int32)
    large = jnp.minimum(large, NUM_BUCKETS - 1)
    return jnp.where(is_small, dist, large)


def _bias_vectors(rel_bias, seq, tq, tk):
    assert tq == tk and tk + 1 >= MAX_DISTANCE
    rb = rel_bias.astype(F32)
    bias_dist = rb[_t5_causal_bucket(jnp.arange(seq, dtype=jnp.int32))].T
    rel = (bias_dist - rb[NUM_BUCKETS - 1][:, None]) * LOG2E
    d_diag = jnp.arange(2 * tq, dtype=jnp.int32) - (tk - 1)
    d_sub = jnp.minimum(d_diag + tk, seq - 1)
    v_diag = jnp.where(d_diag >= 0, rel[:, jnp.clip(d_diag, 0, seq - 1)], NEG_BIG)
    v_sub = rel[:, d_sub]
    return jnp.stack([v_diag, v_sub], axis=1)[:, :, None, :]


def _moe_dispatch(route, tm):
    t = route.shape[0]
    expert = route[:, :2].astype(jnp.int32).reshape(-1)
    onehot = (expert[:, None] == jnp.arange(N_EXPERTS, dtype=jnp.int32)[None, :]).astype(jnp.int32)
    rank = jnp.take_along_axis(jnp.cumsum(onehot, axis=0), expert[:, None], axis=1)[:, 0] - 1
    counts = jnp.sum(onehot, axis=0)
    padded = ((counts + tm - 1) // tm) * tm
    ends = jnp.cumsum(padded)
    starts = ends - padded
    pos = starts[expert] + rank
    p = 2 * t + N_EXPERTS * tm
    row_token = jnp.zeros((p,), jnp.int32).at[pos].set(jnp.arange(2 * t, dtype=jnp.int32) // 2)
    tile_start = jnp.arange(p // tm, dtype=jnp.int32) * tm
    tile_active = (tile_start < ends[-1]).astype(jnp.int32)
    tile_expert = jnp.minimum(jnp.sum((tile_start[:, None] >= ends[None, :]).astype(jnp.int32), axis=1),
                              N_EXPERTS - 1)
    last_expert = tile_expert[jnp.maximum(ends[-1] // tm - 1, 0)]
    tile_expert = jnp.where(tile_active != 0, tile_expert, last_expert).astype(jnp.int32)
    pos_by_choice = pos.reshape(t, 2).T.reshape(-1)
    return pos_by_choice, row_token, tile_expert, tile_active


def kernel(x, rel_bias, mix_norm_g, w_in, lam_qk, subln_g, conv_w, w_o_attn, w_o_conv, w_o, ffn_norm_g,
           dense_w_gate_up, dense_w_down, router_w, expert_w_gate_up, expert_w_down, final_norm_g):
    batch, seq, d = x.shape
    t = batch * seq
    depth = w_in.shape[0]
    qk_w = N_HEADS * 2 * HEAD_DIM
    v_w = N_HEADS * V_HEAD_DIM
    tq = min(ATTN_TQ, seq)
    tk = tq
    tm = min(ROW_TILE, seq)

    xf = x.reshape(t, d)
    bias_vecs = _bias_vectors(rel_bias, seq, tq, tk)
    conv_cols = tuple(range(5))

    for l in range(depth):
        w3 = w_in[l].astype(BF16).reshape(d, -1, v_w).transpose(1, 0, 2)
        q4, k4, vt5, z = _in_proj(xf, mix_norm_g[l], w3, batch, seq, tk)

        lq = lam_qk[l].astype(F32)
        lam_init = 0.8 - 0.6 * math.exp(-0.3 * l)
        lam = (jnp.exp(jnp.sum(lq[0] * lq[1])) - jnp.exp(jnp.sum(lq[2] * lq[3])) + lam_init).reshape(1)
        gsub = subln_g[l].astype(F32) * (1.0 - lam_init)
        gfull = jnp.broadcast_to(gsub[:, None], (V_HEAD_DIM, tq))
        a = _attention(q4, k4, vt5, bias_vecs, lam, gfull, tq, tk)

        xf = _mixer_out(z, a.reshape(t, v_w), conv_w[l].astype(F32), w_o_attn[l].astype(BF16),
                        w_o_conv[l].astype(BF16), w_o[l].astype(BF16), xf, conv_cols, tm, seq)

        last = l == depth - 1
        if l % 2 == 0:
            wgu = dense_w_gate_up[l // 2]
            ff = wgu.shape[1] // 2
            xf = _dense_ffn(xf, ffn_norm_g[l], wgu[:, :ff].astype(BF16), wgu[:, ff:].astype(BF16),
                            dense_w_down[l // 2].astype(BF16), tm)
            if last:
                xf = _final_norm(xf, final_norm_g, tm)
        else:
            mt = min(MOE_TILE, seq)
            rw_pad = jnp.zeros((d, LANES), F32).at[:, :N_EXPERTS].set(router_w[l // 2].astype(F32))
            hb, route = _router(xf, ffn_norm_g[l], rw_pad, tm)
            pos_by_choice, row_token, tile_expert, tile_active = _moe_dispatch(route, mt)
            xs = hb[row_token]
            ff = expert_w_down.shape[2]
            ys = _grouped_ffn(tile_expert, tile_active, xs,
                              _to_bf16(expert_w_gate_up[l // 2], d // 2, ff),
                              _to_bf16(expert_w_down[l // 2], ff // 2, d), mt, ff // 2)
            xf = _combine(xf, route, ys[pos_by_choice], final_norm_g, tm, final_norm=last)

    return xf.reshape(batch, seq, d)
```

```python
import functools
import math

import jax
import jax.numpy as jnp
from jax import lax
from jax.experimental import pallas as pl
from jax.experimental.pallas import tpu as pltpu

F32 = jnp.float32
BF16 = jnp.bfloat16

N_HEADS = 8
HEAD_DIM = 64
V_HEAD_DIM = 2 * HEAD_DIM
V_ROWS = V_HEAD_DIM + 16
NUM_BUCKETS = 32
MAX_DISTANCE = 128
N_EXPERTS = 8
RMS_EPS = 1e-6

LANES = 128
VMEM_LIMIT_BYTES = 56 * 1024 * 1024

LOG2E = math.log2(math.e)
NEG_BIG = -1e30

ATTN_TQ = 512
ATTN_TK = 512
ATTN_QCHAIN = 256
ROW_TILE = 512
MOE_TILE = 512


def _cparams(sem):
    return pltpu.CompilerParams(dimension_semantics=sem, vmem_limit_bytes=VMEM_LIMIT_BYTES)


def _rms(x, g):
    ms = jnp.mean(x * x, axis=-1, keepdims=True)
    return x * lax.rsqrt(ms + RMS_EPS) * g


def _in_proj_kernel(x_ref, g_ref, w_ref, q_ref, k_ref, vt_ref, z_ref):
    h = _rms(x_ref[...], g_ref[...]).astype(BF16)

    def proj(blk):
        return jnp.dot(h, w_ref[blk], preferred_element_type=F32)

    for dst, blk in ((q_ref, 0), (k_ref, 1)):
        y = proj(blk).astype(dst.dtype)
        for hh in range(N_HEADS):
            dst[0, hh] = y[:, hh * LANES:(hh + 1) * LANES]

    yt = proj(2).astype(vt_ref.dtype).T
    pad_rows = V_ROWS - V_HEAD_DIM
    row = lax.broadcasted_iota(jnp.int32, (pad_rows, yt.shape[1]), 0)
    ones_rows = jnp.where(row == 0, 1.0, 0.0).astype(vt_ref.dtype)
    for hh in range(N_HEADS):
        vt_ref[0, hh, 0, :V_HEAD_DIM, :] = yt[hh * V_HEAD_DIM:(hh + 1) * V_HEAD_DIM, :]
        vt_ref[0, hh, 0, V_HEAD_DIM:, :] = ones_rows

    width = w_ref.shape[2]
    for c in range(z_ref.shape[1] // width):
        z_ref[:, c * width:(c + 1) * width] = proj(3 + c).astype(z_ref.dtype)


def _in_proj(x, g, w3, batch, seq, tm):
    t, d = x.shape
    nblk, _, width = w3.shape
    assert width == N_HEADS * 2 * HEAD_DIM == N_HEADS * V_HEAD_DIM
    nt = seq // tm
    n_rest = (nblk - 3) * width
    head_spec = pl.BlockSpec((1, N_HEADS, tm, LANES), lambda i: (i // nt, 0, i % nt, 0))
    return pl.pallas_call(
        _in_proj_kernel,
        grid=(t // tm,),
        in_specs=[
            pl.BlockSpec((tm, d), lambda i: (i, 0)),
            pl.BlockSpec((1, d), lambda i: (0, 0)),
            pl.BlockSpec(w3.shape, lambda i: (0, 0, 0), pipeline_mode=pl.Buffered(1)),
        ],
        out_specs=[
            head_spec, head_spec,
            pl.BlockSpec((1, N_HEADS, 1, V_ROWS, tm), lambda i: (i // nt, 0, i % nt, 0, 0)),
            pl.BlockSpec((tm, n_rest), lambda i: (i, 0)),
        ],
        out_shape=[jax.ShapeDtypeStruct((batch, N_HEADS, seq, LANES), BF16),
                   jax.ShapeDtypeStruct((batch, N_HEADS, seq, LANES), BF16),
                   jax.ShapeDtypeStruct((batch, N_HEADS, nt, V_ROWS, tm), BF16),
                   jax.ShapeDtypeStruct((t, n_rest), BF16)],
        compiler_params=_cparams(("parallel",)),
        name="in_proj",
    )(x, g.reshape(1, d), w3)


def _attn_kernel(lam_ref, q_ref, k_ref, vt_ref, ub_ref, g_ref, *refs, tq, tk, n_cast):
    cast_in = refs[:n_cast]
    o_ref = refs[n_cast]
    cast_out = refs[n_cast + 1:2 * n_cast + 1]
    bias_ref, s0_ref, s1_ref, mc0_ref, mc1_ref, acc_ref, m_ref = refs[2 * n_cast + 1:]
    s_refs = (s0_ref, s1_ref)
    mc_refs = (mc0_ref, mc1_ref)
    i = pl.program_id(2)

    for w_ref, wb_ref in zip(cast_in, cast_out):
        wb_ref[...] = w_ref[...].astype(wb_ref.dtype)

    @pl.when(i == 0)
    def _():
        for t in range(2):
            u = jnp.broadcast_to(ub_ref[0, t], (tk, 2 * tq))
            bias_ref[t] = pltpu.roll(u, 1, 1, stride=1, stride_axis=0)[:, tq:]

    q = q_ref[0, 0].astype(F32) * (LOG2E * HEAD_DIM ** -0.5)
    lane = lax.broadcasted_iota(jnp.int32, q.shape, 1)
    qs = (jnp.where(lane < HEAD_DIM, q, 0.0).astype(BF16),
          jnp.where(lane >= HEAD_DIM, q, 0.0).astype(BF16))

    m_ref[...] = jnp.full(m_ref.shape, NEG_BIG, F32)
    acc_ref[...] = jnp.zeros(acc_ref.shape, F32)

    def produce(j, bias, slot):
        row0 = pl.multiple_of(j * tk, tk)
        k = k_ref[0, 0, pl.ds(row0, tk), :]
        for mp in range(2):
            s = lax.dot_general(k, qs[mp], (((1,), (1,)), ((), ())), preferred_element_type=F32)
            if bias is not None:
                s = s + bias
            s_refs[slot][mp] = s
            mc_refs[slot][mp] = jnp.max(s, axis=0, keepdims=True)

    def consume(j, slot):
        vt = vt_ref[0, 0, j]
        for mp in range(2):
            for h in range(tq // ATTN_QCHAIN):
                cols = slice(h * ATTN_QCHAIN, (h + 1) * ATTN_QCHAIN)
                m_old = m_ref[mp, :, cols]
                m_new = jnp.maximum(m_old, mc_refs[slot][mp, :, cols])
                alpha = jnp.exp2(m_old - m_new)
                p = jnp.exp2(s_refs[slot][mp, :, cols] - m_new)
                pv = jnp.dot(vt, p.astype(BF16), preferred_element_type=F32)
                acc_ref[mp, :, cols] = alpha * acc_ref[mp, :, cols] + pv
                m_ref[mp, :, cols] = m_new

    produce(i, bias_ref[0], 0)

    @pl.when(i >= 1)
    def _():
        produce(i - 1, bias_ref[1], 1)
        consume(i, 0)

    n_far = jnp.maximum(i - 1, 0)

    def far_steps(f0, count):
        for t in range(count):
            f = f0 + t
            produce(f, None, t % 2)
            consume(jnp.where(f == 0, i - 1, f - 1), (t + 1) % 2)

    def far_quad(g, carry):
        far_steps(4 * g, 4)
        return carry

    lax.fori_loop(0, n_far // 4, far_quad, 0)
    rem0 = (n_far // 4) * 4

    @pl.when(n_far - rem0 >= 2)
    def _():
        far_steps(rem0, 2)

    @pl.when(jnp.logical_and(n_far >= 1, n_far % 2 == 1))
    def _():
        far_steps(n_far - 1, 1)

    last = jnp.where(i >= 2, i - 2, 0)

    @pl.when(i % 2 == 0)
    def _():
        consume(last, 0)

    @pl.when(i % 2 == 1)
    def _():
        consume(last, 1)

    den = slice(V_HEAD_DIM, V_HEAD_DIM + 1)
    o = (acc_ref[0, :V_HEAD_DIM] / acc_ref[0, den]
         - lam_ref[0] * (acc_ref[1, :V_HEAD_DIM] / acc_ref[1, den]))
    ms = jnp.mean(o * o, axis=0, keepdims=True)
    o = o * lax.rsqrt(ms + RMS_EPS) * g_ref[...]
    o_ref[0] = o.T.astype(o_ref.dtype)


def _attention(q4, k4, vt5, bias_vecs, lam, gfull, tq, tk, cast_weights=()):
    batch, _, seq, _ = q4.shape
    nq = seq // tq
    nk = seq // tk
    n_steps = batch * N_HEADS * nq
    bf16_rows = 16
    cast_specs = []
    for w in cast_weights:
        rows, cols = w.shape
        rb = bf16_rows * pl.cdiv(rows, bf16_rows * n_steps)
        assert rows % rb == 0
        last = rows // rb - 1
        cast_specs.append(pl.BlockSpec(
            (rb, cols), lambda b, h, i, last=last: (jnp.minimum((b * N_HEADS + h) * nq + i, last), 0)))
    kern = functools.partial(_attn_kernel, tq=tq, tk=tk, n_cast=len(cast_weights))
    outs = pl.pallas_call(
        kern,
        grid=(batch, N_HEADS, nq),
        in_specs=[
            pl.BlockSpec(memory_space=pltpu.SMEM),
            pl.BlockSpec((1, 1, tq, LANES), lambda b, h, i: (b, h, i, 0)),
            pl.BlockSpec((1, 1, seq, LANES), lambda b, h, i: (b, h, 0, 0)),
            pl.BlockSpec((1, 1, nk, V_ROWS, tk), lambda b, h, i: (b, h, 0, 0, 0)),
            pl.BlockSpec((1, 2, 1, 2 * tq), lambda b, h, i: (h, 0, 0, 0)),
            pl.BlockSpec((V_HEAD_DIM, tq), lambda b, h, i: (0, 0)),
        ] + cast_specs,
        out_specs=[pl.BlockSpec((1, tq, V_HEAD_DIM), lambda b, h, i: (b, i, h))] + cast_specs,
        out_shape=[jax.ShapeDtypeStruct((batch, seq, N_HEADS * V_HEAD_DIM), BF16)]
        + [jax.ShapeDtypeStruct(w.shape, BF16) for w in cast_weights],
        scratch_shapes=[
            pltpu.VMEM((2, tk, tq), F32),
            pltpu.VMEM((2, tk, tq), F32),
            pltpu.VMEM((2, tk, tq), F32),
            pltpu.VMEM((2, 1, tq), F32),
            pltpu.VMEM((2, 1, tq), F32),
            pltpu.VMEM((2, V_ROWS, tq), F32),
            pltpu.VMEM((2, 1, tq), F32),
        ],
        compiler_params=_cparams(("parallel", "parallel", "arbitrary")),
        name="diff_attention",
    )(lam, q4, k4, vt5, bias_vecs, gfull, *cast_weights)
    return outs[0], tuple(outs[1:])


def _mixer_out_kernel(ch_ref, cb_ref, cc_ref, ga_ref, gc_ref, chp_ref, ccp_ref, a_ref, cw_ref,
                      woa_ref, woc_ref, wo_ref, x_ref, o_ref, *, tm, seq):
    i = pl.program_id(0)
    u = cc_ref[...].astype(F32) * ch_ref[...].astype(F32)
    up = ccp_ref[...].astype(F32) * chp_ref[...].astype(F32)
    up = jnp.where((i * tm) % seq == 0, 0.0, up)
    row = lax.broadcasted_iota(jnp.int32, u.shape, 0)
    u1 = jnp.where(row == 0, up[7:8], pltpu.roll(u, 1, 0))
    u2 = jnp.where(row == 0, up[6:7], jnp.where(row == 1, up[7:8], pltpu.roll(u, 2, 0)))
    cw = cw_ref[...]
    y = cw[0:1] * u2 + cw[1:2] * u1 + cw[2:3] * u
    c = (cb_ref[...].astype(F32) * y).astype(BF16)
    pa = jnp.dot(a_ref[...], woa_ref[...], preferred_element_type=F32)
    pc = jnp.dot(c, woc_ref[...], preferred_element_type=F32)
    merged = jax.nn.sigmoid(ga_ref[...].astype(F32)) * pa + jax.nn.sigmoid(gc_ref[...].astype(F32)) * pc
    o_ref[...] = x_ref[...] + jnp.dot(merged.astype(BF16), wo_ref[...], preferred_element_type=F32)


def _mixer_out(z, a, conv_w, woa, woc, wo, x, cols, tm, seq):
    t, d = x.shape
    w = conv_w.shape[1]
    c_h, c_b, c_c, c_ga, c_gc = cols
    blk = lambda col: pl.BlockSpec((tm, w), lambda i: (i, col))
    prev = lambda col: pl.BlockSpec((8, w), lambda i: (jnp.maximum(i * (tm // 8) - 1, 0), col))
    full = lambda arr: pl.BlockSpec(arr.shape, lambda i: (0, 0))
    kern = functools.partial(_mixer_out_kernel, tm=tm, seq=seq)
    return pl.pallas_call(
        kern,
        grid=(t // tm,),
        in_specs=[blk(c_h), blk(c_b), blk(c_c), blk(c_ga), blk(c_gc), prev(c_h), prev(c_c),
                  pl.BlockSpec((tm, a.shape[1]), lambda i: (i, 0)),
                  full(conv_w), full(woa), full(woc), full(wo),
                  pl.BlockSpec((tm, d), lambda i: (i, 0))],
        out_specs=pl.BlockSpec((tm, d), lambda i: (i, 0)),
        out_shape=jax.ShapeDtypeStruct((t, d), F32),
        compiler_params=_cparams(("parallel",)),
        name="mixer_out",
    )(z, z, z, z, z, z, z, a, conv_w, woa, woc, wo, x)


def _dense_ffn_kernel(x_ref, g_ref, wg_ref, wu_ref, wd_ref, o_ref):
    x = x_ref[...]
    h = _rms(x, g_ref[...]).astype(BF16)
    gate = jnp.dot(h, wg_ref[...], preferred_element_type=F32)
    up = jnp.dot(h, wu_ref[...], preferred_element_type=F32)
    act = (gate * jax.nn.sigmoid(gate) * up).astype(BF16)
    o_ref[...] = x + jnp.dot(act, wd_ref[...], preferred_element_type=F32)


def _dense_ffn(x, g, wg, wu, wd, tm):
    t, d = x.shape
    resident = lambda arr: pl.BlockSpec(arr.shape, lambda i: (0, 0), pipeline_mode=pl.Buffered(1))
    return pl.pallas_call(
        _dense_ffn_kernel,
        grid=(t // tm,),
        in_specs=[pl.BlockSpec((tm, d), lambda i: (i, 0)),
                  pl.BlockSpec((1, d), lambda i: (0, 0)),
                  resident(wg), resident(wu), resident(wd)],
        out_specs=pl.BlockSpec((tm, d), lambda i: (i, 0)),
        out_shape=jax.ShapeDtypeStruct((t, d), F32),
        compiler_params=_cparams(("parallel",)),
        name="dense_ffn",
    )(x, g.reshape(1, d), wg, wu, wd)


def _split_bf16(v):
    hi = v.astype(BF16)
    lo = (v - hi.astype(F32)).astype(BF16)
    return hi, lo


def _router_kernel(x_ref, g_ref, rw_ref, h_ref, r_ref):
    h = _rms(x_ref[...], g_ref[...])
    h_ref[...] = h.astype(BF16)
    h_hi, h_lo = _split_bf16(h)
    w_hi, w_lo = _split_bf16(rw_ref[...])
    logits = (jnp.dot(h_hi, w_hi, preferred_element_type=F32)
              + jnp.dot(h_hi, w_lo, preferred_element_type=F32)
              + jnp.dot(h_lo, w_hi, preferred_element_type=F32))
    lane = lax.broadcasted_iota(jnp.int32, logits.shape, 1)
    logits = jnp.where(lane < N_EXPERTS, logits, -jnp.inf)
    v1 = jnp.max(logits, axis=-1, keepdims=True)
    i1 = jnp.min(jnp.where(logits == v1, lane, LANES), axis=-1, keepdims=True)
    rest = jnp.where(lane == i1, -jnp.inf, logits)
    v2 = jnp.max(rest, axis=-1, keepdims=True)
    i2 = jnp.min(jnp.where(rest == v2, lane, LANES), axis=-1, keepdims=True)
    e2 = jnp.exp(v2 - v1)
    g1 = 1.0 / (1.0 + e2)
    g2 = e2 / (1.0 + e2)
    out = jnp.where(lane == 0, i1.astype(F32),
                    jnp.where(lane == 1, i2.astype(F32),
                              jnp.where(lane == 2, g1, jnp.where(lane == 3, g2, 0.0))))
    r_ref[...] = out


def _router(x, g, rw_pad, tm):
    t, d = x.shape
    return pl.pallas_call(
        _router_kernel,
        grid=(t // tm,),
        in_specs=[pl.BlockSpec((tm, d), lambda i: (i, 0)),
                  pl.BlockSpec((1, d), lambda i: (0, 0)),
                  pl.BlockSpec(rw_pad.shape, lambda i: (0, 0))],
        out_specs=[pl.BlockSpec((tm, d), lambda i: (i, 0)),
                   pl.BlockSpec((tm, LANES), lambda i: (i, 0))],
        out_shape=[jax.ShapeDtypeStruct((t, d), BF16),
                   jax.ShapeDtypeStruct((t, LANES), F32)],
        compiler_params=_cparams(("parallel",)),
        name="moe_router",
    )(x, g.reshape(1, d), rw_pad)


def _gmm_kernel(te_ref, ta_ref, xs_ref, wg_ref, wu_ref, wd_ref, o_ref, acc_ref, *, nch):
    i = pl.program_id(0)
    c = pl.program_id(1)
    active = ta_ref[i] != 0

    @pl.when(active)
    def _():
        x = xs_ref[...]
        gate = jnp.dot(x, wg_ref[0], preferred_element_type=F32)
        up = jnp.dot(x, wu_ref[0], preferred_element_type=F32)
        act = (gate * jax.nn.sigmoid(gate) * up).astype(BF16)
        contrib = jnp.dot(act, wd_ref[0], preferred_element_type=F32)

        @pl.when(c == 0)
        def _():
            acc_ref[...] = contrib

        @pl.when(c != 0)
        def _():
            acc_ref[...] += contrib

        @pl.when(c == nch - 1)
        def _():
            o_ref[...] = acc_ref[...].astype(o_ref.dtype)

    @pl.when(jnp.logical_and(jnp.logical_not(active), c == nch - 1))
    def _():
        o_ref[...] = jnp.zeros(o_ref.shape, o_ref.dtype)


def _grouped_ffn(tile_expert, tile_active, xs, w_gu, w_down, tm, tf):
    p, d = xs.shape
    ff = w_down.shape[1]
    nch = ff // tf
    kern = functools.partial(_gmm_kernel, nch=nch)

    def chunk(c, ta, i):
        return jnp.where(ta[i] != 0, c, nch - 1)

    grid_spec = pltpu.PrefetchScalarGridSpec(
        num_scalar_prefetch=2,
        grid=(p // tm, nch),
        in_specs=[
            pl.BlockSpec((tm, d), lambda i, c, te, ta: (i, 0)),
            pl.BlockSpec((1, d, tf), lambda i, c, te, ta: (te[i], 0, chunk(c, ta, i))),
            pl.BlockSpec((1, d, tf), lambda i, c, te, ta: (te[i], 0, nch + chunk(c, ta, i))),
            pl.BlockSpec((1, tf, d), lambda i, c, te, ta: (te[i], chunk(c, ta, i), 0)),
        ],
        out_specs=pl.BlockSpec((tm, d), lambda i, c, te, ta: (i, 0)),
        scratch_shapes=[pltpu.VMEM((tm, d), F32)],
    )
    return pl.pallas_call(
        kern,
        grid_spec=grid_spec,
        out_shape=jax.ShapeDtypeStruct((p, d), BF16),
        compiler_params=_cparams(("arbitrary", "arbitrary")),
        name="moe_grouped_ffn",
    )(tile_expert, tile_active, xs, w_gu, w_gu, w_down)


def _combine_kernel(x_ref, r_ref, y0_ref, y1_ref, g_ref, o_ref, *, final_norm):
    route = r_ref[...]
    x = (x_ref[...] + route[:, 2:3] * y0_ref[...].astype(F32)
         + route[:, 3:4] * y1_ref[...].astype(F32))
    o_ref[...] = _rms(x, g_ref[...]) if final_norm else x


def _combine(x, route, ysel, g, tm, final_norm):
    t, d = x.shape
    nt = t // tm
    kern = functools.partial(_combine_kernel, final_norm=final_norm)
    return pl.pallas_call(
        kern,
        grid=(nt,),
        in_specs=[pl.BlockSpec((tm, d), lambda i: (i, 0)),
                  pl.BlockSpec((tm, LANES), lambda i: (i, 0)),
                  pl.BlockSpec((tm, d), lambda i: (i, 0)),
                  pl.BlockSpec((tm, d), lambda i: (nt + i, 0)),
                  pl.BlockSpec((1, d), lambda i: (0, 0))],
        out_specs=pl.BlockSpec((tm, d), lambda i: (i, 0)),
        out_shape=jax.ShapeDtypeStruct((t, d), F32),
        compiler_params=_cparams(("parallel",)),
        name="moe_combine",
    )(x, route, ysel, ysel, g.reshape(1, d))


def _final_norm_kernel(x_ref, g_ref, o_ref):
    o_ref[...] = _rms(x_ref[...], g_ref[...])


def _final_norm(x, g, tm):
    t, d = x.shape
    return pl.pallas_call(
        _final_norm_kernel,
        grid=(t // tm,),
        in_specs=[pl.BlockSpec((tm, d), lambda i: (i, 0)), pl.BlockSpec((1, d), lambda i: (0, 0))],
        out_specs=pl.BlockSpec((tm, d), lambda i: (i, 0)),
        out_shape=jax.ShapeDtypeStruct((t, d), F32),
        compiler_params=_cparams(("parallel",)),
        name="final_norm",
    )(x, g.reshape(1, d))


def _t5_causal_bucket(dist):
    max_exact = NUM_BUCKETS // 2
    is_small = dist < max_exact
    d = jnp.maximum(dist, 1).astype(F32)
    large = max_exact + (jnp.log(d / max_exact) / math.log(MAX_DISTANCE / max_exact)
                         * (NUM_BUCKETS - max_exact)).astype(jnp.int32)
    large = jnp.minimum(large, NUM_BUCKETS - 1)
    return jnp.where(is_small, dist, large)


def _bias_vectors(rel_bias, seq, tq, tk):
    assert tq == tk and tk + 1 >= MAX_DISTANCE
    rb = rel_bias.astype(F32)
    bias_dist = rb[_t5_causal_bucket(jnp.arange(seq, dtype=jnp.int32))].T
    rel = (bias_dist - rb[NUM_BUCKETS - 1][:, None]) * LOG2E
    d_diag = jnp.arange(2 * tq, dtype=jnp.int32) - (tk - 1)
    d_sub = jnp.minimum(d_diag + tk, seq - 1)
    v_diag = jnp.where(d_diag >= 0, rel[:, jnp.clip(d_diag, 0, seq - 1)], NEG_BIG)
    v_sub = rel[:, d_sub]
    return jnp.stack([v_diag, v_sub], axis=1)[:, :, None, :]


def _moe_dispatch(route, tm):
    t = route.shape[0]
    expert = route[:, :2].astype(jnp.int32).reshape(-1)
    onehot = (expert[:, None] == jnp.arange(N_EXPERTS, dtype=jnp.int32)[None, :]).astype(jnp.int32)
    rank = jnp.take_along_axis(jnp.cumsum(onehot, axis=0), expert[:, None], axis=1)[:, 0] - 1
    counts = jnp.sum(onehot, axis=0)
    padded = ((counts + tm - 1) // tm) * tm
    ends = jnp.cumsum(padded)
    starts = ends - padded
    pos = starts[expert] + rank
    p = 2 * t + N_EXPERTS * tm
    row_token = jnp.zeros((p,), jnp.int32).at[pos].set(jnp.arange(2 * t, dtype=jnp.int32) // 2)
    tile_start = jnp.arange(p // tm, dtype=jnp.int32) * tm
    tile_active = (tile_start < ends[-1]).astype(jnp.int32)
    tile_expert = jnp.minimum(jnp.sum((tile_start[:, None] >= ends[None, :]).astype(jnp.int32), axis=1),
                              N_EXPERTS - 1)
    last_expert = tile_expert[jnp.maximum(ends[-1] // tm - 1, 0)]
    tile_expert = jnp.where(tile_active != 0, tile_expert, last_expert).astype(jnp.int32)
    pos_by_choice = pos.reshape(t, 2).T.reshape(-1)
    return pos_by_choice, row_token, tile_expert, tile_active


def kernel(x, rel_bias, mix_norm_g, w_in, lam_qk, subln_g, conv_w, w_o_attn, w_o_conv, w_o, ffn_norm_g,
           dense_w_gate_up, dense_w_down, router_w, expert_w_gate_up, expert_w_down, final_norm_g):
    batch, seq, d = x.shape
    t = batch * seq
    depth = w_in.shape[0]
    qk_w = N_HEADS * 2 * HEAD_DIM
    v_w = N_HEADS * V_HEAD_DIM
    tq = min(ATTN_TQ, seq)
    tk = tq
    tm = min(ROW_TILE, seq)

    xf = x.reshape(t, d)
    bias_vecs = _bias_vectors(rel_bias, seq, tq, tk)
    conv_cols = tuple(range(5))

    for l in range(depth):
        w3 = w_in[l].astype(BF16).reshape(d, -1, v_w).transpose(1, 0, 2)
        q4, k4, vt5, z = _in_proj(xf, mix_norm_g[l], w3, batch, seq, tk)

        lq = lam_qk[l].astype(F32)
        lam_init = 0.8 - 0.6 * math.exp(-0.3 * l)
        lam = (jnp.exp(jnp.sum(lq[0] * lq[1])) - jnp.exp(jnp.sum(lq[2] * lq[3])) + lam_init).reshape(1)
        gsub = subln_g[l].astype(F32) * (1.0 - lam_init)
        gfull = jnp.broadcast_to(gsub[:, None], (V_HEAD_DIM, tq))
        cast_weights = ()
        if l % 2 == 0 and l + 1 < depth:
            e = (l + 1) // 2
            cast_weights = (expert_w_gate_up[e].reshape(-1, expert_w_gate_up.shape[-1]),
                            expert_w_down[e].reshape(-1, expert_w_down.shape[-1]))
        a, cast_bf16 = _attention(q4, k4, vt5, bias_vecs, lam, gfull, tq, tk, cast_weights)
        if cast_weights:
            expert_bf16 = (cast_bf16[0].reshape(expert_w_gate_up.shape[1:]),
                           cast_bf16[1].reshape(expert_w_down.shape[1:]))

        xf = _mixer_out(z, a.reshape(t, v_w), conv_w[l].astype(F32), w_o_attn[l].astype(BF16),
                        w_o_conv[l].astype(BF16), w_o[l].astype(BF16), xf, conv_cols, tm, seq)

        last = l == depth - 1
        if l % 2 == 0:
            wgu = dense_w_gate_up[l // 2]
            ff = wgu.shape[1] // 2
            xf = _dense_ffn(xf, ffn_norm_g[l], wgu[:, :ff].astype(BF16), wgu[:, ff:].astype(BF16),
                            dense_w_down[l // 2].astype(BF16), tm)
            if last:
                xf = _final_norm(xf, final_norm_g, tm)
        else:
            mt = min(MOE_TILE, seq)
            rw_pad = jnp.zeros((d, LANES), F32).at[:, :N_EXPERTS].set(router_w[l // 2].astype(F32))
            hb, route = _router(xf, ffn_norm_g[l], rw_pad, tm)
            pos_by_choice, row_token, tile_expert, tile_active = _moe_dispatch(route, mt)
            xs = hb[row_token]
            ff = expert_w_down.shape[2]
            ys = _grouped_ffn(tile_expert, tile_active, xs, expert_bf16[0], expert_bf16[1], mt, ff // 2)
            xf = _combine(xf, route, ys[pos_by_choice], final_norm_g, tm, final_norm=last)

    return xf.reshape(batch, seq, d)
```

```python
import functools
import math

import jax
import jax.numpy as jnp
from jax import lax
from jax.experimental import pallas as pl
from jax.experimental.pallas import tpu as pltpu

F32 = jnp.float32
BF16 = jnp.bfloat16

N_HEADS = 8
HEAD_DIM = 64
V_HEAD_DIM = 2 * HEAD_DIM
V_ROWS = V_HEAD_DIM + 16
NUM_BUCKETS = 32
MAX_DISTANCE = 128
N_EXPERTS = 8
RMS_EPS = 1e-6

LANES = 128
VMEM_LIMIT_BYTES = 56 * 1024 * 1024

LOG2E = math.log2(math.e)
NEG_BIG = -1e30

ATTN_TQ = 512
ATTN_TK = 512
ATTN_QCHAIN = 256
ROW_TILE = 512
MOE_TILE = 512
MOE_PARTS = 4


def _cparams(sem):
    return pltpu.CompilerParams(dimension_semantics=sem, vmem_limit_bytes=VMEM_LIMIT_BYTES)


def _rms(x, g):
    ms = jnp.mean(x * x, axis=-1, keepdims=True)
    return x * lax.rsqrt(ms + RMS_EPS) * g


def _in_proj_kernel(x_ref, g_ref, w_ref, q_ref, k_ref, vt_ref, z_ref):
    h = _rms(x_ref[...], g_ref[...]).astype(BF16)

    def proj(blk):
        return jnp.dot(h, w_ref[blk], preferred_element_type=F32)

    for dst, blk in ((q_ref, 0), (k_ref, 1)):
        y = proj(blk).astype(dst.dtype)
        for hh in range(N_HEADS):
            dst[0, hh] = y[:, hh * LANES:(hh + 1) * LANES]

    yt = proj(2).astype(vt_ref.dtype).T
    pad_rows = V_ROWS - V_HEAD_DIM
    row = lax.broadcasted_iota(jnp.int32, (pad_rows, yt.shape[1]), 0)
    ones_rows = jnp.where(row == 0, 1.0, 0.0).astype(vt_ref.dtype)
    for hh in range(N_HEADS):
        vt_ref[0, hh, 0, :V_HEAD_DIM, :] = yt[hh * V_HEAD_DIM:(hh + 1) * V_HEAD_DIM, :]
        vt_ref[0, hh, 0, V_HEAD_DIM:, :] = ones_rows

    width = w_ref.shape[2]
    for c in range(z_ref.shape[1] // width):
        z_ref[:, c * width:(c + 1) * width] = proj(3 + c).astype(z_ref.dtype)


def _in_proj(x, g, w3, batch, seq, tm):
    t, d = x.shape
    nblk, _, width = w3.shape
    assert width == N_HEADS * 2 * HEAD_DIM == N_HEADS * V_HEAD_DIM
    nt = seq // tm
    n_rest = (nblk - 3) * width
    head_spec = pl.BlockSpec((1, N_HEADS, tm, LANES), lambda i: (i // nt, 0, i % nt, 0))
    return pl.pallas_call(
        _in_proj_kernel,
        grid=(t // tm,),
        in_specs=[
            pl.BlockSpec((tm, d), lambda i: (i, 0)),
            pl.BlockSpec((1, d), lambda i: (0, 0)),
            pl.BlockSpec(w3.shape, lambda i: (0, 0, 0), pipeline_mode=pl.Buffered(1)),
        ],
        out_specs=[
            head_spec, head_spec,
            pl.BlockSpec((1, N_HEADS, 1, V_ROWS, tm), lambda i: (i // nt, 0, i % nt, 0, 0)),
            pl.BlockSpec((tm, n_rest), lambda i: (i, 0)),
        ],
        out_shape=[jax.ShapeDtypeStruct((batch, N_HEADS, seq, LANES), BF16),
                   jax.ShapeDtypeStruct((batch, N_HEADS, seq, LANES), BF16),
                   jax.ShapeDtypeStruct((batch, N_HEADS, nt, V_ROWS, tm), BF16),
                   jax.ShapeDtypeStruct((t, n_rest), BF16)],
        compiler_params=_cparams(("parallel",)),
        name="in_proj",
    )(x, g.reshape(1, d), w3)


def _attn_kernel(lam_ref, q_ref, k_ref, vt_ref, ub_ref, g_ref, *refs, tq, tk, n_cast):
    cast_in = refs[:n_cast]
    o_ref = refs[n_cast]
    cast_out = refs[n_cast + 1:2 * n_cast + 1]
    bias_ref, s0_ref, s1_ref, mc0_ref, mc1_ref, acc_ref, m_ref = refs[2 * n_cast + 1:]
    s_refs = (s0_ref, s1_ref)
    mc_refs = (mc0_ref, mc1_ref)
    i = pl.program_id(2)

    for w_ref, wb_ref in zip(cast_in, cast_out):
        wb_ref[...] = w_ref[...].astype(wb_ref.dtype)

    @pl.when(i == 0)
    def _():
        for t in range(2):
            u = jnp.broadcast_to(ub_ref[0, t], (tk, 2 * tq))
            bias_ref[t] = pltpu.roll(u, 1, 1, stride=1, stride_axis=0)[:, tq:]

    q = q_ref[0, 0].astype(F32) * (LOG2E * HEAD_DIM ** -0.5)
    lane = lax.broadcasted_iota(jnp.int32, q.shape, 1)
    qs = (jnp.where(lane < HEAD_DIM, q, 0.0).astype(BF16),
          jnp.where(lane >= HEAD_DIM, q, 0.0).astype(BF16))

    m_ref[...] = jnp.full(m_ref.shape, NEG_BIG, F32)
    acc_ref[...] = jnp.zeros(acc_ref.shape, F32)

    def produce(j, bias, slot):
        row0 = pl.multiple_of(j * tk, tk)
        k = k_ref[0, 0, pl.ds(row0, tk), :]
        for mp in range(2):
            s = lax.dot_general(k, qs[mp], (((1,), (1,)), ((), ())), preferred_element_type=F32)
            if bias is not None:
                s = s + bias
            s_refs[slot][mp] = s
            mc_refs[slot][mp] = jnp.max(s, axis=0, keepdims=True)

    def consume(j, slot):
        vt = vt_ref[0, 0, j]
        for mp in range(2):
            for h in range(tq // ATTN_QCHAIN):
                cols = slice(h * ATTN_QCHAIN, (h + 1) * ATTN_QCHAIN)
                m_old = m_ref[mp, :, cols]
                m_new = jnp.maximum(m_old, mc_refs[slot][mp, :, cols])
                alpha = jnp.exp2(m_old - m_new)
                p = jnp.exp2(s_refs[slot][mp, :, cols] - m_new)
                pv = jnp.dot(vt, p.astype(BF16), preferred_element_type=F32)
                acc_ref[mp, :, cols] = alpha * acc_ref[mp, :, cols] + pv
                m_ref[mp, :, cols] = m_new

    produce(i, bias_ref[0], 0)

    @pl.when(i >= 1)
    def _():
        produce(i - 1, bias_ref[1], 1)
        consume(i, 0)

    n_far = jnp.maximum(i - 1, 0)

    def far_steps(f0, count):
        for t in range(count):
            f = f0 + t
            produce(f, None, t % 2)
            consume(jnp.where(f == 0, i - 1, f - 1), (t + 1) % 2)

    def far_quad(g, carry):
        far_steps(4 * g, 4)
        return carry

    lax.fori_loop(0, n_far // 4, far_quad, 0)
    rem0 = (n_far // 4) * 4

    @pl.when(n_far - rem0 >= 2)
    def _():
        far_steps(rem0, 2)

    @pl.when(jnp.logical_and(n_far >= 1, n_far % 2 == 1))
    def _():
        far_steps(n_far - 1, 1)

    last = jnp.where(i >= 2, i - 2, 0)

    @pl.when(i % 2 == 0)
    def _():
        consume(last, 0)

    @pl.when(i % 2 == 1)
    def _():
        consume(last, 1)

    den = slice(V_HEAD_DIM, V_HEAD_DIM + 1)
    o = (acc_ref[0, :V_HEAD_DIM] / acc_ref[0, den]
         - lam_ref[0] * (acc_ref[1, :V_HEAD_DIM] / acc_ref[1, den]))
    ms = jnp.mean(o * o, axis=0, keepdims=True)
    o = o * lax.rsqrt(ms + RMS_EPS) * g_ref[...]
    o_ref[0] = o.T.astype(o_ref.dtype)


def _attention(q4, k4, vt5, bias_vecs, lam, gfull, tq, tk, cast_weights=()):
    batch, _, seq, _ = q4.shape
    nq = seq // tq
    nk = seq // tk
    n_steps = batch * N_HEADS * nq
    bf16_rows = 16
    cast_specs = []
    for w in cast_weights:
        rows, cols = w.shape
        rb = bf16_rows * pl.cdiv(rows, bf16_rows * n_steps)
        assert rows % rb == 0
        last = rows // rb - 1
        cast_specs.append(pl.BlockSpec(
            (rb, cols), lambda b, h, i, last=last: (jnp.minimum((b * N_HEADS + h) * nq + i, last), 0)))
    kern = functools.partial(_attn_kernel, tq=tq, tk=tk, n_cast=len(cast_weights))
    outs = pl.pallas_call(
        kern,
        grid=(batch, N_HEADS, nq),
        in_specs=[
            pl.BlockSpec(memory_space=pltpu.SMEM),
            pl.BlockSpec((1, 1, tq, LANES), lambda b, h, i: (b, h, i, 0)),
            pl.BlockSpec((1, 1, seq, LANES), lambda b, h, i: (b, h, 0, 0)),
            pl.BlockSpec((1, 1, nk, V_ROWS, tk), lambda b, h, i: (b, h, 0, 0, 0)),
            pl.BlockSpec((1, 2, 1, 2 * tq), lambda b, h, i: (h, 0, 0, 0)),
            pl.BlockSpec((V_HEAD_DIM, tq), lambda b, h, i: (0, 0)),
        ] + cast_specs,
        out_specs=[pl.BlockSpec((1, tq, V_HEAD_DIM), lambda b, h, i: (b, i, h))] + cast_specs,
        out_shape=[jax.ShapeDtypeStruct((batch, seq, N_HEADS * V_HEAD_DIM), BF16)]
        + [jax.ShapeDtypeStruct(w.shape, BF16) for w in cast_weights],
        scratch_shapes=[
            pltpu.VMEM((2, tk, tq), F32),
            pltpu.VMEM((2, tk, tq), F32),
            pltpu.VMEM((2, tk, tq), F32),
            pltpu.VMEM((2, 1, tq), F32),
            pltpu.VMEM((2, 1, tq), F32),
            pltpu.VMEM((2, V_ROWS, tq), F32),
            pltpu.VMEM((2, 1, tq), F32),
        ],
        compiler_params=_cparams(("parallel", "parallel", "arbitrary")),
        name="diff_attention",
    )(lam, q4, k4, vt5, bias_vecs, gfull, *cast_weights)
    return outs[0], tuple(outs[1:])


def _mixer_out_kernel(ch_ref, cb_ref, cc_ref, ga_ref, gc_ref, chp_ref, ccp_ref, a_ref, cw_ref,
                      woa_ref, woc_ref, wo_ref, x_ref, o_ref, *, tm, seq):
    i = pl.program_id(0)
    u = cc_ref[...].astype(F32) * ch_ref[...].astype(F32)
    up = ccp_ref[...].astype(F32) * chp_ref[...].astype(F32)
    up = jnp.where((i * tm) % seq == 0, 0.0, up)
    row = lax.broadcasted_iota(jnp.int32, u.shape, 0)
    u1 = jnp.where(row == 0, up[7:8], pltpu.roll(u, 1, 0))
    u2 = jnp.where(row == 0, up[6:7], jnp.where(row == 1, up[7:8], pltpu.roll(u, 2, 0)))
    cw = cw_ref[...]
    y = cw[0:1] * u2 + cw[1:2] * u1 + cw[2:3] * u
    c = (cb_ref[...].astype(F32) * y).astype(BF16)
    pa = jnp.dot(a_ref[...], woa_ref[...], preferred_element_type=F32)
    pc = jnp.dot(c, woc_ref[...], preferred_element_type=F32)
    merged = jax.nn.sigmoid(ga_ref[...].astype(F32)) * pa + jax.nn.sigmoid(gc_ref[...].astype(F32)) * pc
    o_ref[...] = x_ref[...] + jnp.dot(merged.astype(BF16), wo_ref[...], preferred_element_type=F32)


def _mixer_out(z, a, conv_w, woa, woc, wo, x, cols, tm, seq):
    t, d = x.shape
    w = conv_w.shape[1]
    c_h, c_b, c_c, c_ga, c_gc = cols
    blk = lambda col: pl.BlockSpec((tm, w), lambda i: (i, col))
    prev = lambda col: pl.BlockSpec((8, w), lambda i: (jnp.maximum(i * (tm // 8) - 1, 0), col))
    full = lambda arr: pl.BlockSpec(arr.shape, lambda i: (0, 0))
    kern = functools.partial(_mixer_out_kernel, tm=tm, seq=seq)
    return pl.pallas_call(
        kern,
        grid=(t // tm,),
        in_specs=[blk(c_h), blk(c_b), blk(c_c), blk(c_ga), blk(c_gc), prev(c_h), prev(c_c),
                  pl.BlockSpec((tm, a.shape[1]), lambda i: (i, 0)),
                  full(conv_w), full(woa), full(woc), full(wo),
                  pl.BlockSpec((tm, d), lambda i: (i, 0))],
        out_specs=pl.BlockSpec((tm, d), lambda i: (i, 0)),
        out_shape=jax.ShapeDtypeStruct((t, d), F32),
        compiler_params=_cparams(("parallel",)),
        name="mixer_out",
    )(z, z, z, z, z, z, z, a, conv_w, woa, woc, wo, x)


def _dense_ffn_kernel(x_ref, g_ref, wg_ref, wu_ref, wd_ref, o_ref):
    x = x_ref[...]
    h = _rms(x, g_ref[...]).astype(BF16)
    gate = jnp.dot(h, wg_ref[...], preferred_element_type=F32)
    up = jnp.dot(h, wu_ref[...], preferred_element_type=F32)
    act = (gate * jax.nn.sigmoid(gate) * up).astype(BF16)
    o_ref[...] = x + jnp.dot(act, wd_ref[...], preferred_element_type=F32)


def _dense_ffn(x, g, wg, wu, wd, tm):
    t, d = x.shape
    resident = lambda arr: pl.BlockSpec(arr.shape, lambda i: (0, 0), pipeline_mode=pl.Buffered(1))
    return pl.pallas_call(
        _dense_ffn_kernel,
        grid=(t // tm,),
        in_specs=[pl.BlockSpec((tm, d), lambda i: (i, 0)),
                  pl.BlockSpec((1, d), lambda i: (0, 0)),
                  resident(wg), resident(wu), resident(wd)],
        out_specs=pl.BlockSpec((tm, d), lambda i: (i, 0)),
        out_shape=jax.ShapeDtypeStruct((t, d), F32),
        compiler_params=_cparams(("parallel",)),
        name="dense_ffn",
    )(x, g.reshape(1, d), wg, wu, wd)


def _split_bf16(v):
    hi = v.astype(BF16)
    lo = (v - hi.astype(F32)).astype(BF16)
    return hi, lo


def _router_kernel(x_ref, g_ref, rw_ref, h_ref, r_ref):
    h = _rms(x_ref[...], g_ref[...])
    h_ref[...] = h.astype(BF16)
    h_hi, h_lo = _split_bf16(h)
    w_hi, w_lo = _split_bf16(rw_ref[...])
    logits = (jnp.dot(h_hi, w_hi, preferred_element_type=F32)
              + jnp.dot(h_hi, w_lo, preferred_element_type=F32)
              + jnp.dot(h_lo, w_hi, preferred_element_type=F32))
    lane = lax.broadcasted_iota(jnp.int32, logits.shape, 1)
    logits = jnp.where(lane < N_EXPERTS, logits, -jnp.inf)
    v1 = jnp.max(logits, axis=-1, keepdims=True)
    i1 = jnp.min(jnp.where(logits == v1, lane, LANES), axis=-1, keepdims=True)
    rest = jnp.where(lane == i1, -jnp.inf, logits)
    v2 = jnp.max(rest, axis=-1, keepdims=True)
    i2 = jnp.min(jnp.where(rest == v2, lane, LANES), axis=-1, keepdims=True)
    e2 = jnp.exp(v2 - v1)
    g1 = 1.0 / (1.0 + e2)
    g2 = e2 / (1.0 + e2)
    out = jnp.where(lane == 0, i1.astype(F32),
                    jnp.where(lane == 1, i2.astype(F32),
                              jnp.where(lane == 2, g1, jnp.where(lane == 3, g2, 0.0))))
    r_ref[...] = out


def _router(x, g, rw_pad, tm):
    t, d = x.shape
    return pl.pallas_call(
        _router_kernel,
        grid=(t // tm,),
        in_specs=[pl.BlockSpec((tm, d), lambda i: (i, 0)),
                  pl.BlockSpec((1, d), lambda i: (0, 0)),
                  pl.BlockSpec(rw_pad.shape, lambda i: (0, 0))],
        out_specs=[pl.BlockSpec((tm, d), lambda i: (i, 0)),
                   pl.BlockSpec((tm, LANES), lambda i: (i, 0))],
        out_shape=[jax.ShapeDtypeStruct((t, d), BF16),
                   jax.ShapeDtypeStruct((t, LANES), F32)],
        compiler_params=_cparams(("parallel",)),
        name="moe_router",
    )(x, g.reshape(1, d), rw_pad)


def _gmm_kernel(te_ref, ta_ref, xs_ref, wg_ref, wu_ref, wd_ref, *rest, nch, tile0):
    o_ref, acc_ref = rest[-2:]
    i = tile0 + pl.program_id(0)
    c = pl.program_id(1)
    active = ta_ref[i] != 0

    @pl.when(active)
    def _():
        x = xs_ref[...]
        gate = jnp.dot(x, wg_ref[0], preferred_element_type=F32)
        up = jnp.dot(x, wu_ref[0], preferred_element_type=F32)
        act = (gate * jax.nn.sigmoid(gate) * up).astype(BF16)
        contrib = jnp.dot(act, wd_ref[0], preferred_element_type=F32)

        @pl.when(c == 0)
        def _():
            acc_ref[...] = contrib

        @pl.when(c != 0)
        def _():
            acc_ref[...] += contrib

        @pl.when(c == nch - 1)
        def _():
            o_ref[...] = acc_ref[...].astype(o_ref.dtype)

    @pl.when(jnp.logical_and(jnp.logical_not(active), c == nch - 1))
    def _():
        o_ref[...] = jnp.zeros(o_ref.shape, o_ref.dtype)


def _grouped_ffn(tile_expert, tile_active, xs_part, w_gu, w_down, ys_prev, tile0, n_tiles, tm, tf):
    p_part, d = xs_part.shape
    ff = w_down.shape[1]
    nch = ff // tf
    kern = functools.partial(_gmm_kernel, nch=nch, tile0=tile0)

    def chunk(c, ta, i):
        return jnp.where(ta[tile0 + i] != 0, c, nch - 1)

    in_specs = [
        pl.BlockSpec((tm, d), lambda i, c, te, ta: (i, 0)),
        pl.BlockSpec((1, d, tf), lambda i, c, te, ta: (te[tile0 + i], 0, chunk(c, ta, i))),
        pl.BlockSpec((1, d, tf), lambda i, c, te, ta: (te[tile0 + i], 0, nch + chunk(c, ta, i))),
        pl.BlockSpec((1, tf, d), lambda i, c, te, ta: (te[tile0 + i], chunk(c, ta, i), 0)),
    ]
    operands = [tile_expert, tile_active, xs_part, w_gu, w_gu, w_down]
    aliases = {}
    if ys_prev is not None:
        in_specs.append(pl.BlockSpec(memory_space=pl.ANY))
        operands.append(ys_prev)
        aliases = {len(operands) - 1: 0}
    grid_spec = pltpu.PrefetchScalarGridSpec(
        num_scalar_prefetch=2,
        grid=(p_part // tm, nch),
        in_specs=in_specs,
        out_specs=pl.BlockSpec((tm, d), lambda i, c, te, ta: (tile0 + i, 0)),
        scratch_shapes=[pltpu.VMEM((tm, d), F32)],
    )
    return pl.pallas_call(
        kern,
        grid_spec=grid_spec,
        out_shape=jax.ShapeDtypeStruct((n_tiles * tm, d), BF16),
        input_output_aliases=aliases,
        compiler_params=_cparams(("arbitrary", "arbitrary")),
        name="moe_grouped_ffn",
    )(*operands)


def _combine_kernel(x_ref, r_ref, y0_ref, y1_ref, g_ref, o_ref, *, final_norm):
    route = r_ref[...]
    x = (x_ref[...] + route[:, 2:3] * y0_ref[...].astype(F32)
         + route[:, 3:4] * y1_ref[...].astype(F32))
    o_ref[...] = _rms(x, g_ref[...]) if final_norm else x


def _combine(x, route, ysel, g, tm, final_norm):
    t, d = x.shape
    nt = t // tm
    kern = functools.partial(_combine_kernel, final_norm=final_norm)
    return pl.pallas_call(
        kern,
        grid=(nt,),
        in_specs=[pl.BlockSpec((tm, d), lambda i: (i, 0)),
                  pl.BlockSpec((tm, LANES), lambda i: (i, 0)),
                  pl.BlockSpec((tm, d), lambda i: (i, 0)),
                  pl.BlockSpec((tm, d), lambda i: (nt + i, 0)),
                  pl.BlockSpec((1, d), lambda i: (0, 0))],
        out_specs=pl.BlockSpec((tm, d), lambda i: (i, 0)),
        out_shape=jax.ShapeDtypeStruct((t, d), F32),
        compiler_params=_cparams(("parallel",)),
        name="moe_combine",
    )(x, route, ysel, ysel, g.reshape(1, d))


def _final_norm_kernel(x_ref, g_ref, o_ref):
    o_ref[...] = _rms(x_ref[...], g_ref[...])


def _final_norm(x, g, tm):
    t, d = x.shape
    return pl.pallas_call(
        _final_norm_kernel,
        grid=(t // tm,),
        in_specs=[pl.BlockSpec((tm, d), lambda i: (i, 0)), pl.BlockSpec((1, d), lambda i: (0, 0))],
        out_specs=pl.BlockSpec((tm, d), lambda i: (i, 0)),
        out_shape=jax.ShapeDtypeStruct((t, d), F32),
        compiler_params=_cparams(("parallel",)),
        name="final_norm",
    )(x, g.reshape(1, d))


def _t5_causal_bucket(dist):
    max_exact = NUM_BUCKETS // 2
    is_small = dist < max_exact
    d = jnp.maximum(dist, 1).astype(F32)
    large = max_exact + (jnp.log(d / max_exact) / math.log(MAX_DISTANCE / max_exact)
                         * (NUM_BUCKETS - max_exact)).astype(jnp.int32)
    large = jnp.minimum(large, NUM_BUCKETS - 1)
    return jnp.where(is_small, dist, large)


def _bias_vectors(rel_bias, seq, tq, tk):
    assert tq == tk and tk + 1 >= MAX_DISTANCE
    rb = rel_bias.astype(F32)
    bias_dist = rb[_t5_causal_bucket(jnp.arange(seq, dtype=jnp.int32))].T
    rel = (bias_dist - rb[NUM_BUCKETS - 1][:, None]) * LOG2E
    d_diag = jnp.arange(2 * tq, dtype=jnp.int32) - (tk - 1)
    d_sub = jnp.minimum(d_diag + tk, seq - 1)
    v_diag = jnp.where(d_diag >= 0, rel[:, jnp.clip(d_diag, 0, seq - 1)], NEG_BIG)
    v_sub = rel[:, d_sub]
    return jnp.stack([v_diag, v_sub], axis=1)[:, :, None, :]


def _moe_dispatch(route, tm):
    t = route.shape[0]
    expert = route[:, :2].astype(jnp.int32).reshape(-1)
    onehot = (expert[:, None] == jnp.arange(N_EXPERTS, dtype=jnp.int32)[None, :]).astype(jnp.int32)
    rank = jnp.take_along_axis(jnp.cumsum(onehot, axis=0), expert[:, None], axis=1)[:, 0] - 1
    counts = jnp.sum(onehot, axis=0)
    padded = ((counts + tm - 1) // tm) * tm
    ends = jnp.cumsum(padded)
    starts = ends - padded
    pos = starts[expert] + rank
    p = 2 * t + N_EXPERTS * tm
    row_token = jnp.zeros((p,), jnp.int32).at[pos].set(jnp.arange(2 * t, dtype=jnp.int32) // 2)
    tile_start = jnp.arange(p // tm, dtype=jnp.int32) * tm
    tile_active = (tile_start < ends[-1]).astype(jnp.int32)
    tile_expert = jnp.minimum(jnp.sum((tile_start[:, None] >= ends[None, :]).astype(jnp.int32), axis=1),
                              N_EXPERTS - 1)
    last_expert = tile_expert[jnp.maximum(ends[-1] // tm - 1, 0)]
    tile_expert = jnp.where(tile_active != 0, tile_expert, last_expert).astype(jnp.int32)
    pos_by_choice = pos.reshape(t, 2).T.reshape(-1)
    return pos_by_choice, row_token, tile_expert, tile_active


def kernel(x, rel_bias, mix_norm_g, w_in, lam_qk, subln_g, conv_w, w_o_attn, w_o_conv, w_o, ffn_norm_g,
           dense_w_gate_up, dense_w_down, router_w, expert_w_gate_up, expert_w_down, final_norm_g):
    batch, seq, d = x.shape
    t = batch * seq
    depth = w_in.shape[0]
    qk_w = N_HEADS * 2 * HEAD_DIM
    v_w = N_HEADS * V_HEAD_DIM
    tq = min(ATTN_TQ, seq)
    tk = tq
    tm = min(ROW_TILE, seq)

    xf = x.reshape(t, d)
    bias_vecs = _bias_vectors(rel_bias, seq, tq, tk)
    conv_cols = tuple(range(5))

    for l in range(depth):
        w3 = w_in[l].astype(BF16).reshape(d, -1, v_w).transpose(1, 0, 2)
        q4, k4, vt5, z = _in_proj(xf, mix_norm_g[l], w3, batch, seq, tk)

        lq = lam_qk[l].astype(F32)
        lam_init = 0.8 - 0.6 * math.exp(-0.3 * l)
        lam = (jnp.exp(jnp.sum(lq[0] * lq[1])) - jnp.exp(jnp.sum(lq[2] * lq[3])) + lam_init).reshape(1)
        gsub = subln_g[l].astype(F32) * (1.0 - lam_init)
        gfull = jnp.broadcast_to(gsub[:, None], (V_HEAD_DIM, tq))
        cast_weights = ()
        if l % 2 == 0 and l + 1 < depth:
            e = (l + 1) // 2
            cast_weights = (expert_w_gate_up[e].reshape(-1, expert_w_gate_up.shape[-1]),
                            expert_w_down[e].reshape(-1, expert_w_down.shape[-1]))
        a, cast_bf16 = _attention(q4, k4, vt5, bias_vecs, lam, gfull, tq, tk, cast_weights)
        if cast_weights:
            expert_bf16 = (cast_bf16[0].reshape(expert_w_gate_up.shape[1:]),
                           cast_bf16[1].reshape(expert_w_down.shape[1:]))

        xf = _mixer_out(z, a.reshape(t, v_w), conv_w[l].astype(F32), w_o_attn[l].astype(BF16),
                        w_o_conv[l].astype(BF16), w_o[l].astype(BF16), xf, conv_cols, tm, seq)

        last = l == depth - 1
        if l % 2 == 0:
            wgu = dense_w_gate_up[l // 2]
            ff = wgu.shape[1] // 2
            xf = _dense_ffn(xf, ffn_norm_g[l], wgu[:, :ff].astype(BF16), wgu[:, ff:].astype(BF16),
                            dense_w_down[l // 2].astype(BF16), tm)
            if last:
                xf = _final_norm(xf, final_norm_g, tm)
        else:
            mt = min(MOE_TILE, seq)
            rw_pad = jnp.zeros((d, LANES), F32).at[:, :N_EXPERTS].set(router_w[l // 2].astype(F32))
            hb, route = _router(xf, ffn_norm_g[l], rw_pad, tm)
            pos_by_choice, row_token, tile_expert, tile_active = _moe_dispatch(route, mt)
            ff = expert_w_down.shape[2]
            n_tiles = row_token.shape[0] // mt
            n_parts = MOE_PARTS if n_tiles % MOE_PARTS == 0 else 1
            part_tiles = n_tiles // n_parts
            ys = None
            for part in range(n_parts):
                rows = slice(part * part_tiles * mt, (part + 1) * part_tiles * mt)
                ys = _grouped_ffn(tile_expert, tile_active, hb[row_token[rows]], expert_bf16[0], expert_bf16[1],
                                  ys, part * part_tiles, n_tiles, mt, ff // 2)
            xf = _combine(xf, route, ys[pos_by_choice], final_norm_g, tm, final_norm=last)

    return xf.reshape(batch, seq, d)
```

```python
import functools
import math

import jax
import jax.numpy as jnp
from jax import lax
from jax.experimental import pallas as pl
from jax.experimental.pallas import tpu as pltpu

F32 = jnp.float32
BF16 = jnp.bfloat16

N_HEADS = 8
HEAD_DIM = 64
V_HEAD_DIM = 2 * HEAD_DIM
V_ROWS = V_HEAD_DIM + 16
NUM_BUCKETS = 32
MAX_DISTANCE = 128
N_EXPERTS = 8
RMS_EPS = 1e-6

LANES = 128
VMEM_LIMIT_BYTES = 56 * 1024 * 1024

LOG2E = math.log2(math.e)
NEG_BIG = -1e30

ATTN_TQ = 512
ATTN_TK = 512
ATTN_QCHAIN = 256
ROW_TILE = 512
MOE_TILE = 512
MOE_PARTS = 4


def _cparams(sem):
    return pltpu.CompilerParams(dimension_semantics=sem, vmem_limit_bytes=VMEM_LIMIT_BYTES)


def _rms(x, g):
    ms = jnp.mean(x * x, axis=-1, keepdims=True)
    return x * lax.rsqrt(ms + RMS_EPS) * g


def _in_proj_kernel(x_ref, g_ref, w_ref, q_ref, k_ref, vt_ref, z_ref):
    h = _rms(x_ref[...], g_ref[...]).astype(BF16)

    def proj(blk):
        return jnp.dot(h, w_ref[blk], preferred_element_type=F32)

    for dst, blk in ((q_ref, 0), (k_ref, 1)):
        y = proj(blk).astype(dst.dtype)
        for hh in range(N_HEADS):
            dst[0, hh] = y[:, hh * LANES:(hh + 1) * LANES]

    yt = proj(2).astype(vt_ref.dtype).T
    pad_rows = V_ROWS - V_HEAD_DIM
    row = lax.broadcasted_iota(jnp.int32, (pad_rows, yt.shape[1]), 0)
    ones_rows = jnp.where(row == 0, 1.0, 0.0).astype(vt_ref.dtype)
    for hh in range(N_HEADS):
        vt_ref[0, hh, 0, :V_HEAD_DIM, :] = yt[hh * V_HEAD_DIM:(hh + 1) * V_HEAD_DIM, :]
        vt_ref[0, hh, 0, V_HEAD_DIM:, :] = ones_rows

    width = w_ref.shape[2]
    for c in range(z_ref.shape[1] // width):
        z_ref[:, c * width:(c + 1) * width] = proj(3 + c).astype(z_ref.dtype)


def _in_proj(x, g, w3, batch, seq, tm):
    t, d = x.shape
    nblk, _, width = w3.shape
    assert width == N_HEADS * 2 * HEAD_DIM == N_HEADS * V_HEAD_DIM
    nt = seq // tm
    n_rest = (nblk - 3) * width
    head_spec = pl.BlockSpec((1, N_HEADS, tm, LANES), lambda i: (i // nt, 0, i % nt, 0))
    return pl.pallas_call(
        _in_proj_kernel,
        grid=(t // tm,),
        in_specs=[
            pl.BlockSpec((tm, d), lambda i: (i, 0)),
            pl.BlockSpec((1, d), lambda i: (0, 0)),
            pl.BlockSpec(w3.shape, lambda i: (0, 0, 0), pipeline_mode=pl.Buffered(1)),
        ],
        out_specs=[
            head_spec, head_spec,
            pl.BlockSpec((1, N_HEADS, 1, V_ROWS, tm), lambda i: (i // nt, 0, i % nt, 0, 0)),
            pl.BlockSpec((tm, n_rest), lambda i: (i, 0)),
        ],
        out_shape=[jax.ShapeDtypeStruct((batch, N_HEADS, seq, LANES), BF16),
                   jax.ShapeDtypeStruct((batch, N_HEADS, seq, LANES), BF16),
                   jax.ShapeDtypeStruct((batch, N_HEADS, nt, V_ROWS, tm), BF16),
                   jax.ShapeDtypeStruct((t, n_rest), BF16)],
        compiler_params=_cparams(("parallel",)),
        name="in_proj",
    )(x, g.reshape(1, d), w3)


def _attn_kernel(lam_ref, q_ref, qn_ref, k_ref, vt_ref, ub_ref, g_ref, *refs, tq, tk, n_cast):
    cast_in = refs[:n_cast]
    o_ref = refs[n_cast]
    cast_out = refs[n_cast + 1:2 * n_cast + 1]
    bias_ref, s0_ref, s1_ref, sd_ref, mc0_ref, mc1_ref, mcd_ref, acc_ref, m_ref = refs[2 * n_cast + 1:]
    bufs = ((s0_ref, mc0_ref), (s1_ref, mc1_ref))
    diag_buf = (sd_ref, mcd_ref)
    i = pl.program_id(2)
    nq = pl.num_programs(2)

    for w_ref, wb_ref in zip(cast_in, cast_out):
        wb_ref[...] = w_ref[...].astype(wb_ref.dtype)

    def masked_queries(ref):
        q = ref[0, 0].astype(F32) * (LOG2E * HEAD_DIM ** -0.5)
        lane = lax.broadcasted_iota(jnp.int32, q.shape, 1)
        return (jnp.where(lane < HEAD_DIM, q, 0.0).astype(BF16),
                jnp.where(lane >= HEAD_DIM, q, 0.0).astype(BF16))

    def produce(j, bias, buf, queries):
        s_ref, mc_ref = buf
        row0 = pl.multiple_of(j * tk, tk)
        k = k_ref[0, 0, pl.ds(row0, tk), :]
        for mp in range(2):
            s = lax.dot_general(k, queries[mp], (((1,), (1,)), ((), ())), preferred_element_type=F32)
            if bias is not None:
                s = s + bias
            s_ref[mp] = s
            mc_ref[mp] = jnp.max(s, axis=0, keepdims=True)

    def consume(j, buf):
        s_ref, mc_ref = buf
        vt = vt_ref[0, 0, j]
        for mp in range(2):
            for h in range(tq // ATTN_QCHAIN):
                cols = slice(h * ATTN_QCHAIN, (h + 1) * ATTN_QCHAIN)
                m_old = m_ref[mp, :, cols]
                m_new = jnp.maximum(m_old, mc_ref[mp, :, cols])
                alpha = jnp.exp2(m_old - m_new)
                p = jnp.exp2(s_ref[mp, :, cols] - m_new)
                pv = jnp.dot(vt, p.astype(BF16), preferred_element_type=F32)
                acc_ref[mp, :, cols] = alpha * acc_ref[mp, :, cols] + pv
                m_ref[mp, :, cols] = m_new

    qs = masked_queries(q_ref)

    @pl.when(i == 0)
    def _():
        for t in range(2):
            u = jnp.broadcast_to(ub_ref[0, t], (tk, 2 * tq))
            bias_ref[t] = pltpu.roll(u, 1, 1, stride=1, stride_axis=0)[:, tq:]
        produce(i, bias_ref[0], diag_buf, qs)

    m_ref[...] = jnp.full(m_ref.shape, NEG_BIG, F32)
    acc_ref[...] = jnp.zeros(acc_ref.shape, F32)

    def steps(k0, count, from_start=False):
        for t in range(count):
            if from_start and t == 0:
                produce(i - 1, bias_ref[1], bufs[1], qs)
                consume(i, diag_buf)
                continue
            par = (t + 1) % 2
            f = k0 + t - 1
            prev = (i - 1 if t == 1 else f - 1) if from_start else f - 1
            produce(f, None, bufs[par], qs)
            consume(prev, bufs[1 - par])

    n_quads = i // 4
    rem = i - 4 * n_quads

    @pl.when(n_quads >= 1)
    def _():
        steps(0, 4, from_start=True)

    def quad(g, carry):
        steps(4 * g, 4)
        return carry

    lax.fori_loop(1, n_quads, quad, 0)

    @pl.when(jnp.logical_and(n_quads == 0, rem >= 2))
    def _():
        steps(0, 2, from_start=True)

    @pl.when(jnp.logical_and(n_quads == 0, rem == 1))
    def _():
        steps(0, 1, from_start=True)

    @pl.when(jnp.logical_and(n_quads >= 1, rem >= 2))
    def _():
        steps(4 * n_quads, 2)

    @pl.when(jnp.logical_and(i >= 3, rem % 2 == 1))
    def _():
        steps(i - 1, 1)

    last = jnp.where(i >= 2, i - 2, 0)
    nxt = jnp.minimum(i + 1, nq - 1)

    def last_step(buf):
        qs_next = masked_queries(qn_ref)
        consume(jnp.where(i == 0, 0, last), buf)
        produce(nxt, bias_ref[0], diag_buf, qs_next)

    @pl.when(i == 0)
    def _():
        last_step(diag_buf)

    for parity in range(2):
        @pl.when(jnp.logical_and(i >= 1, i % 2 == parity))
        def _():
            last_step(bufs[parity])

    den = slice(V_HEAD_DIM, V_HEAD_DIM + 1)
    o = (acc_ref[0, :V_HEAD_DIM] / acc_ref[0, den]
         - lam_ref[0] * (acc_ref[1, :V_HEAD_DIM] / acc_ref[1, den]))
    ms = jnp.mean(o * o, axis=0, keepdims=True)
    o = o * lax.rsqrt(ms + RMS_EPS) * g_ref[...]
    o_ref[0] = o.T.astype(o_ref.dtype)


def _attention(q4, k4, vt5, bias_vecs, lam, gfull, tq, tk, cast_weights=()):
    batch, _, seq, _ = q4.shape
    nq = seq // tq
    nk = seq // tk
    n_steps = batch * N_HEADS * nq
    bf16_rows = 16
    cast_specs = []
    for w in cast_weights:
        rows, cols = w.shape
        rb = bf16_rows * pl.cdiv(rows, bf16_rows * n_steps)
        assert rows % rb == 0
        last = rows // rb - 1
        cast_specs.append(pl.BlockSpec(
            (rb, cols), lambda b, h, i, last=last: (jnp.minimum((b * N_HEADS + h) * nq + i, last), 0)))
    kern = functools.partial(_attn_kernel, tq=tq, tk=tk, n_cast=len(cast_weights))
    outs = pl.pallas_call(
        kern,
        grid=(batch, N_HEADS, nq),
        in_specs=[
            pl.BlockSpec(memory_space=pltpu.SMEM),
            pl.BlockSpec((1, 1, tq, LANES), lambda b, h, i: (b, h, i, 0)),
            pl.BlockSpec((1, 1, tq, LANES), lambda b, h, i: (b, h, jnp.minimum(i + 1, nq - 1), 0)),
            pl.BlockSpec((1, 1, seq, LANES), lambda b, h, i: (b, h, 0, 0)),
            pl.BlockSpec((1, 1, nk, V_ROWS, tk), lambda b, h, i: (b, h, 0, 0, 0)),
            pl.BlockSpec((1, 2, 1, 2 * tq), lambda b, h, i: (h, 0, 0, 0)),
            pl.BlockSpec((V_HEAD_DIM, tq), lambda b, h, i: (0, 0)),
        ] + cast_specs,
        out_specs=[pl.BlockSpec((1, tq, V_HEAD_DIM), lambda b, h, i: (b, i, h))] + cast_specs,
        out_shape=[jax.ShapeDtypeStruct((batch, seq, N_HEADS * V_HEAD_DIM), BF16)]
        + [jax.ShapeDtypeStruct(w.shape, BF16) for w in cast_weights],
        scratch_shapes=[
            pltpu.VMEM((2, tk, tq), F32),
            pltpu.VMEM((2, tk, tq), F32),
            pltpu.VMEM((2, tk, tq), F32),
            pltpu.VMEM((2, tk, tq), F32),
            pltpu.VMEM((2, 1, tq), F32),
            pltpu.VMEM((2, 1, tq), F32),
            pltpu.VMEM((2, 1, tq), F32),
            pltpu.VMEM((2, V_ROWS, tq), F32),
            pltpu.VMEM((2, 1, tq), F32),
        ],
        compiler_params=_cparams(("parallel", "parallel", "arbitrary")),
        name="diff_attention",
    )(lam, q4, q4, k4, vt5, bias_vecs, gfull, *cast_weights)
    return outs[0], tuple(outs[1:])


def _mixer_out_kernel(ch_ref, cb_ref, cc_ref, ga_ref, gc_ref, chp_ref, ccp_ref, a_ref, cw_ref,
                      woa_ref, woc_ref, wo_ref, x_ref, o_ref, *, tm, seq):
    i = pl.program_id(0)
    u = cc_ref[...].astype(F32) * ch_ref[...].astype(F32)
    up = ccp_ref[...].astype(F32) * chp_ref[...].astype(F32)
    up = jnp.where((i * tm) % seq == 0, 0.0, up)
    row = lax.broadcasted_iota(jnp.int32, u.shape, 0)
    u1 = jnp.where(row == 0, up[7:8], pltpu.roll(u, 1, 0))
    u2 = jnp.where(row == 0, up[6:7], jnp.where(row == 1, up[7:8], pltpu.roll(u, 2, 0)))
    cw = cw_ref[...]
    y = cw[0:1] * u2 + cw[1:2] * u1 + cw[2:3] * u
    c = (cb_ref[...].astype(F32) * y).astype(BF16)
    pa = jnp.dot(a_ref[...], woa_ref[...], preferred_element_type=F32)
    pc = jnp.dot(c, woc_ref[...], preferred_element_type=F32)
    merged = jax.nn.sigmoid(ga_ref[...].astype(F32)) * pa + jax.nn.sigmoid(gc_ref[...].astype(F32)) * pc
    o_ref[...] = x_ref[...] + jnp.dot(merged.astype(BF16), wo_ref[...], preferred_element_type=F32)


def _mixer_out(z, a, conv_w, woa, woc, wo, x, cols, tm, seq):
    t, d = x.shape
    w = conv_w.shape[1]
    c_h, c_b, c_c, c_ga, c_gc = cols
    blk = lambda col: pl.BlockSpec((tm, w), lambda i: (i, col))
    prev = lambda col: pl.BlockSpec((8, w), lambda i: (jnp.maximum(i * (tm // 8) - 1, 0), col))
    full = lambda arr: pl.BlockSpec(arr.shape, lambda i: (0, 0))
    kern = functools.partial(_mixer_out_kernel, tm=tm, seq=seq)
    return pl.pallas_call(
        kern,
        grid=(t // tm,),
        in_specs=[blk(c_h), blk(c_b), blk(c_c), blk(c_ga), blk(c_gc), prev(c_h), prev(c_c),
                  pl.BlockSpec((tm, a.shape[1]), lambda i: (i, 0)),
                  full(conv_w), full(woa), full(woc), full(wo),
                  pl.BlockSpec((tm, d), lambda i: (i, 0))],
        out_specs=pl.BlockSpec((tm, d), lambda i: (i, 0)),
        out_shape=jax.ShapeDtypeStruct((t, d), F32),
        compiler_params=_cparams(("parallel",)),
        name="mixer_out",
    )(z, z, z, z, z, z, z, a, conv_w, woa, woc, wo, x)


def _dense_ffn_kernel(x_ref, g_ref, wg_ref, wu_ref, wd_ref, o_ref):
    x = x_ref[...]
    h = _rms(x, g_ref[...]).astype(BF16)
    gate = jnp.dot(h, wg_ref[...], preferred_element_type=F32)
    up = jnp.dot(h, wu_ref[...], preferred_element_type=F32)
    act = (gate * jax.nn.sigmoid(gate) * up).astype(BF16)
    o_ref[...] = x + jnp.dot(act, wd_ref[...], preferred_element_type=F32)


def _dense_ffn(x, g, wg, wu, wd, tm):
    t, d = x.shape
    resident = lambda arr: pl.BlockSpec(arr.shape, lambda i: (0, 0), pipeline_mode=pl.Buffered(1))
    return pl.pallas_call(
        _dense_ffn_kernel,
        grid=(t // tm,),
        in_specs=[pl.BlockSpec((tm, d), lambda i: (i, 0)),
                  pl.BlockSpec((1, d), lambda i: (0, 0)),
                  resident(wg), resident(wu), resident(wd)],
        out_specs=pl.BlockSpec((tm, d), lambda i: (i, 0)),
        out_shape=jax.ShapeDtypeStruct((t, d), F32),
        compiler_params=_cparams(("parallel",)),
        name="dense_ffn",
    )(x, g.reshape(1, d), wg, wu, wd)


def _split_bf16(v):
    hi = v.astype(BF16)
    lo = (v - hi.astype(F32)).astype(BF16)
    return hi, lo


def _router_kernel(x_ref, g_ref, rw_ref, h_ref, r_ref):
    h = _rms(x_ref[...], g_ref[...])
    h_ref[...] = h.astype(BF16)
    h_hi, h_lo = _split_bf16(h)
    w_hi, w_lo = _split_bf16(rw_ref[...])
    logits = (jnp.dot(h_hi, w_hi, preferred_element_type=F32)
              + jnp.dot(h_hi, w_lo, preferred_element_type=F32)
              + jnp.dot(h_lo, w_hi, preferred_element_type=F32))
    lane = lax.broadcasted_iota(jnp.int32, logits.shape, 1)
    logits = jnp.where(lane < N_EXPERTS, logits, -jnp.inf)
    v1 = jnp.max(logits, axis=-1, keepdims=True)
    i1 = jnp.min(jnp.where(logits == v1, lane, LANES), axis=-1, keepdims=True)
    rest = jnp.where(lane == i1, -jnp.inf, logits)
    v2 = jnp.max(rest, axis=-1, keepdims=True)
    i2 = jnp.min(jnp.where(rest == v2, lane, LANES), axis=-1, keepdims=True)
    e2 = jnp.exp(v2 - v1)
    g1 = 1.0 / (1.0 + e2)
    g2 = e2 / (1.0 + e2)
    out = jnp.where(lane == 0, i1.astype(F32),
                    jnp.where(lane == 1, i2.astype(F32),
                              jnp.where(lane == 2, g1, jnp.where(lane == 3, g2, 0.0))))
    r_ref[...] = out


def _router(x, g, rw_pad, tm):
    t, d = x.shape
    return pl.pallas_call(
        _router_kernel,
        grid=(t // tm,),
        in_specs=[pl.BlockSpec((tm, d), lambda i: (i, 0)),
                  pl.BlockSpec((1, d), lambda i: (0, 0)),
                  pl.BlockSpec(rw_pad.shape, lambda i: (0, 0))],
        out_specs=[pl.BlockSpec((tm, d), lambda i: (i, 0)),
                   pl.BlockSpec((tm, LANES), lambda i: (i, 0))],
        out_shape=[jax.ShapeDtypeStruct((t, d), BF16),
                   jax.ShapeDtypeStruct((t, LANES), F32)],
        compiler_params=_cparams(("parallel",)),
        name="moe_router",
    )(x, g.reshape(1, d), rw_pad)


def _gmm_kernel(te_ref, ta_ref, xs_ref, wg_ref, wu_ref, wd_ref, *rest, nch, tile0):
    o_ref, acc_ref = rest[-2:]
    i = tile0 + pl.program_id(0)
    c = pl.program_id(1)
    active = ta_ref[i] != 0

    @pl.when(active)
    def _():
        x = xs_ref[...]
        gate = jnp.dot(x, wg_ref[0], preferred_element_type=F32)
        up = jnp.dot(x, wu_ref[0], preferred_element_type=F32)
        act = (gate * jax.nn.sigmoid(gate) * up).astype(BF16)
        contrib = jnp.dot(act, wd_ref[0], preferred_element_type=F32)

        @pl.when(c == 0)
        def _():
            acc_ref[...] = contrib

        @pl.when(c != 0)
        def _():
            acc_ref[...] += contrib

        @pl.when(c == nch - 1)
        def _():
            o_ref[...] = acc_ref[...].astype(o_ref.dtype)

    @pl.when(jnp.logical_and(jnp.logical_not(active), c == nch - 1))
    def _():
        o_ref[...] = jnp.zeros(o_ref.shape, o_ref.dtype)


def _grouped_ffn(tile_expert, tile_active, xs_part, w_gu, w_down, ys_prev, tile0, n_tiles, tm, tf):
    p_part, d = xs_part.shape
    ff = w_down.shape[1]
    nch = ff // tf
    kern = functools.partial(_gmm_kernel, nch=nch, tile0=tile0)

    def chunk(c, ta, i):
        return jnp.where(ta[tile0 + i] != 0, c, nch - 1)

    in_specs = [
        pl.BlockSpec((tm, d), lambda i, c, te, ta: (i, 0)),
        pl.BlockSpec((1, d, tf), lambda i, c, te, ta: (te[tile0 + i], 0, chunk(c, ta, i))),
        pl.BlockSpec((1, d, tf), lambda i, c, te, ta: (te[tile0 + i], 0, nch + chunk(c, ta, i))),
        pl.BlockSpec((1, tf, d), lambda i, c, te, ta: (te[tile0 + i], chunk(c, ta, i), 0)),
    ]
    operands = [tile_expert, tile_active, xs_part, w_gu, w_gu, w_down]
    aliases = {}
    if ys_prev is not None:
        in_specs.append(pl.BlockSpec(memory_space=pl.ANY))
        operands.append(ys_prev)
        aliases = {len(operands) - 1: 0}
    grid_spec = pltpu.PrefetchScalarGridSpec(
        num_scalar_prefetch=2,
        grid=(p_part // tm, nch),
        in_specs=in_specs,
        out_specs=pl.BlockSpec((tm, d), lambda i, c, te, ta: (tile0 + i, 0)),
        scratch_shapes=[pltpu.VMEM((tm, d), F32)],
    )
    return pl.pallas_call(
        kern,
        grid_spec=grid_spec,
        out_shape=jax.ShapeDtypeStruct((n_tiles * tm, d), BF16),
        input_output_aliases=aliases,
        compiler_params=_cparams(("arbitrary", "arbitrary")),
        name="moe_grouped_ffn",
    )(*operands)


def _combine_kernel(x_ref, r_ref, y0_ref, y1_ref, g_ref, o_ref, *, final_norm):
    route = r_ref[...]
    x = (x_ref[...] + route[:, 2:3] * y0_ref[...].astype(F32)
         + route[:, 3:4] * y1_ref[...].astype(F32))
    o_ref[...] = _rms(x, g_ref[...]) if final_norm else x


def _combine(x, route, ysel, g, tm, final_norm):
    t, d = x.shape
    nt = t // tm
    kern = functools.partial(_combine_kernel, final_norm=final_norm)
    return pl.pallas_call(
        kern,
        grid=(nt,),
        in_specs=[pl.BlockSpec((tm, d), lambda i: (i, 0)),
                  pl.BlockSpec((tm, LANES), lambda i: (i, 0)),
                  pl.BlockSpec((tm, d), lambda i: (i, 0)),
                  pl.BlockSpec((tm, d), lambda i: (nt + i, 0)),
                  pl.BlockSpec((1, d), lambda i: (0, 0))],
        out_specs=pl.BlockSpec((tm, d), lambda i: (i, 0)),
        out_shape=jax.ShapeDtypeStruct((t, d), F32),
        compiler_params=_cparams(("parallel",)),
        name="moe_combine",
    )(x, route, ysel, ysel, g.reshape(1, d))


def _final_norm_kernel(x_ref, g_ref, o_ref):
    o_ref[...] = _rms(x_ref[...], g_ref[...])


def _final_norm(x, g, tm):
    t, d = x.shape
    return pl.pallas_call(
        _final_norm_kernel,
        grid=(t // tm,),
        in_specs=[pl.BlockSpec((tm, d), lambda i: (i, 0)), pl.BlockSpec((1, d), lambda i: (0, 0))],
        out_specs=pl.BlockSpec((tm, d), lambda i: (i, 0)),
        out_shape=jax.ShapeDtypeStruct((t, d), F32),
        compiler_params=_cparams(("parallel",)),
        name="final_norm",
    )(x, g.reshape(1, d))


def _t5_causal_bucket(dist):
    max_exact = NUM_BUCKETS // 2
    is_small = dist < max_exact
    d = jnp.maximum(dist, 1).astype(F32)
    large = max_exact + (jnp.log(d / max_exact) / math.log(MAX_DISTANCE / max_exact)
                         * (NUM_BUCKETS - max_exact)).astype(jnp.int32)
    large = jnp.minimum(large, NUM_BUCKETS - 1)
    return jnp.where(is_small, dist, large)


def _bias_vectors(rel_bias, seq, tq, tk):
    assert tq == tk and tk + 1 >= MAX_DISTANCE
    rb = rel_bias.astype(F32)
    bias_dist = rb[_t5_causal_bucket(jnp.arange(seq, dtype=jnp.int32))].T
    rel = (bias_dist - rb[NUM_BUCKETS - 1][:, None]) * LOG2E
    d_diag = jnp.arange(2 * tq, dtype=jnp.int32) - (tk - 1)
    d_sub = jnp.minimum(d_diag + tk, seq - 1)
    v_diag = jnp.where(d_diag >= 0, rel[:, jnp.clip(d_diag, 0, seq - 1)], NEG_BIG)
    v_sub = rel[:, d_sub]
    return jnp.stack([v_diag, v_sub], axis=1)[:, :, None, :]


def _moe_dispatch(route, tm):
    t = route.shape[0]
    expert = route[:, :2].astype(jnp.int32).reshape(-1)
    onehot = (expert[:, None] == jnp.arange(N_EXPERTS, dtype=jnp.int32)[None, :]).astype(jnp.int32)
    rank = jnp.take_along_axis(jnp.cumsum(onehot, axis=0), expert[:, None], axis=1)[:, 0] - 1
    counts = jnp.sum(onehot, axis=0)
    padded = ((counts + tm - 1) // tm) * tm
    ends = jnp.cumsum(padded)
    starts = ends - padded
    pos = starts[expert] + rank
    p = 2 * t + N_EXPERTS * tm
    row_token = jnp.zeros((p,), jnp.int32).at[pos].set(jnp.arange(2 * t, dtype=jnp.int32) // 2)
    tile_start = jnp.arange(p // tm, dtype=jnp.int32) * tm
    tile_active = (tile_start < ends[-1]).astype(jnp.int32)
    tile_expert = jnp.minimum(jnp.sum((tile_start[:, None] >= ends[None, :]).astype(jnp.int32), axis=1),
                              N_EXPERTS - 1)
    last_expert = tile_expert[jnp.maximum(ends[-1] // tm - 1, 0)]
    tile_expert = jnp.where(tile_active != 0, tile_expert, last_expert).astype(jnp.int32)
    pos_by_choice = pos.reshape(t, 2).T.reshape(-1)
    return pos_by_choice, row_token, tile_expert, tile_active


def kernel(x, rel_bias, mix_norm_g, w_in, lam_qk, subln_g, conv_w, w_o_attn, w_o_conv, w_o, ffn_norm_g,
           dense_w_gate_up, dense_w_down, router_w, expert_w_gate_up, expert_w_down, final_norm_g):
    batch, seq, d = x.shape
    t = batch * seq
    depth = w_in.shape[0]
    qk_w = N_HEADS * 2 * HEAD_DIM
    v_w = N_HEADS * V_HEAD_DIM
    tq = min(ATTN_TQ, seq)
    tk = tq
    tm = min(ROW_TILE, seq)

    xf = x.reshape(t, d)
    bias_vecs = _bias_vectors(rel_bias, seq, tq, tk)
    conv_cols = tuple(range(5))

    for l in range(depth):
        w3 = w_in[l].astype(BF16).reshape(d, -1, v_w).transpose(1, 0, 2)
        q4, k4, vt5, z = _in_proj(xf, mix_norm_g[l], w3, batch, seq, tk)

        lq = lam_qk[l].astype(F32)
        lam_init = 0.8 - 0.6 * math.exp(-0.3 * l)
        lam = (jnp.exp(jnp.sum(lq[0] * lq[1])) - jnp.exp(jnp.sum(lq[2] * lq[3])) + lam_init).reshape(1)
        gsub = subln_g[l].astype(F32) * (1.0 - lam_init)
        gfull = jnp.broadcast_to(gsub[:, None], (V_HEAD_DIM, tq))
        cast_weights = ()
        if l % 2 == 0 and l + 1 < depth:
            e = (l + 1) // 2
            cast_weights = (expert_w_gate_up[e].reshape(-1, expert_w_gate_up.shape[-1]),
                            expert_w_down[e].reshape(-1, expert_w_down.shape[-1]))
        a, cast_bf16 = _attention(q4, k4, vt5, bias_vecs, lam, gfull, tq, tk, cast_weights)
        if cast_weights:
            expert_bf16 = (cast_bf16[0].reshape(expert_w_gate_up.shape[1:]),
                           cast_bf16[1].reshape(expert_w_down.shape[1:]))

        xf = _mixer_out(z, a.reshape(t, v_w), conv_w[l].astype(F32), w_o_attn[l].astype(BF16),
                        w_o_conv[l].astype(BF16), w_o[l].astype(BF16), xf, conv_cols, tm, seq)

        last = l == depth - 1
        if l % 2 == 0:
            wgu = dense_w_gate_up[l // 2]
            ff = wgu.shape[1] // 2
            xf = _dense_ffn(xf, ffn_norm_g[l], wgu[:, :ff].astype(BF16), wgu[:, ff:].astype(BF16),
                            dense_w_down[l // 2].astype(BF16), tm)
            if last:
                xf = _final_norm(xf, final_norm_g, tm)
        else:
            mt = min(MOE_TILE, seq)
            rw_pad = jnp.zeros((d, LANES), F32).at[:, :N_EXPERTS].set(router_w[l // 2].astype(F32))
            hb, route = _router(xf, ffn_norm_g[l], rw_pad, tm)
            pos_by_choice, row_token, tile_expert, tile_active = _moe_dispatch(route, mt)
            ff = expert_w_down.shape[2]
            n_tiles = row_token.shape[0] // mt
            n_parts = MOE_PARTS if n_tiles % MOE_PARTS == 0 else 1
            part_tiles = n_tiles // n_parts
            ys = None
            for part in range(n_parts):
                rows = slice(part * part_tiles * mt, (part + 1) * part_tiles * mt)
                ys = _grouped_ffn(tile_expert, tile_active, hb[row_token[rows]], expert_bf16[0], expert_bf16[1],
                                  ys, part * part_tiles, n_tiles, mt, ff // 2)
            xf = _combine(xf, route, ys[pos_by_choice], final_norm_g, tm, final_norm=last)

    return xf.reshape(batch, seq, d)
```

```python
import functools
import math

import jax
import jax.numpy as jnp
from jax import lax
from jax.experimental import pallas as pl
from jax.experimental.pallas import tpu as pltpu

F32 = jnp.float32
BF16 = jnp.bfloat16

N_HEADS = 8
HEAD_DIM = 64
V_HEAD_DIM = 2 * HEAD_DIM
V_ROWS = V_HEAD_DIM + 16
NUM_BUCKETS = 32
MAX_DISTANCE = 128
N_EXPERTS = 8
RMS_EPS = 1e-6

LANES = 128
VMEM_LIMIT_BYTES = 56 * 1024 * 1024

LOG2E = math.log2(math.e)
NEG_BIG = -1e30

ATTN_TQ = 512
ATTN_TK = 512
ATTN_QCHAIN = 256
ROW_TILE = 512
MOE_TILE = 512
MOE_PART_EIGHTHS = (1, 2, 5)


def _cparams(sem):
    return pltpu.CompilerParams(dimension_semantics=sem, vmem_limit_bytes=VMEM_LIMIT_BYTES)


def _rms(x, g):
    ms = jnp.mean(x * x, axis=-1, keepdims=True)
    return x * lax.rsqrt(ms + RMS_EPS) * g


def _in_proj_kernel(x_ref, g_ref, w_ref, q_ref, k_ref, vt_ref, z_ref):
    h = _rms(x_ref[...], g_ref[...]).astype(BF16)

    def proj(blk):
        return jnp.dot(h, w_ref[blk], preferred_element_type=F32)

    for dst, blk in ((q_ref, 0), (k_ref, 1)):
        y = proj(blk).astype(dst.dtype)
        for hh in range(N_HEADS):
            dst[0, hh] = y[:, hh * LANES:(hh + 1) * LANES]

    yt = proj(2).astype(vt_ref.dtype).T
    pad_rows = V_ROWS - V_HEAD_DIM
    row = lax.broadcasted_iota(jnp.int32, (pad_rows, yt.shape[1]), 0)
    ones_rows = jnp.where(row == 0, 1.0, 0.0).astype(vt_ref.dtype)
    for hh in range(N_HEADS):
        vt_ref[0, hh, 0, :V_HEAD_DIM, :] = yt[hh * V_HEAD_DIM:(hh + 1) * V_HEAD_DIM, :]
        vt_ref[0, hh, 0, V_HEAD_DIM:, :] = ones_rows

    width = w_ref.shape[2]
    for c in range(z_ref.shape[1] // width):
        z_ref[:, c * width:(c + 1) * width] = proj(3 + c).astype(z_ref.dtype)


def _in_proj(x, g, w3, batch, seq, tm):
    t, d = x.shape
    nblk, _, width = w3.shape
    assert width == N_HEADS * 2 * HEAD_DIM == N_HEADS * V_HEAD_DIM
    nt = seq // tm
    n_rest = (nblk - 3) * width
    head_spec = pl.BlockSpec((1, N_HEADS, tm, LANES), lambda i: (i // nt, 0, i % nt, 0))
    return pl.pallas_call(
        _in_proj_kernel,
        grid=(t // tm,),
        in_specs=[
            pl.BlockSpec((tm, d), lambda i: (i, 0)),
            pl.BlockSpec((1, d), lambda i: (0, 0)),
            pl.BlockSpec(w3.shape, lambda i: (0, 0, 0), pipeline_mode=pl.Buffered(1)),
        ],
        out_specs=[
            head_spec, head_spec,
            pl.BlockSpec((1, N_HEADS, 1, V_ROWS, tm), lambda i: (i // nt, 0, i % nt, 0, 0)),
            pl.BlockSpec((tm, n_rest), lambda i: (i, 0)),
        ],
        out_shape=[jax.ShapeDtypeStruct((batch, N_HEADS, seq, LANES), BF16),
                   jax.ShapeDtypeStruct((batch, N_HEADS, seq, LANES), BF16),
                   jax.ShapeDtypeStruct((batch, N_HEADS, nt, V_ROWS, tm), BF16),
                   jax.ShapeDtypeStruct((t, n_rest), BF16)],
        compiler_params=_cparams(("parallel",)),
        name="in_proj",
    )(x, g.reshape(1, d), w3)


def _attn_kernel(lam_ref, q_ref, qn_ref, k_ref, vt_ref, ub_ref, g_ref, *refs, tq, tk, n_cast):
    cast_in = refs[:n_cast]
    o_ref = refs[n_cast]
    cast_out = refs[n_cast + 1:2 * n_cast + 1]
    bias_ref, s0_ref, s1_ref, sd_ref, mc0_ref, mc1_ref, mcd_ref, acc_ref, m_ref = refs[2 * n_cast + 1:]
    bufs = ((s0_ref, mc0_ref), (s1_ref, mc1_ref))
    diag_buf = (sd_ref, mcd_ref)
    i = pl.program_id(2)
    nq = pl.num_programs(2)

    for w_ref, wb_ref in zip(cast_in, cast_out):
        wb_ref[...] = w_ref[...].astype(wb_ref.dtype)

    def masked_queries(ref):
        q = ref[0, 0].astype(F32) * (LOG2E * HEAD_DIM ** -0.5)
        lane = lax.broadcasted_iota(jnp.int32, q.shape, 1)
        return (jnp.where(lane < HEAD_DIM, q, 0.0).astype(BF16),
                jnp.where(lane >= HEAD_DIM, q, 0.0).astype(BF16))

    def produce(j, bias, buf, queries):
        s_ref, mc_ref = buf
        row0 = pl.multiple_of(j * tk, tk)
        k = k_ref[0, 0, pl.ds(row0, tk), :]
        for mp in range(2):
            s = lax.dot_general(k, queries[mp], (((1,), (1,)), ((), ())), preferred_element_type=F32)
            if bias is not None:
                s = s + bias
            s_ref[mp] = s
            mc_ref[mp] = jnp.max(s, axis=0, keepdims=True)

    def consume(j, buf, first=False):
        s_ref, mc_ref = buf
        vt = vt_ref[0, 0, j]
        for mp in range(2):
            for h in range(tq // ATTN_QCHAIN):
                cols = slice(h * ATTN_QCHAIN, (h + 1) * ATTN_QCHAIN)
                if first:
                    m_new = mc_ref[mp, :, cols]
                else:
                    m_old = m_ref[mp, :, cols]
                    m_new = jnp.maximum(m_old, mc_ref[mp, :, cols])
                p = jnp.exp2(s_ref[mp, :, cols] - m_new)
                pv = jnp.dot(vt, p.astype(BF16), preferred_element_type=F32)
                if first:
                    acc_ref[mp, :, cols] = pv
                else:
                    acc_ref[mp, :, cols] = jnp.exp2(m_old - m_new) * acc_ref[mp, :, cols] + pv
                m_ref[mp, :, cols] = m_new

    qs = masked_queries(q_ref)

    @pl.when(i == 0)
    def _():
        for t in range(2):
            u = jnp.broadcast_to(ub_ref[0, t], (tk, 2 * tq))
            bias_ref[t] = pltpu.roll(u, 1, 1, stride=1, stride_axis=0)[:, tq:]
        produce(i, bias_ref[0], diag_buf, qs)

    def steps(k0, count, from_start=False):
        for t in range(count):
            if from_start and t == 0:
                produce(i - 1, bias_ref[1], bufs[1], qs)
                consume(i, diag_buf, first=True)
                continue
            par = (t + 1) % 2
            f = k0 + t - 1
            prev = (i - 1 if t == 1 else f - 1) if from_start else f - 1
            produce(f, None, bufs[par], qs)
            consume(prev, bufs[1 - par])

    n_quads = i // 4
    rem = i - 4 * n_quads

    @pl.when(n_quads >= 1)
    def _():
        steps(0, 4, from_start=True)

    def quad(g, carry):
        steps(4 * g, 4)
        return carry

    lax.fori_loop(1, n_quads, quad, 0)

    @pl.when(jnp.logical_and(n_quads == 0, rem >= 2))
    def _():
        steps(0, 2, from_start=True)

    @pl.when(jnp.logical_and(n_quads == 0, rem == 1))
    def _():
        steps(0, 1, from_start=True)

    @pl.when(jnp.logical_and(n_quads >= 1, rem >= 2))
    def _():
        steps(4 * n_quads, 2)

    @pl.when(jnp.logical_and(i >= 3, rem % 2 == 1))
    def _():
        steps(i - 1, 1)

    last = jnp.where(i >= 2, i - 2, 0)
    nxt = jnp.minimum(i + 1, nq - 1)

    def finish():
        den = slice(V_HEAD_DIM, V_HEAD_DIM + 1)
        o = (acc_ref[0, :V_HEAD_DIM] / acc_ref[0, den]
             - lam_ref[0] * (acc_ref[1, :V_HEAD_DIM] / acc_ref[1, den]))
        ms = jnp.mean(o * o, axis=0, keepdims=True)
        o = o * lax.rsqrt(ms + RMS_EPS) * g_ref[...]
        o_ref[0] = o.T.astype(o_ref.dtype)

    def last_step(j, buf, first=False):
        qs_next = masked_queries(qn_ref)
        consume(j, buf, first=first)
        finish()
        produce(nxt, bias_ref[0], diag_buf, qs_next)

    @pl.when(i == 0)
    def _():
        last_step(0, diag_buf, first=True)

    for parity in range(2):
        @pl.when(jnp.logical_and(i >= 1, i % 2 == parity))
        def _():
            last_step(last, bufs[parity])


def _attention(q4, k4, vt5, bias_vecs, lam, gfull, tq, tk, cast_weights=()):
    batch, _, seq, _ = q4.shape
    nq = seq // tq
    nk = seq // tk
    n_steps = batch * N_HEADS * nq
    bf16_rows = 16
    cast_specs = []
    for w in cast_weights:
        rows, cols = w.shape
        rb = bf16_rows * pl.cdiv(rows, bf16_rows * n_steps)
        assert rows % rb == 0
        last = rows // rb - 1
        cast_specs.append(pl.BlockSpec(
            (rb, cols), lambda b, h, i, last=last: (jnp.minimum((b * N_HEADS + h) * nq + i, last), 0)))
    kern = functools.partial(_attn_kernel, tq=tq, tk=tk, n_cast=len(cast_weights))
    outs = pl.pallas_call(
        kern,
        grid=(batch, N_HEADS, nq),
        in_specs=[
            pl.BlockSpec(memory_space=pltpu.SMEM),
            pl.BlockSpec((1, 1, tq, LANES), lambda b, h, i: (b, h, i, 0)),
            pl.BlockSpec((1, 1, tq, LANES), lambda b, h, i: (b, h, jnp.minimum(i + 1, nq - 1), 0)),
            pl.BlockSpec((1, 1, seq, LANES), lambda b, h, i: (b, h, 0, 0)),
            pl.BlockSpec((1, 1, nk, V_ROWS, tk), lambda b, h, i: (b, h, 0, 0, 0)),
            pl.BlockSpec((1, 2, 1, 2 * tq), lambda b, h, i: (h, 0, 0, 0)),
            pl.BlockSpec((V_HEAD_DIM, tq), lambda b, h, i: (0, 0)),
        ] + cast_specs,
        out_specs=[pl.BlockSpec((1, tq, V_HEAD_DIM), lambda b, h, i: (b, i, h))] + cast_specs,
        out_shape=[jax.ShapeDtypeStruct((batch, seq, N_HEADS * V_HEAD_DIM), BF16)]
        + [jax.ShapeDtypeStruct(w.shape, BF16) for w in cast_weights],
        scratch_shapes=[
            pltpu.VMEM((2, tk, tq), F32),
            pltpu.VMEM((2, tk, tq), F32),
            pltpu.VMEM((2, tk, tq), F32),
            pltpu.VMEM((2, tk, tq), F32),
            pltpu.VMEM((2, 1, tq), F32),
            pltpu.VMEM((2, 1, tq), F32),
            pltpu.VMEM((2, 1, tq), F32),
            pltpu.VMEM((2, V_ROWS, tq), F32),
            pltpu.VMEM((2, 1, tq), F32),
        ],
        compiler_params=_cparams(("parallel", "parallel", "arbitrary")),
        name="diff_attention",
    )(lam, q4, q4, k4, vt5, bias_vecs, gfull, *cast_weights)
    return outs[0], tuple(outs[1:])


def _mixer_out_kernel(ch_ref, cb_ref, cc_ref, ga_ref, gc_ref, chp_ref, ccp_ref, a_ref, cw_ref,
                      woa_ref, woc_ref, wo_ref, x_ref, o_ref, *, tm, seq):
    i = pl.program_id(0)
    u = cc_ref[...].astype(F32) * ch_ref[...].astype(F32)
    up = ccp_ref[...].astype(F32) * chp_ref[...].astype(F32)
    up = jnp.where((i * tm) % seq == 0, 0.0, up)
    row = lax.broadcasted_iota(jnp.int32, u.shape, 0)
    u1 = jnp.where(row == 0, up[7:8], pltpu.roll(u, 1, 0))
    u2 = jnp.where(row == 0, up[6:7], jnp.where(row == 1, up[7:8], pltpu.roll(u, 2, 0)))
    cw = cw_ref[...]
    y = cw[0:1] * u2 + cw[1:2] * u1 + cw[2:3] * u
    c = (cb_ref[...].astype(F32) * y).astype(BF16)
    pa = jnp.dot(a_ref[...], woa_ref[...], preferred_element_type=F32)
    pc = jnp.dot(c, woc_ref[...], preferred_element_type=F32)
    merged = jax.nn.sigmoid(ga_ref[...].astype(F32)) * pa + jax.nn.sigmoid(gc_ref[...].astype(F32)) * pc
    o_ref[...] = x_ref[...] + jnp.dot(merged.astype(BF16), wo_ref[...], preferred_element_type=F32)


def _mixer_out(z, a, conv_w, woa, woc, wo, x, cols, tm, seq):
    t, d = x.shape
    w = conv_w.shape[1]
    c_h, c_b, c_c, c_ga, c_gc = cols
    blk = lambda col: pl.BlockSpec((tm, w), lambda i: (i, col))
    prev = lambda col: pl.BlockSpec((8, w), lambda i: (jnp.maximum(i * (tm // 8) - 1, 0), col))
    full = lambda arr: pl.BlockSpec(arr.shape, lambda i: (0, 0))
    kern = functools.partial(_mixer_out_kernel, tm=tm, seq=seq)
    return pl.pallas_call(
        kern,
        grid=(t // tm,),
        in_specs=[blk(c_h), blk(c_b), blk(c_c), blk(c_ga), blk(c_gc), prev(c_h), prev(c_c),
                  pl.BlockSpec((tm, a.shape[1]), lambda i: (i, 0)),
                  full(conv_w), full(woa), full(woc), full(wo),
                  pl.BlockSpec((tm, d), lambda i: (i, 0))],
        out_specs=pl.BlockSpec((tm, d), lambda i: (i, 0)),
        out_shape=jax.ShapeDtypeStruct((t, d), F32),
        compiler_params=_cparams(("parallel",)),
        name="mixer_out",
    )(z, z, z, z, z, z, z, a, conv_w, woa, woc, wo, x)


def _dense_ffn_kernel(x_ref, g_ref, wg_ref, wu_ref, wd_ref, o_ref):
    x = x_ref[...]
    h = _rms(x, g_ref[...]).astype(BF16)
    gate = jnp.dot(h, wg_ref[...], preferred_element_type=F32)
    up = jnp.dot(h, wu_ref[...], preferred_element_type=F32)
    act = (gate * jax.nn.sigmoid(gate) * up).astype(BF16)
    o_ref[...] = x + jnp.dot(act, wd_ref[...], preferred_element_type=F32)


def _dense_ffn(x, g, wg, wu, wd, tm):
    t, d = x.shape
    resident = lambda arr: pl.BlockSpec(arr.shape, lambda i: (0, 0), pipeline_mode=pl.Buffered(1))
    return pl.pallas_call(
        _dense_ffn_kernel,
        grid=(t // tm,),
        in_specs=[pl.BlockSpec((tm, d), lambda i: (i, 0)),
                  pl.BlockSpec((1, d), lambda i: (0, 0)),
                  resident(wg), resident(wu), resident(wd)],
        out_specs=pl.BlockSpec((tm, d), lambda i: (i, 0)),
        out_shape=jax.ShapeDtypeStruct((t, d), F32),
        compiler_params=_cparams(("parallel",)),
        name="dense_ffn",
    )(x, g.reshape(1, d), wg, wu, wd)


def _split_bf16(v):
    hi = v.astype(BF16)
    lo = (v - hi.astype(F32)).astype(BF16)
    return hi, lo


def _router_kernel(x_ref, g_ref, rw_ref, h_ref, r_ref):
    h = _rms(x_ref[...], g_ref[...])
    h_ref[...] = h.astype(BF16)
    h_hi, h_lo = _split_bf16(h)
    w_hi, w_lo = _split_bf16(rw_ref[...])
    logits = (jnp.dot(h_hi, w_hi, preferred_element_type=F32)
              + jnp.dot(h_hi, w_lo, preferred_element_type=F32)
              + jnp.dot(h_lo, w_hi, preferred_element_type=F32))
    lane = lax.broadcasted_iota(jnp.int32, logits.shape, 1)
    logits = jnp.where(lane < N_EXPERTS, logits, -jnp.inf)
    v1 = jnp.max(logits, axis=-1, keepdims=True)
    i1 = jnp.min(jnp.where(logits == v1, lane, LANES), axis=-1, keepdims=True)
    rest = jnp.where(lane == i1, -jnp.inf, logits)
    v2 = jnp.max(rest, axis=-1, keepdims=True)
    i2 = jnp.min(jnp.where(rest == v2, lane, LANES), axis=-1, keepdims=True)
    e2 = jnp.exp(v2 - v1)
    g1 = 1.0 / (1.0 + e2)
    g2 = e2 / (1.0 + e2)
    out = jnp.where(lane == 0, i1.astype(F32),
                    jnp.where(lane == 1, i2.astype(F32),
                              jnp.where(lane == 2, g1, jnp.where(lane == 3, g2, 0.0))))
    r_ref[...] = out


def _router(x, g, rw_pad, tm):
    t, d = x.shape
    return pl.pallas_call(
        _router_kernel,
        grid=(t // tm,),
        in_specs=[pl.BlockSpec((tm, d), lambda i: (i, 0)),
                  pl.BlockSpec((1, d), lambda i: (0, 0)),
                  pl.BlockSpec(rw_pad.shape, lambda i: (0, 0))],
        out_specs=[pl.BlockSpec((tm, d), lambda i: (i, 0)),
                   pl.BlockSpec((tm, LANES), lambda i: (i, 0))],
        out_shape=[jax.ShapeDtypeStruct((t, d), BF16),
                   jax.ShapeDtypeStruct((t, LANES), F32)],
        compiler_params=_cparams(("parallel",)),
        name="moe_router",
    )(x, g.reshape(1, d), rw_pad)


def _gmm_kernel(te_ref, ta_ref, xs_ref, wg_ref, wu_ref, wd_ref, *rest, nch, tile0):
    o_ref, acc_ref = rest[-2:]
    i = tile0 + pl.program_id(0)
    c = pl.program_id(1)
    active = ta_ref[i] != 0

    @pl.when(active)
    def _():
        x = xs_ref[...]
        gate = jnp.dot(x, wg_ref[0], preferred_element_type=F32)
        up = jnp.dot(x, wu_ref[0], preferred_element_type=F32)
        act = (gate * jax.nn.sigmoid(gate) * up).astype(BF16)
        contrib = jnp.dot(act, wd_ref[0], preferred_element_type=F32)

        @pl.when(c == 0)
        def _():
            acc_ref[...] = contrib

        @pl.when(c != 0)
        def _():
            acc_ref[...] += contrib

        @pl.when(c == nch - 1)
        def _():
            o_ref[...] = acc_ref[...].astype(o_ref.dtype)

    @pl.when(jnp.logical_and(jnp.logical_not(active), c == nch - 1))
    def _():
        o_ref[...] = jnp.zeros(o_ref.shape, o_ref.dtype)


def _grouped_ffn(tile_expert, tile_active, xs_part, w_gu, w_down, ys_prev, tile0, n_tiles, tm, tf):
    p_part, d = xs_part.shape
    ff = w_down.shape[1]
    nch = ff // tf
    kern = functools.partial(_gmm_kernel, nch=nch, tile0=tile0)

    def chunk(c, ta, i):
        return jnp.where(ta[tile0 + i] != 0, c, nch - 1)

    in_specs = [
        pl.BlockSpec((tm, d), lambda i, c, te, ta: (i, 0)),
        pl.BlockSpec((1, d, tf), lambda i, c, te, ta: (te[tile0 + i], 0, chunk(c, ta, i))),
        pl.BlockSpec((1, d, tf), lambda i, c, te, ta: (te[tile0 + i], 0, nch + chunk(c, ta, i))),
        pl.BlockSpec((1, tf, d), lambda i, c, te, ta: (te[tile0 + i], chunk(c, ta, i), 0)),
    ]
    in_specs.append(pl.BlockSpec(memory_space=pl.ANY))
    operands = [tile_expert, tile_active, xs_part, w_gu, w_gu, w_down, ys_prev]
    aliases = {len(operands) - 1: 0}
    grid_spec = pltpu.PrefetchScalarGridSpec(
        num_scalar_prefetch=2,
        grid=(p_part // tm, nch),
        in_specs=in_specs,
        out_specs=pl.BlockSpec((tm, d), lambda i, c, te, ta: (tile0 + i, 0)),
        scratch_shapes=[pltpu.VMEM((tm, d), F32)],
    )
    return pl.pallas_call(
        kern,
        grid_spec=grid_spec,
        out_shape=jax.ShapeDtypeStruct((n_tiles * tm, d), BF16),
        input_output_aliases=aliases,
        compiler_params=_cparams(("arbitrary", "arbitrary")),
        name="moe_grouped_ffn",
    )(*operands)


def _combine_kernel(x_ref, r_ref, y0_ref, y1_ref, g_ref, o_ref, *, final_norm):
    route = r_ref[...]
    x = (x_ref[...] + route[:, 2:3] * y0_ref[...].astype(F32)
         + route[:, 3:4] * y1_ref[...].astype(F32))
    o_ref[...] = _rms(x, g_ref[...]) if final_norm else x


def _combine(x, route, ysel, g, tm, final_norm):
    t, d = x.shape
    nt = t // tm
    kern = functools.partial(_combine_kernel, final_norm=final_norm)
    return pl.pallas_call(
        kern,
        grid=(nt,),
        in_specs=[pl.BlockSpec((tm, d), lambda i: (i, 0)),
                  pl.BlockSpec((tm, LANES), lambda i: (i, 0)),
                  pl.BlockSpec((tm, d), lambda i: (i, 0)),
                  pl.BlockSpec((tm, d), lambda i: (nt + i, 0)),
                  pl.BlockSpec((1, d), lambda i: (0, 0))],
        out_specs=pl.BlockSpec((tm, d), lambda i: (i, 0)),
        out_shape=jax.ShapeDtypeStruct((t, d), F32),
        compiler_params=_cparams(("parallel",)),
        name="moe_combine",
    )(x, route, ysel, ysel, g.reshape(1, d))


def _final_norm_kernel(x_ref, g_ref, o_ref):
    o_ref[...] = _rms(x_ref[...], g_ref[...])


def _final_norm(x, g, tm):
    t, d = x.shape
    return pl.pallas_call(
        _final_norm_kernel,
        grid=(t // tm,),
        in_specs=[pl.BlockSpec((tm, d), lambda i: (i, 0)), pl.BlockSpec((1, d), lambda i: (0, 0))],
        out_specs=pl.BlockSpec((tm, d), lambda i: (i, 0)),
        out_shape=jax.ShapeDtypeStruct((t, d), F32),
        compiler_params=_cparams(("parallel",)),
        name="final_norm",
    )(x, g.reshape(1, d))


def _t5_causal_bucket(dist):
    max_exact = NUM_BUCKETS // 2
    is_small = dist < max_exact
    d = jnp.maximum(dist, 1).astype(F32)
    large = max_exact + (jnp.log(d / max_exact) / math.log(MAX_DISTANCE / max_exact)
                         * (NUM_BUCKETS - max_exact)).astype(jnp.int32)
    large = jnp.minimum(large, NUM_BUCKETS - 1)
    return jnp.where(is_small, dist, large)


def _bias_vectors(rel_bias, seq, tq, tk):
    assert tq == tk and tk + 1 >= MAX_DISTANCE
    rb = rel_bias.astype(F32)
    bias_dist = rb[_t5_causal_bucket(jnp.arange(seq, dtype=jnp.int32))].T
    rel = (bias_dist - rb[NUM_BUCKETS - 1][:, None]) * LOG2E
    d_diag = jnp.arange(2 * tq, dtype=jnp.int32) - (tk - 1)
    d_sub = jnp.minimum(d_diag + tk, seq - 1)
    v_diag = jnp.where(d_diag >= 0, rel[:, jnp.clip(d_diag, 0, seq - 1)], NEG_BIG)
    v_sub = rel[:, d_sub]
    return jnp.stack([v_diag, v_sub], axis=1)[:, :, None, :]


def _moe_dispatch(route, tm):
    t = route.shape[0]
    expert = route[:, :2].astype(jnp.int32).reshape(-1)
    onehot = (expert[:, None] == jnp.arange(N_EXPERTS, dtype=jnp.int32)[None, :]).astype(jnp.int32)
    rank = jnp.take_along_axis(jnp.cumsum(onehot, axis=0), expert[:, None], axis=1)[:, 0] - 1
    counts = jnp.sum(onehot, axis=0)
    padded = ((counts + tm - 1) // tm) * tm
    ends = jnp.cumsum(padded)
    starts = ends - padded
    pos = starts[expert] + rank
    p = 2 * t + N_EXPERTS * tm
    row_token = jnp.zeros((p,), jnp.int32).at[pos].set(jnp.arange(2 * t, dtype=jnp.int32) // 2)
    tile_start = jnp.arange(p // tm, dtype=jnp.int32) * tm
    tile_active = (tile_start < ends[-1]).astype(jnp.int32)
    tile_expert = jnp.minimum(jnp.sum((tile_start[:, None] >= ends[None, :]).astype(jnp.int32), axis=1),
                              N_EXPERTS - 1)
    last_expert = tile_expert[jnp.maximum(ends[-1] // tm - 1, 0)]
    tile_expert = jnp.where(tile_active != 0, tile_expert, last_expert).astype(jnp.int32)
    pos_by_choice = pos.reshape(t, 2).T.reshape(-1)
    return pos_by_choice, row_token, tile_expert, tile_active


def kernel(x, rel_bias, mix_norm_g, w_in, lam_qk, subln_g, conv_w, w_o_attn, w_o_conv, w_o, ffn_norm_g,
           dense_w_gate_up, dense_w_down, router_w, expert_w_gate_up, expert_w_down, final_norm_g):
    batch, seq, d = x.shape
    t = batch * seq
    depth = w_in.shape[0]
    qk_w = N_HEADS * 2 * HEAD_DIM
    v_w = N_HEADS * V_HEAD_DIM
    tq = min(ATTN_TQ, seq)
    tk = tq
    tm = min(ROW_TILE, seq)

    xf = x.reshape(t, d)
    bias_vecs = _bias_vectors(rel_bias, seq, tq, tk)
    conv_cols = tuple(range(5))

    for l in range(depth):
        w3 = w_in[l].astype(BF16).reshape(d, -1, v_w).transpose(1, 0, 2)
        q4, k4, vt5, z = _in_proj(xf, mix_norm_g[l], w3, batch, seq, tk)

        lq = lam_qk[l].astype(F32)
        lam_init = 0.8 - 0.6 * math.exp(-0.3 * l)
        lam = (jnp.exp(jnp.sum(lq[0] * lq[1])) - jnp.exp(jnp.sum(lq[2] * lq[3])) + lam_init).reshape(1)
        gsub = subln_g[l].astype(F32) * (1.0 - lam_init)
        gfull = jnp.broadcast_to(gsub[:, None], (V_HEAD_DIM, tq))
        cast_weights = ()
        if l % 2 == 0 and l + 1 < depth:
            e = (l + 1) // 2
            cast_weights = (expert_w_gate_up[e].reshape(-1, expert_w_gate_up.shape[-1]),
                            expert_w_down[e].reshape(-1, expert_w_down.shape[-1]))
        a, cast_bf16 = _attention(q4, k4, vt5, bias_vecs, lam, gfull, tq, tk, cast_weights)
        if cast_weights:
            expert_bf16 = (cast_bf16[0].reshape(expert_w_gate_up.shape[1:]),
                           cast_bf16[1].reshape(expert_w_down.shape[1:]))

        xf = _mixer_out(z, a.reshape(t, v_w), conv_w[l].astype(F32), w_o_attn[l].astype(BF16),
                        w_o_conv[l].astype(BF16), w_o[l].astype(BF16), xf, conv_cols, tm, seq)

        last = l == depth - 1
        if l % 2 == 0:
            wgu = dense_w_gate_up[l // 2]
            ff = wgu.shape[1] // 2
            xf = _dense_ffn(xf, ffn_norm_g[l], wgu[:, :ff].astype(BF16), wgu[:, ff:].astype(BF16),
                            dense_w_down[l // 2].astype(BF16), tm)
            if last:
                xf = _final_norm(xf, final_norm_g, tm)
        else:
            mt = min(MOE_TILE, seq)
            rw_pad = jnp.zeros((d, LANES), F32).at[:, :N_EXPERTS].set(router_w[l // 2].astype(F32))
            hb, route = _router(xf, ffn_norm_g[l], rw_pad, tm)
            pos_by_choice, row_token, tile_expert, tile_active = _moe_dispatch(route, mt)
            ff = expert_w_down.shape[2]
            n_tiles = row_token.shape[0] // mt
            part_sizes = ([n_tiles // 8 * e for e in MOE_PART_EIGHTHS] if n_tiles % 8 == 0 else [n_tiles])
            ys = jnp.zeros((n_tiles * mt, d), BF16)
            tile0 = 0
            for part_tiles in part_sizes:
                rows = slice(tile0 * mt, (tile0 + part_tiles) * mt)
                ys = _grouped_ffn(tile_expert, tile_active, hb[row_token[rows]], expert_bf16[0], expert_bf16[1],
                                  ys, tile0, n_tiles, mt, ff // 2)
                tile0 += part_tiles
            xf = _combine(xf, route, ys[pos_by_choice], final_norm_g, tm, final_norm=last)

    return xf.reshape(batch, seq, d)
```

```python
import functools
import math

import jax
import jax.numpy as jnp
from jax import lax
from jax.experimental import pallas as pl
from jax.experimental.pallas import tpu as pltpu

F32 = jnp.float32
BF16 = jnp.bfloat16

N_HEADS = 8
HEAD_DIM = 64
V_HEAD_DIM = 2 * HEAD_DIM
V_ROWS = V_HEAD_DIM + 16
NUM_BUCKETS = 32
MAX_DISTANCE = 128
N_EXPERTS = 8
RMS_EPS = 1e-6

LANES = 128
VMEM_LIMIT_BYTES = 56 * 1024 * 1024

LOG2E = math.log2(math.e)
NEG_BIG = -1e30

ATTN_TQ = 512
ATTN_TK = 512
ATTN_QCHAIN = 256
ROW_TILE = 512
MOE_TILE = 512
MOE_PART_EIGHTHS = (1, 2, 5)


def _cparams(sem):
    return pltpu.CompilerParams(dimension_semantics=sem, vmem_limit_bytes=VMEM_LIMIT_BYTES)


def _rms(x, g):
    ms = jnp.mean(x * x, axis=-1, keepdims=True)
    return x * lax.rsqrt(ms + RMS_EPS) * g


def _in_proj_kernel(x_ref, g_ref, w_ref, q_ref, k_ref, vt_ref, z_ref):
    h = _rms(x_ref[...], g_ref[...]).astype(BF16)

    def proj(blk):
        return jnp.dot(h, w_ref[blk], preferred_element_type=F32)

    for dst, blk in ((q_ref, 0), (k_ref, 1)):
        y = proj(blk).astype(dst.dtype)
        for hh in range(N_HEADS):
            dst[0, hh] = y[:, hh * LANES:(hh + 1) * LANES]

    yt = proj(2).astype(vt_ref.dtype).T
    pad_rows = V_ROWS - V_HEAD_DIM
    row = lax.broadcasted_iota(jnp.int32, (pad_rows, yt.shape[1]), 0)
    ones_rows = jnp.where(row == 0, 1.0, 0.0).astype(vt_ref.dtype)
    for hh in range(N_HEADS):
        vt_ref[0, hh, 0, :V_HEAD_DIM, :] = yt[hh * V_HEAD_DIM:(hh + 1) * V_HEAD_DIM, :]
        vt_ref[0, hh, 0, V_HEAD_DIM:, :] = ones_rows

    width = w_ref.shape[2]
    for c in range(z_ref.shape[1] // width):
        z_ref[:, c * width:(c + 1) * width] = proj(3 + c).astype(z_ref.dtype)


def _in_proj(x, g, w3, batch, seq, tm):
    t, d = x.shape
    nblk, _, width = w3.shape
    assert width == N_HEADS * 2 * HEAD_DIM == N_HEADS * V_HEAD_DIM
    nt = seq // tm
    n_rest = (nblk - 3) * width
    head_spec = pl.BlockSpec((1, N_HEADS, tm, LANES), lambda i: (i // nt, 0, i % nt, 0))
    return pl.pallas_call(
        _in_proj_kernel,
        grid=(t // tm,),
        in_specs=[
            pl.BlockSpec((tm, d), lambda i: (i, 0)),
            pl.BlockSpec((1, d), lambda i: (0, 0)),
            pl.BlockSpec(w3.shape, lambda i: (0, 0, 0), pipeline_mode=pl.Buffered(1)),
        ],
        out_specs=[
            head_spec, head_spec,
            pl.BlockSpec((1, N_HEADS, 1, V_ROWS, tm), lambda i: (i // nt, 0, i % nt, 0, 0)),
            pl.BlockSpec((tm, n_rest), lambda i: (i, 0)),
        ],
        out_shape=[jax.ShapeDtypeStruct((batch, N_HEADS, seq, LANES), BF16),
                   jax.ShapeDtypeStruct((batch, N_HEADS, seq, LANES), BF16),
                   jax.ShapeDtypeStruct((batch, N_HEADS, nt, V_ROWS, tm), BF16),
                   jax.ShapeDtypeStruct((t, n_rest), BF16)],
        compiler_params=_cparams(("parallel",)),
        name="in_proj",
    )(x, g.reshape(1, d), w3)


def _attn_kernel(lam_ref, q_ref, qn_ref, k_ref, vt_ref, ub_ref, g_ref, *refs, tq, tk, n_cast):
    cast_in = refs[:n_cast]
    o_ref = refs[n_cast]
    cast_out = refs[n_cast + 1:2 * n_cast + 1]
    bias_ref, s0_ref, s1_ref, sd_ref, mc0_ref, mc1_ref, mcd_ref, acc_ref, m_ref = refs[2 * n_cast + 1:]
    bufs = ((s0_ref, mc0_ref), (s1_ref, mc1_ref))
    diag_buf = (sd_ref, mcd_ref)
    i = pl.program_id(2)
    nq = pl.num_programs(2)

    for w_ref, wb_ref in zip(cast_in, cast_out):
        wb_ref[...] = w_ref[...].astype(wb_ref.dtype)

    def masked_queries(ref):
        q = ref[0, 0].astype(F32) * (LOG2E * HEAD_DIM ** -0.5)
        lane = lax.broadcasted_iota(jnp.int32, q.shape, 1)
        return (jnp.where(lane < HEAD_DIM, q, 0.0).astype(BF16),
                jnp.where(lane >= HEAD_DIM, q, 0.0).astype(BF16))

    def produce(j, bias, buf, queries):
        s_ref, mc_ref = buf
        row0 = pl.multiple_of(j * tk, tk)
        k = k_ref[0, 0, pl.ds(row0, tk), :]
        for mp in range(2):
            s = lax.dot_general(k, queries[mp], (((1,), (1,)), ((), ())), preferred_element_type=F32)
            if bias is not None:
                s = s + bias
            s_ref[mp] = s
            mc_ref[mp] = jnp.max(s, axis=0, keepdims=True)

    def consume(j, buf, first=False):
        s_ref, mc_ref = buf
        vt = vt_ref[0, 0, j]
        for mp in range(2):
            for h in range(tq // ATTN_QCHAIN):
                cols = slice(h * ATTN_QCHAIN, (h + 1) * ATTN_QCHAIN)
                if first:
                    m_new = mc_ref[mp, :, cols]
                else:
                    m_old = m_ref[mp, :, cols]
                    m_new = jnp.maximum(m_old, mc_ref[mp, :, cols])
                p = jnp.exp2(s_ref[mp, :, cols] - m_new)
                pv = jnp.dot(vt, p.astype(BF16), preferred_element_type=F32)
                if first:
                    acc_ref[mp, :, cols] = pv
                else:
                    acc_ref[mp, :, cols] = jnp.exp2(m_old - m_new) * acc_ref[mp, :, cols] + pv
                m_ref[mp, :, cols] = m_new

    qs = masked_queries(q_ref)

    @pl.when(i == 0)
    def _():
        for t in range(2):
            u = jnp.broadcast_to(ub_ref[0, t], (tk, 2 * tq))
            bias_ref[t] = pltpu.roll(u, 1, 1, stride=1, stride_axis=0)[:, tq:]
        produce(i, bias_ref[0], diag_buf, qs)

    def steps(k0, count, from_start=False):
        for t in range(count):
            if from_start and t == 0:
                produce(i - 1, bias_ref[1], bufs[1], qs)
                consume(i, diag_buf, first=True)
                continue
            par = (t + 1) % 2
            f = k0 + t - 1
            prev = (i - 1 if t == 1 else f - 1) if from_start else f - 1
            produce(f, None, bufs[par], qs)
            consume(prev, bufs[1 - par])

    n_quads = i // 4
    rem = i - 4 * n_quads

    @pl.when(n_quads >= 1)
    def _():
        steps(0, 4, from_start=True)

    n_octets = (n_quads - 1) // 2

    def octet(g, carry):
        steps(4 + 8 * g, 8)
        return carry

    lax.fori_loop(0, n_octets, octet, 0)

    @pl.when(jnp.logical_and(n_quads >= 2, n_quads % 2 == 0))
    def _():
        steps(4 * n_quads - 4, 4)

    @pl.when(jnp.logical_and(n_quads == 0, rem >= 2))
    def _():
        steps(0, 2, from_start=True)

    @pl.when(jnp.logical_and(n_quads == 0, rem == 1))
    def _():
        steps(0, 1, from_start=True)

    @pl.when(jnp.logical_and(n_quads >= 1, rem >= 2))
    def _():
        steps(4 * n_quads, 2)

    @pl.when(jnp.logical_and(i >= 3, rem % 2 == 1))
    def _():
        steps(i - 1, 1)

    last = jnp.where(i >= 2, i - 2, 0)
    nxt = jnp.minimum(i + 1, nq - 1)

    def finish():
        den = slice(V_HEAD_DIM, V_HEAD_DIM + 1)
        o = (acc_ref[0, :V_HEAD_DIM] / acc_ref[0, den]
             - lam_ref[0] * (acc_ref[1, :V_HEAD_DIM] / acc_ref[1, den]))
        ms = jnp.mean(o * o, axis=0, keepdims=True)
        o = o * lax.rsqrt(ms + RMS_EPS) * g_ref[...]
        o_ref[0] = o.T.astype(o_ref.dtype)

    def last_step(j, buf, first=False):
        qs_next = masked_queries(qn_ref)
        consume(j, buf, first=first)
        finish()
        produce(nxt, bias_ref[0], diag_buf, qs_next)

    @pl.when(i == 0)
    def _():
        last_step(0, diag_buf, first=True)

    for parity in range(2):
        @pl.when(jnp.logical_and(i >= 1, i % 2 == parity))
        def _():
            last_step(last, bufs[parity])


def _attention(q4, k4, vt5, bias_vecs, lam, gfull, tq, tk, cast_weights=()):
    batch, _, seq, _ = q4.shape
    nq = seq // tq
    nk = seq // tk
    n_steps = batch * N_HEADS * nq
    bf16_rows = 16
    cast_specs = []
    for w in cast_weights:
        rows, cols = w.shape
        rb = bf16_rows * pl.cdiv(rows, bf16_rows * n_steps)
        assert rows % rb == 0
        last = rows // rb - 1
        cast_specs.append(pl.BlockSpec(
            (rb, cols), lambda b, h, i, last=last: (jnp.minimum((b * N_HEADS + h) * nq + i, last), 0)))
    kern = functools.partial(_attn_kernel, tq=tq, tk=tk, n_cast=len(cast_weights))
    outs = pl.pallas_call(
        kern,
        grid=(batch, N_HEADS, nq),
        in_specs=[
            pl.BlockSpec(memory_space=pltpu.SMEM),
            pl.BlockSpec((1, 1, tq, LANES), lambda b, h, i: (b, h, i, 0)),
            pl.BlockSpec((1, 1, tq, LANES), lambda b, h, i: (b, h, jnp.minimum(i + 1, nq - 1), 0)),
            pl.BlockSpec((1, 1, seq, LANES), lambda b, h, i: (b, h, 0, 0)),
            pl.BlockSpec((1, 1, nk, V_ROWS, tk), lambda b, h, i: (b, h, 0, 0, 0)),
            pl.BlockSpec((1, 2, 1, 2 * tq), lambda b, h, i: (h, 0, 0, 0)),
            pl.BlockSpec((V_HEAD_DIM, tq), lambda b, h, i: (0, 0)),
        ] + cast_specs,
        out_specs=[pl.BlockSpec((1, tq, V_HEAD_DIM), lambda b, h, i: (b, i, h))] + cast_specs,
        out_shape=[jax.ShapeDtypeStruct((batch, seq, N_HEADS * V_HEAD_DIM), BF16)]
        + [jax.ShapeDtypeStruct(w.shape, BF16) for w in cast_weights],
        scratch_shapes=[
            pltpu.VMEM((2, tk, tq), F32),
            pltpu.VMEM((2, tk, tq), F32),
            pltpu.VMEM((2, tk, tq), F32),
            pltpu.VMEM((2, tk, tq), F32),
            pltpu.VMEM((2, 1, tq), F32),
            pltpu.VMEM((2, 1, tq), F32),
            pltpu.VMEM((2, 1, tq), F32),
            pltpu.VMEM((2, V_ROWS, tq), F32),
            pltpu.VMEM((2, 1, tq), F32),
        ],
        compiler_params=_cparams(("parallel", "parallel", "arbitrary")),
        name="diff_attention",
    )(lam, q4, q4, k4, vt5, bias_vecs, gfull, *cast_weights)
    return outs[0], tuple(outs[1:])


def _mixer_out_kernel(ch_ref, cb_ref, cc_ref, ga_ref, gc_ref, chp_ref, ccp_ref, a_ref, cw_ref,
                      woa_ref, woc_ref, wo_ref, x_ref, o_ref, *, tm, seq):
    i = pl.program_id(0)
    u = cc_ref[...].astype(F32) * ch_ref[...].astype(F32)
    up = ccp_ref[...].astype(F32) * chp_ref[...].astype(F32)
    up = jnp.where((i * tm) % seq == 0, 0.0, up)
    row = lax.broadcasted_iota(jnp.int32, u.shape, 0)
    u1 = jnp.where(row == 0, up[7:8], pltpu.roll(u, 1, 0))
    u2 = jnp.where(row == 0, up[6:7], jnp.where(row == 1, up[7:8], pltpu.roll(u, 2, 0)))
    cw = cw_ref[...]
    y = cw[0:1] * u2 + cw[1:2] * u1 + cw[2:3] * u
    c = (cb_ref[...].astype(F32) * y).astype(BF16)
    pa = jnp.dot(a_ref[...], woa_ref[...], preferred_element_type=F32)
    pc = jnp.dot(c, woc_ref[...], preferred_element_type=F32)
    merged = jax.nn.sigmoid(ga_ref[...].astype(F32)) * pa + jax.nn.sigmoid(gc_ref[...].astype(F32)) * pc
    o_ref[...] = x_ref[...] + jnp.dot(merged.astype(BF16), wo_ref[...], preferred_element_type=F32)


def _mixer_out(z, a, conv_w, woa, woc, wo, x, cols, tm, seq):
    t, d = x.shape
    w = conv_w.shape[1]
    c_h, c_b, c_c, c_ga, c_gc = cols
    blk = lambda col: pl.BlockSpec((tm, w), lambda i: (i, col))
    prev = lambda col: pl.BlockSpec((8, w), lambda i: (jnp.maximum(i * (tm // 8) - 1, 0), col))
    full = lambda arr: pl.BlockSpec(arr.shape, lambda i: (0, 0))
    kern = functools.partial(_mixer_out_kernel, tm=tm, seq=seq)
    return pl.pallas_call(
        kern,
        grid=(t // tm,),
        in_specs=[blk(c_h), blk(c_b), blk(c_c), blk(c_ga), blk(c_gc), prev(c_h), prev(c_c),
                  pl.BlockSpec((tm, a.shape[1]), lambda i: (i, 0)),
                  full(conv_w), full(woa), full(woc), full(wo),
                  pl.BlockSpec((tm, d), lambda i: (i, 0))],
        out_specs=pl.BlockSpec((tm, d), lambda i: (i, 0)),
        out_shape=jax.ShapeDtypeStruct((t, d), F32),
        compiler_params=_cparams(("parallel",)),
        name="mixer_out",
    )(z, z, z, z, z, z, z, a, conv_w, woa, woc, wo, x)


def _dense_ffn_kernel(x_ref, g_ref, wg_ref, wu_ref, wd_ref, o_ref):
    x = x_ref[...]
    h = _rms(x, g_ref[...]).astype(BF16)
    gate = jnp.dot(h, wg_ref[...], preferred_element_type=F32)
    up = jnp.dot(h, wu_ref[...], preferred_element_type=F32)
    act = (gate * jax.nn.sigmoid(gate) * up).astype(BF16)
    o_ref[...] = x + jnp.dot(act, wd_ref[...], preferred_element_type=F32)


def _dense_ffn(x, g, wg, wu, wd, tm):
    t, d = x.shape
    resident = lambda arr: pl.BlockSpec(arr.shape, lambda i: (0, 0), pipeline_mode=pl.Buffered(1))
    return pl.pallas_call(
        _dense_ffn_kernel,
        grid=(t // tm,),
        in_specs=[pl.BlockSpec((tm, d), lambda i: (i, 0)),
                  pl.BlockSpec((1, d), lambda i: (0, 0)),
                  resident(wg), resident(wu), resident(wd)],
        out_specs=pl.BlockSpec((tm, d), lambda i: (i, 0)),
        out_shape=jax.ShapeDtypeStruct((t, d), F32),
        compiler_params=_cparams(("parallel",)),
        name="dense_ffn",
    )(x, g.reshape(1, d), wg, wu, wd)


def _split_bf16(v):
    hi = v.astype(BF16)
    lo = (v - hi.astype(F32)).astype(BF16)
    return hi, lo


def _router_kernel(x_ref, g_ref, rw_ref, h_ref, r_ref):
    h = _rms(x_ref[...], g_ref[...])
    h_ref[...] = h.astype(BF16)
    h_hi, h_lo = _split_bf16(h)
    w_hi, w_lo = _split_bf16(rw_ref[...])
    logits = (jnp.dot(h_hi, w_hi, preferred_element_type=F32)
              + jnp.dot(h_hi, w_lo, preferred_element_type=F32)
              + jnp.dot(h_lo, w_hi, preferred_element_type=F32))
    lane = lax.broadcasted_iota(jnp.int32, logits.shape, 1)
    logits = jnp.where(lane < N_EXPERTS, logits, -jnp.inf)
    v1 = jnp.max(logits, axis=-1, keepdims=True)
    i1 = jnp.min(jnp.where(logits == v1, lane, LANES), axis=-1, keepdims=True)
    rest = jnp.where(lane == i1, -jnp.inf, logits)
    v2 = jnp.max(rest, axis=-1, keepdims=True)
    i2 = jnp.min(jnp.where(rest == v2, lane, LANES), axis=-1, keepdims=True)
    e2 = jnp.exp(v2 - v1)
    g1 = 1.0 / (1.0 + e2)
    g2 = e2 / (1.0 + e2)
    out = jnp.where(lane == 0, i1.astype(F32),
                    jnp.where(lane == 1, i2.astype(F32),
                              jnp.where(lane == 2, g1, jnp.where(lane == 3, g2, 0.0))))
    r_ref[...] = out


def _router(x, g, rw_pad, tm):
    t, d = x.shape
    return pl.pallas_call(
        _router_kernel,
        grid=(t // tm,),
        in_specs=[pl.BlockSpec((tm, d), lambda i: (i, 0)),
                  pl.BlockSpec((1, d), lambda i: (0, 0)),
                  pl.BlockSpec(rw_pad.shape, lambda i: (0, 0))],
        out_specs=[pl.BlockSpec((tm, d), lambda i: (i, 0)),
                   pl.BlockSpec((tm, LANES), lambda i: (i, 0))],
        out_shape=[jax.ShapeDtypeStruct((t, d), BF16),
                   jax.ShapeDtypeStruct((t, LANES), F32)],
        compiler_params=_cparams(("parallel",)),
        name="moe_router",
    )(x, g.reshape(1, d), rw_pad)


def _gmm_kernel(te_ref, ta_ref, xs_ref, wg_ref, wu_ref, wd_ref, *rest, nch, tile0):
    o_ref, acc_ref = rest[-2:]
    i = tile0 + pl.program_id(0)
    c = pl.program_id(1)
    active = ta_ref[i] != 0

    @pl.when(active)
    def _():
        x = xs_ref[...]
        gate = jnp.dot(x, wg_ref[0], preferred_element_type=F32)
        up = jnp.dot(x, wu_ref[0], preferred_element_type=F32)
        act = (gate * jax.nn.sigmoid(gate) * up).astype(BF16)
        contrib = jnp.dot(act, wd_ref[0], preferred_element_type=F32)

        @pl.when(c == 0)
        def _():
            acc_ref[...] = contrib

        @pl.when(c != 0)
        def _():
            acc_ref[...] += contrib

        @pl.when(c == nch - 1)
        def _():
            o_ref[...] = acc_ref[...].astype(o_ref.dtype)

    @pl.when(jnp.logical_and(jnp.logical_not(active), c == nch - 1))
    def _():
        o_ref[...] = jnp.zeros(o_ref.shape, o_ref.dtype)


def _grouped_ffn(tile_expert, tile_active, xs_part, w_gu, w_down, ys_prev, tile0, n_tiles, tm, tf):
    p_part, d = xs_part.shape
    ff = w_down.shape[1]
    nch = ff // tf
    kern = functools.partial(_gmm_kernel, nch=nch, tile0=tile0)

    def chunk(c, ta, i):
        return jnp.where(ta[tile0 + i] != 0, c, nch - 1)

    in_specs = [
        pl.BlockSpec((tm, d), lambda i, c, te, ta: (i, 0)),
        pl.BlockSpec((1, d, tf), lambda i, c, te, ta: (te[tile0 + i], 0, chunk(c, ta, i))),
        pl.BlockSpec((1, d, tf), lambda i, c, te, ta: (te[tile0 + i], 0, nch + chunk(c, ta, i))),
        pl.BlockSpec((1, tf, d), lambda i, c, te, ta: (te[tile0 + i], chunk(c, ta, i), 0)),
    ]
    in_specs.append(pl.BlockSpec(memory_space=pl.ANY))
    operands = [tile_expert, tile_active, xs_part, w_gu, w_gu, w_down, ys_prev]
    aliases = {len(operands) - 1: 0}
    grid_spec = pltpu.PrefetchScalarGridSpec(
        num_scalar_prefetch=2,
        grid=(p_part // tm, nch),
        in_specs=in_specs,
        out_specs=pl.BlockSpec((tm, d), lambda i, c, te, ta: (tile0 + i, 0)),
        scratch_shapes=[pltpu.VMEM((tm, d), F32)],
    )
    return pl.pallas_call(
        kern,
        grid_spec=grid_spec,
        out_shape=jax.ShapeDtypeStruct((n_tiles * tm, d), BF16),
        input_output_aliases=aliases,
        compiler_params=_cparams(("arbitrary", "arbitrary")),
        name="moe_grouped_ffn",
    )(*operands)


def _combine_kernel(x_ref, r_ref, y0_ref, y1_ref, g_ref, o_ref, *, final_norm):
    route = r_ref[...]
    x = (x_ref[...] + route[:, 2:3] * y0_ref[...].astype(F32)
         + route[:, 3:4] * y1_ref[...].astype(F32))
    o_ref[...] = _rms(x, g_ref[...]) if final_norm else x


def _combine(x, route, ysel, g, tm, final_norm):
    t, d = x.shape
    nt = t // tm
    kern = functools.partial(_combine_kernel, final_norm=final_norm)
    return pl.pallas_call(
        kern,
        grid=(nt,),
        in_specs=[pl.BlockSpec((tm, d), lambda i: (i, 0)),
                  pl.BlockSpec((tm, LANES), lambda i: (i, 0)),
                  pl.BlockSpec((tm, d), lambda i: (i, 0)),
                  pl.BlockSpec((tm, d), lambda i: (nt + i, 0)),
                  pl.BlockSpec((1, d), lambda i: (0, 0))],
        out_specs=pl.BlockSpec((tm, d), lambda i: (i, 0)),
        out_shape=jax.ShapeDtypeStruct((t, d), F32),
        compiler_params=_cparams(("parallel",)),
        name="moe_combine",
    )(x, route, ysel, ysel, g.reshape(1, d))


def _final_norm_kernel(x_ref, g_ref, o_ref):
    o_ref[...] = _rms(x_ref[...], g_ref[...])


def _final_norm(x, g, tm):
    t, d = x.shape
    return pl.pallas_call(
        _final_norm_kernel,
        grid=(t // tm,),
        in_specs=[pl.BlockSpec((tm, d), lambda i: (i, 0)), pl.BlockSpec((1, d), lambda i: (0, 0))],
        out_specs=pl.BlockSpec((tm, d), lambda i: (i, 0)),
        out_shape=jax.ShapeDtypeStruct((t, d), F32),
        compiler_params=_cparams(("parallel",)),
        name="final_norm",
    )(x, g.reshape(1, d))


def _t5_causal_bucket(dist):
    max_exact = NUM_BUCKETS // 2
    is_small = dist < max_exact
    d = jnp.maximum(dist, 1).astype(F32)
    large = max_exact + (jnp.log(d / max_exact) / math.log(MAX_DISTANCE / max_exact)
                         * (NUM_BUCKETS - max_exact)).astype(jnp.int32)
    large = jnp.minimum(large, NUM_BUCKETS - 1)
    return jnp.where(is_small, dist, large)


def _bias_vectors(rel_bias, seq, tq, tk):
    assert tq == tk and tk + 1 >= MAX_DISTANCE
    rb = rel_bias.astype(F32)
    bias_dist = rb[_t5_causal_bucket(jnp.arange(seq, dtype=jnp.int32))].T
    rel = (bias_dist - rb[NUM_BUCKETS - 1][:, None]) * LOG2E
    d_diag = jnp.arange(2 * tq, dtype=jnp.int32) - (tk - 1)
    d_sub = jnp.minimum(d_diag + tk, seq - 1)
    v_diag = jnp.where(d_diag >= 0, rel[:, jnp.clip(d_diag, 0, seq - 1)], NEG_BIG)
    v_sub = rel[:, d_sub]
    return jnp.stack([v_diag, v_sub], axis=1)[:, :, None, :]


def _moe_dispatch(route, tm):
    t = route.shape[0]
    expert = route[:, :2].astype(jnp.int32).reshape(-1)
    onehot = (expert[:, None] == jnp.arange(N_EXPERTS, dtype=jnp.int32)[None, :]).astype(jnp.int32)
    rank = jnp.take_along_axis(jnp.cumsum(onehot, axis=0), expert[:, None], axis=1)[:, 0] - 1
    counts = jnp.sum(onehot, axis=0)
    padded = ((counts + tm - 1) // tm) * tm
    ends = jnp.cumsum(padded)
    starts = ends - padded
    pos = starts[expert] + rank
    p = 2 * t + N_EXPERTS * tm
    row_token = jnp.zeros((p,), jnp.int32).at[pos].set(jnp.arange(2 * t, dtype=jnp.int32) // 2)
    tile_start = jnp.arange(p // tm, dtype=jnp.int32) * tm
    tile_active = (tile_start < ends[-1]).astype(jnp.int32)
    tile_expert = jnp.minimum(jnp.sum((tile_start[:, None] >= ends[None, :]).astype(jnp.int32), axis=1),
                              N_EXPERTS - 1)
    last_expert = tile_expert[jnp.maximum(ends[-1] // tm - 1, 0)]
    tile_expert = jnp.where(tile_active != 0, tile_expert, last_expert).astype(jnp.int32)
    pos_by_choice = pos.reshape(t, 2).T.reshape(-1)
    return pos_by_choice, row_token, tile_expert, tile_active


def kernel(x, rel_bias, mix_norm_g, w_in, lam_qk, subln_g, conv_w, w_o_attn, w_o_conv, w_o, ffn_norm_g,
           dense_w_gate_up, dense_w_down, router_w, expert_w_gate_up, expert_w_down, final_norm_g):
    batch, seq, d = x.shape
    t = batch * seq
    depth = w_in.shape[0]
    qk_w = N_HEADS * 2 * HEAD_DIM
    v_w = N_HEADS * V_HEAD_DIM
    tq = min(ATTN_TQ, seq)
    tk = tq
    tm = min(ROW_TILE, seq)

    xf = x.reshape(t, d)
    bias_vecs = _bias_vectors(rel_bias, seq, tq, tk)
    conv_cols = tuple(range(5))

    for l in range(depth):
        w3 = w_in[l].astype(BF16).reshape(d, -1, v_w).transpose(1, 0, 2)
        q4, k4, vt5, z = _in_proj(xf, mix_norm_g[l], w3, batch, seq, tk)

        lq = lam_qk[l].astype(F32)
        lam_init = 0.8 - 0.6 * math.exp(-0.3 * l)
        lam = (jnp.exp(jnp.sum(lq[0] * lq[1])) - jnp.exp(jnp.sum(lq[2] * lq[3])) + lam_init).reshape(1)
        gsub = subln_g[l].astype(F32) * (1.0 - lam_init)
        gfull = jnp.broadcast_to(gsub[:, None], (V_HEAD_DIM, tq))
        cast_weights = ()
        if l % 2 == 0 and l + 1 < depth:
            e = (l + 1) // 2
            cast_weights = (expert_w_gate_up[e].reshape(-1, expert_w_gate_up.shape[-1]),
                            expert_w_down[e].reshape(-1, expert_w_down.shape[-1]))
        a, cast_bf16 = _attention(q4, k4, vt5, bias_vecs, lam, gfull, tq, tk, cast_weights)
        if cast_weights:
            expert_bf16 = (cast_bf16[0].reshape(expert_w_gate_up.shape[1:]),
                           cast_bf16[1].reshape(expert_w_down.shape[1:]))

        xf = _mixer_out(z, a.reshape(t, v_w), conv_w[l].astype(F32), w_o_attn[l].astype(BF16),
                        w_o_conv[l].astype(BF16), w_o[l].astype(BF16), xf, conv_cols, tm, seq)

        last = l == depth - 1
        if l % 2 == 0:
            wgu = dense_w_gate_up[l // 2]
            ff = wgu.shape[1] // 2
            xf = _dense_ffn(xf, ffn_norm_g[l], wgu[:, :ff].astype(BF16), wgu[:, ff:].astype(BF16),
                            dense_w_down[l // 2].astype(BF16), tm)
            if last:
                xf = _final_norm(xf, final_norm_g, tm)
        else:
            mt = min(MOE_TILE, seq)
            rw_pad = jnp.zeros((d, LANES), F32).at[:, :N_EXPERTS].set(router_w[l // 2].astype(F32))
            hb, route = _router(xf, ffn_norm_g[l], rw_pad, tm)
            pos_by_choice, row_token, tile_expert, tile_active = _moe_dispatch(route, mt)
            ff = expert_w_down.shape[2]
            n_tiles = row_token.shape[0] // mt
            part_sizes = ([n_tiles // 8 * e for e in MOE_PART_EIGHTHS] if n_tiles % 8 == 0 else [n_tiles])
            ys = jnp.zeros((n_tiles * mt, d), BF16)
            tile0 = 0
            order = jnp.zeros((), jnp.int32)
            for part_tiles in part_sizes:
                rows = slice(tile0 * mt, (tile0 + part_tiles) * mt)
                xs_part = hb[row_token[rows] + order]
                order = (xs_part[0, 0] != xs_part[0, 0]).astype(jnp.int32)
                ys = _grouped_ffn(tile_expert, tile_active, xs_part, expert_bf16[0], expert_bf16[1],
                                  ys, tile0, n_tiles, mt, ff // 2)
                tile0 += part_tiles
            xf = _combine(xf, route, ys[pos_by_choice], final_norm_g, tm, final_norm=last)

    return xf.reshape(batch, seq, d)
```

```python
import functools
import math

import jax
import jax.numpy as jnp
from jax import lax
from jax.experimental import pallas as pl
from jax.experimental.pallas import tpu as pltpu

F32 = jnp.float32
BF16 = jnp.bfloat16

N_HEADS = 8
HEAD_DIM = 64
V_HEAD_DIM = 2 * HEAD_DIM
NUM_BUCKETS = 32
MAX_DISTANCE = 128
N_EXPERTS = 8
RMS_EPS = 1e-6

LANES = 128
BF16_SUBLANES = 16
VMEM_LIMIT_BYTES = 56 * 1024 * 1024

V_ROWS = V_HEAD_DIM + BF16_SUBLANES

LOG2E = math.log2(math.e)
NEG_BIG = -1e30

ATTN_TQ = 512
ATTN_QCHAIN = 256
ROW_TILE = 512
MOE_TILE = 512
MOE_PART_EIGHTHS = (1, 2, 5)


def _cparams(sem):
    return pltpu.CompilerParams(dimension_semantics=sem, vmem_limit_bytes=VMEM_LIMIT_BYTES)


def _rms(x, g):
    ms = jnp.mean(x * x, axis=-1, keepdims=True)
    return x * lax.rsqrt(ms + RMS_EPS) * g


def _in_proj_kernel(x_ref, g_ref, w_ref, q_ref, k_ref, vt_ref, z_ref):
    h = _rms(x_ref[...], g_ref[...]).astype(BF16)

    def proj(blk):
        return jnp.dot(h, w_ref[blk], preferred_element_type=F32)

    for dst, blk in ((q_ref, 0), (k_ref, 1)):
        y = proj(blk).astype(dst.dtype)
        for hh in range(N_HEADS):
            dst[0, hh] = y[:, hh * LANES:(hh + 1) * LANES]

    yt = proj(2).astype(vt_ref.dtype).T
    pad_rows = V_ROWS - V_HEAD_DIM
    row = lax.broadcasted_iota(jnp.int32, (pad_rows, yt.shape[1]), 0)
    ones_rows = jnp.where(row == 0, 1.0, 0.0).astype(vt_ref.dtype)
    for hh in range(N_HEADS):
        vt_ref[0, hh, 0, :V_HEAD_DIM, :] = yt[hh * V_HEAD_DIM:(hh + 1) * V_HEAD_DIM, :]
        vt_ref[0, hh, 0, V_HEAD_DIM:, :] = ones_rows

    width = w_ref.shape[2]
    for c in range(z_ref.shape[1] // width):
        z_ref[:, c * width:(c + 1) * width] = proj(3 + c).astype(z_ref.dtype)


def _in_proj(x, g, w3, batch, seq, tm):
    t, d = x.shape
    nblk, _, width = w3.shape
    assert width == N_HEADS * 2 * HEAD_DIM == N_HEADS * V_HEAD_DIM
    nt = seq // tm
    n_rest = (nblk - 3) * width
    head_spec = pl.BlockSpec((1, N_HEADS, tm, LANES), lambda i: (i // nt, 0, i % nt, 0))
    return pl.pallas_call(
        _in_proj_kernel,
        grid=(t // tm,),
        in_specs=[
            pl.BlockSpec((tm, d), lambda i: (i, 0)),
            pl.BlockSpec((1, d), lambda i: (0, 0)),
            pl.BlockSpec(w3.shape, lambda i: (0, 0, 0), pipeline_mode=pl.Buffered(1)),
        ],
        out_specs=[
            head_spec, head_spec,
            pl.BlockSpec((1, N_HEADS, 1, V_ROWS, tm), lambda i: (i // nt, 0, i % nt, 0, 0)),
            pl.BlockSpec((tm, n_rest), lambda i: (i, 0)),
        ],
        out_shape=[jax.ShapeDtypeStruct((batch, N_HEADS, seq, LANES), BF16),
                   jax.ShapeDtypeStruct((batch, N_HEADS, seq, LANES), BF16),
                   jax.ShapeDtypeStruct((batch, N_HEADS, nt, V_ROWS, tm), BF16),
                   jax.ShapeDtypeStruct((t, n_rest), BF16)],
        compiler_params=_cparams(("parallel",)),
        name="in_proj",
    )(x, g.reshape(1, d), w3)


def _attn_kernel(lam_ref, q_ref, qn_ref, k_ref, vt_ref, ub_ref, g_ref, *refs, tq, tk, n_cast):
    cast_in = refs[:n_cast]
    o_ref = refs[n_cast]
    cast_out = refs[n_cast + 1:2 * n_cast + 1]
    bias_ref, s0_ref, s1_ref, sd_ref, mc0_ref, mc1_ref, mcd_ref, acc_ref, m_ref = refs[2 * n_cast + 1:]
    bufs = ((s0_ref, mc0_ref), (s1_ref, mc1_ref))
    diag_buf = (sd_ref, mcd_ref)
    i = pl.program_id(2)
    nq = pl.num_programs(2)

    for w_ref, wb_ref in zip(cast_in, cast_out):
        wb_ref[...] = w_ref[...].astype(wb_ref.dtype)

    def masked_queries(ref):
        q = ref[0, 0].astype(F32) * (LOG2E * HEAD_DIM ** -0.5)
        lane = lax.broadcasted_iota(jnp.int32, q.shape, 1)
        return (jnp.where(lane < HEAD_DIM, q, 0.0).astype(BF16),
                jnp.where(lane >= HEAD_DIM, q, 0.0).astype(BF16))

    hk, hq = tk // 2, tq // 2
    nt = (((1,), (1,)), ((), ()))

    def produce(j, bias, buf, queries, diag=False):
        s_ref, mc_ref = buf
        row0 = pl.multiple_of(j * tk, tk)
        k = k_ref[0, 0, pl.ds(row0, tk), :]
        for mp in range(2):
            if diag:
                top = lax.dot_general(k[:hk], queries[mp], nt, preferred_element_type=F32) + bias[:hk]
                low = lax.dot_general(k[hk:], queries[mp][hq:], nt, preferred_element_type=F32) + bias[hk:, hq:]
                s_ref[mp, :hk, :] = top
                s_ref[mp, hk:, hq:] = low
                mc_ref[mp, :, :hq] = jnp.max(top[:, :hq], axis=0, keepdims=True)
                mc_ref[mp, :, hq:] = jnp.maximum(jnp.max(top[:, hq:], axis=0, keepdims=True),
                                                 jnp.max(low, axis=0, keepdims=True))
            else:
                s = lax.dot_general(k, queries[mp], nt, preferred_element_type=F32)
                if bias is not None:
                    s = s + bias
                s_ref[mp] = s
                mc_ref[mp] = jnp.max(s, axis=0, keepdims=True)

    def consume(j, buf, first=False):
        s_ref, mc_ref = buf
        vt = vt_ref[0, 0, j]
        for mp in range(2):
            for h in range(tq // ATTN_QCHAIN):
                cols = slice(h * ATTN_QCHAIN, (h + 1) * ATTN_QCHAIN)
                keys = slice(0, hk) if first and (h + 1) * ATTN_QCHAIN <= hq else slice(0, tk)
                if first:
                    m_new = mc_ref[mp, :, cols]
                else:
                    m_old = m_ref[mp, :, cols]
                    m_new = jnp.maximum(m_old, mc_ref[mp, :, cols])
                p = jnp.exp2(s_ref[mp, keys, cols] - m_new)
                pv = jnp.dot(vt[:, keys], p.astype(BF16), preferred_element_type=F32)
                if first:
                    acc_ref[mp, :, cols] = pv
                else:
                    acc_ref[mp, :, cols] = jnp.exp2(m_old - m_new) * acc_ref[mp, :, cols] + pv
                m_ref[mp, :, cols] = m_new

    qs = masked_queries(q_ref)

    @pl.when(i == 0)
    def _():
        for t in range(2):
            u = jnp.broadcast_to(ub_ref[0, t], (tk, 2 * tq))
            bias_ref[t] = pltpu.roll(u, 1, 1, stride=1, stride_axis=0)[:, tq:]
        produce(i, bias_ref[0], diag_buf, qs, diag=True)

    def steps(k0, count, from_start=False):
        for t in range(count):
            if from_start and t == 0:
                produce(i - 1, bias_ref[1], bufs[1], qs)
                consume(i, diag_buf, first=True)
                continue
            par = (t + 1) % 2
            f = k0 + t - 1
            prev = (i - 1 if t == 1 else f - 1) if from_start else f - 1
            produce(f, None, bufs[par], qs)
            consume(prev, bufs[1 - par])

    n_quads = i // 4
    rem = i - 4 * n_quads

    @pl.when(n_quads >= 1)
    def _():
        steps(0, 4, from_start=True)

    n_octets = (n_quads - 1) // 2

    def octet(g, carry):
        steps(4 + 8 * g, 8)
        return carry

    lax.fori_loop(0, n_octets, octet, 0)

    @pl.when(jnp.logical_and(n_quads >= 2, n_quads % 2 == 0))
    def _():
        steps(4 * n_quads - 4, 4)

    @pl.when(jnp.logical_and(n_quads == 0, rem >= 2))
    def _():
        steps(0, 2, from_start=True)

    @pl.when(jnp.logical_and(n_quads == 0, rem == 1))
    def _():
        steps(0, 1, from_start=True)

    @pl.when(jnp.logical_and(n_quads >= 1, rem >= 2))
    def _():
        steps(4 * n_quads, 2)

    @pl.when(jnp.logical_and(i >= 3, rem % 2 == 1))
    def _():
        steps(i - 1, 1)

    last = jnp.where(i >= 2, i - 2, 0)
    nxt = jnp.minimum(i + 1, nq - 1)

    def finish():
        den = slice(V_HEAD_DIM, V_HEAD_DIM + 1)
        o = (acc_ref[0, :V_HEAD_DIM] / acc_ref[0, den]
             - lam_ref[0] * (acc_ref[1, :V_HEAD_DIM] / acc_ref[1, den]))
        ms = jnp.mean(o * o, axis=0, keepdims=True)
        o = o * lax.rsqrt(ms + RMS_EPS) * g_ref[...]
        o_ref[0] = o.T.astype(o_ref.dtype)

    def last_step(j, buf, first=False):
        qs_next = masked_queries(qn_ref)
        consume(j, buf, first=first)
        finish()
        produce(nxt, bias_ref[0], diag_buf, qs_next, diag=True)

    @pl.when(i == 0)
    def _():
        last_step(0, diag_buf, first=True)

    for parity in range(2):
        @pl.when(jnp.logical_and(i >= 1, i % 2 == parity))
        def _():
            last_step(last, bufs[parity])


def _attention(q4, k4, vt5, bias_vecs, lam, gfull, tq, tk, cast_weights=()):
    batch, _, seq, _ = q4.shape
    nq = seq // tq
    nk = seq // tk
    n_steps = batch * N_HEADS * nq
    cast_specs = []
    for w in cast_weights:
        rows, cols = w.shape
        rb = BF16_SUBLANES * pl.cdiv(rows, BF16_SUBLANES * n_steps)
        assert rows % rb == 0
        last = rows // rb - 1
        cast_specs.append(pl.BlockSpec(
            (rb, cols), lambda b, h, i, last=last: (jnp.minimum((b * N_HEADS + h) * nq + i, last), 0)))
    assert tq == tk and (tq // 2) % ATTN_QCHAIN == 0
    kern = functools.partial(_attn_kernel, tq=tq, tk=tk, n_cast=len(cast_weights))
    outs = pl.pallas_call(
        kern,
        grid=(batch, N_HEADS, nq),
        in_specs=[
            pl.BlockSpec(memory_space=pltpu.SMEM),
            pl.BlockSpec((1, 1, tq, LANES), lambda b, h, i: (b, h, i, 0)),
            pl.BlockSpec((1, 1, tq, LANES), lambda b, h, i: (b, h, jnp.minimum(i + 1, nq - 1), 0)),
            pl.BlockSpec((1, 1, seq, LANES), lambda b, h, i: (b, h, 0, 0)),
            pl.BlockSpec((1, 1, nk, V_ROWS, tk), lambda b, h, i: (b, h, 0, 0, 0)),
            pl.BlockSpec((1, 2, 1, 2 * tq), lambda b, h, i: (h, 0, 0, 0)),
            pl.BlockSpec((V_HEAD_DIM, tq), lambda b, h, i: (0, 0)),
        ] + cast_specs,
        out_specs=[pl.BlockSpec((1, tq, V_HEAD_DIM), lambda b, h, i: (b, i, h))] + cast_specs,
        out_shape=[jax.ShapeDtypeStruct((batch, seq, N_HEADS * V_HEAD_DIM), BF16)]
        + [jax.ShapeDtypeStruct(w.shape, BF16) for w in cast_weights],
        scratch_shapes=[
            pltpu.VMEM((2, tk, tq), F32),
            pltpu.VMEM((2, tk, tq), F32),
            pltpu.VMEM((2, tk, tq), F32),
            pltpu.VMEM((2, tk, tq), F32),
            pltpu.VMEM((2, 1, tq), F32),
            pltpu.VMEM((2, 1, tq), F32),
            pltpu.VMEM((2, 1, tq), F32),
            pltpu.VMEM((2, V_ROWS, tq), F32),
            pltpu.VMEM((2, 1, tq), F32),
        ],
        compiler_params=_cparams(("parallel", "parallel", "arbitrary")),
        name="diff_attention",
    )(lam, q4, q4, k4, vt5, bias_vecs, gfull, *cast_weights)
    return outs[0], tuple(outs[1:])


def _mixer_out_kernel(ch_ref, cb_ref, cc_ref, ga_ref, gc_ref, chp_ref, ccp_ref, a_ref, cw_ref,
                      woa_ref, woc_ref, wo_ref, x_ref, o_ref, *, tm, seq):
    i = pl.program_id(0)
    u = cc_ref[...].astype(F32) * ch_ref[...].astype(F32)
    up = ccp_ref[...].astype(F32) * chp_ref[...].astype(F32)
    up = jnp.where((i * tm) % seq == 0, 0.0, up)
    row = lax.broadcasted_iota(jnp.int32, u.shape, 0)
    u1 = jnp.where(row == 0, up[7:8], pltpu.roll(u, 1, 0))
    u2 = jnp.where(row == 0, up[6:7], jnp.where(row == 1, up[7:8], pltpu.roll(u, 2, 0)))
    cw = cw_ref[...]
    y = cw[0:1] * u2 + cw[1:2] * u1 + cw[2:3] * u
    c = (cb_ref[...].astype(F32) * y).astype(BF16)
    pa = jnp.dot(a_ref[...], woa_ref[...], preferred_element_type=F32)
    pc = jnp.dot(c, woc_ref[...], preferred_element_type=F32)
    merged = jax.nn.sigmoid(ga_ref[...].astype(F32)) * pa + jax.nn.sigmoid(gc_ref[...].astype(F32)) * pc
    o_ref[...] = x_ref[...] + jnp.dot(merged.astype(BF16), wo_ref[...], preferred_element_type=F32)


def _mixer_out(z, a, conv_w, woa, woc, wo, x, cols, tm, seq):
    t, d = x.shape
    w = conv_w.shape[1]
    c_h, c_b, c_c, c_ga, c_gc = cols
    blk = lambda col: pl.BlockSpec((tm, w), lambda i: (i, col))
    prev = lambda col: pl.BlockSpec((8, w), lambda i: (jnp.maximum(i * (tm // 8) - 1, 0), col))
    full = lambda arr: pl.BlockSpec(arr.shape, lambda i: (0, 0))
    kern = functools.partial(_mixer_out_kernel, tm=tm, seq=seq)
    return pl.pallas_call(
        kern,
        grid=(t // tm,),
        in_specs=[blk(c_h), blk(c_b), blk(c_c), blk(c_ga), blk(c_gc), prev(c_h), prev(c_c),
                  pl.BlockSpec((tm, a.shape[1]), lambda i: (i, 0)),
                  full(conv_w), full(woa), full(woc), full(wo),
                  pl.BlockSpec((tm, d), lambda i: (i, 0))],
        out_specs=pl.BlockSpec((tm, d), lambda i: (i, 0)),
        out_shape=jax.ShapeDtypeStruct((t, d), F32),
        compiler_params=_cparams(("parallel",)),
        name="mixer_out",
    )(z, z, z, z, z, z, z, a, conv_w, woa, woc, wo, x)


def _dense_ffn_kernel(x_ref, g_ref, wg_ref, wu_ref, wd_ref, o_ref):
    x = x_ref[...]
    h = _rms(x, g_ref[...]).astype(BF16)
    gate = jnp.dot(h, wg_ref[...], preferred_element_type=F32)
    up = jnp.dot(h, wu_ref[...], preferred_element_type=F32)
    act = (gate * jax.nn.sigmoid(gate) * up).astype(BF16)
    o_ref[...] = x + jnp.dot(act, wd_ref[...], preferred_element_type=F32)


def _dense_ffn(x, g, wg, wu, wd, tm):
    t, d = x.shape
    resident = lambda arr: pl.BlockSpec(arr.shape, lambda i: (0, 0), pipeline_mode=pl.Buffered(1))
    return pl.pallas_call(
        _dense_ffn_kernel,
        grid=(t // tm,),
        in_specs=[pl.BlockSpec((tm, d), lambda i: (i, 0)),
                  pl.BlockSpec((1, d), lambda i: (0, 0)),
                  resident(wg), resident(wu), resident(wd)],
        out_specs=pl.BlockSpec((tm, d), lambda i: (i, 0)),
        out_shape=jax.ShapeDtypeStruct((t, d), F32),
        compiler_params=_cparams(("parallel",)),
        name="dense_ffn",
    )(x, g.reshape(1, d), wg, wu, wd)


def _split_bf16(v):
    hi = v.astype(BF16)
    lo = (v - hi.astype(F32)).astype(BF16)
    return hi, lo


def _router_kernel(x_ref, g_ref, rw_ref, h_ref, r_ref):
    h = _rms(x_ref[...], g_ref[...])
    h_ref[...] = h.astype(BF16)
    h_hi, h_lo = _split_bf16(h)
    w_hi, w_lo = _split_bf16(rw_ref[...])
    logits = (jnp.dot(h_hi, w_hi, preferred_element_type=F32)
              + jnp.dot(h_hi, w_lo, preferred_element_type=F32)
              + jnp.dot(h_lo, w_hi, preferred_element_type=F32))
    lane = lax.broadcasted_iota(jnp.int32, logits.shape, 1)
    logits = jnp.where(lane < N_EXPERTS, logits, -jnp.inf)
    v1 = jnp.max(logits, axis=-1, keepdims=True)
    i1 = jnp.min(jnp.where(logits == v1, lane, LANES), axis=-1, keepdims=True)
    rest = jnp.where(lane == i1, -jnp.inf, logits)
    v2 = jnp.max(rest, axis=-1, keepdims=True)
    i2 = jnp.min(jnp.where(rest == v2, lane, LANES), axis=-1, keepdims=True)
    e2 = jnp.exp(v2 - v1)
    g1 = 1.0 / (1.0 + e2)
    g2 = e2 / (1.0 + e2)
    out = jnp.where(lane == 0, i1.astype(F32),
                    jnp.where(lane == 1, i2.astype(F32),
                              jnp.where(lane == 2, g1, jnp.where(lane == 3, g2, 0.0))))
    r_ref[...] = out


def _router(x, g, rw_pad, tm):
    t, d = x.shape
    return pl.pallas_call(
        _router_kernel,
        grid=(t // tm,),
        in_specs=[pl.BlockSpec((tm, d), lambda i: (i, 0)),
                  pl.BlockSpec((1, d), lambda i: (0, 0)),
                  pl.BlockSpec(rw_pad.shape, lambda i: (0, 0))],
        out_specs=[pl.BlockSpec((tm, d), lambda i: (i, 0)),
                   pl.BlockSpec((tm, LANES), lambda i: (i, 0))],
        out_shape=[jax.ShapeDtypeStruct((t, d), BF16),
                   jax.ShapeDtypeStruct((t, LANES), F32)],
        compiler_params=_cparams(("parallel",)),
        name="moe_router",
    )(x, g.reshape(1, d), rw_pad)


def _gmm_kernel(te_ref, ta_ref, xs_ref, wg_ref, wu_ref, wd_ref, *rest, nch, tile0):
    o_ref, acc_ref = rest[-2:]
    i = tile0 + pl.program_id(0)
    c = pl.program_id(1)
    active = ta_ref[i] != 0

    @pl.when(active)
    def _():
        x = xs_ref[...]
        gate = jnp.dot(x, wg_ref[0], preferred_element_type=F32)
        up = jnp.dot(x, wu_ref[0], preferred_element_type=F32)
        act = (gate * jax.nn.sigmoid(gate) * up).astype(BF16)
        contrib = jnp.dot(act, wd_ref[0], preferred_element_type=F32)

        @pl.when(c == 0)
        def _():
            acc_ref[...] = contrib

        @pl.when(c != 0)
        def _():
            acc_ref[...] += contrib

        @pl.when(c == nch - 1)
        def _():
            o_ref[...] = acc_ref[...].astype(o_ref.dtype)

    @pl.when(jnp.logical_and(jnp.logical_not(active), c == nch - 1))
    def _():
        o_ref[...] = jnp.zeros(o_ref.shape, o_ref.dtype)


def _grouped_ffn(tile_expert, tile_active, xs_part, w_gu, w_down, ys_prev, tile0, n_tiles, tm, tf):
    p_part, d = xs_part.shape
    ff = w_down.shape[1]
    nch = ff // tf
    kern = functools.partial(_gmm_kernel, nch=nch, tile0=tile0)

    def chunk(c, ta, i):
        return jnp.where(ta[tile0 + i] != 0, c, nch - 1)

    in_specs = [
        pl.BlockSpec((tm, d), lambda i, c, te, ta: (i, 0)),
        pl.BlockSpec((1, d, tf), lambda i, c, te, ta: (te[tile0 + i], 0, chunk(c, ta, i))),
        pl.BlockSpec((1, d, tf), lambda i, c, te, ta: (te[tile0 + i], 0, nch + chunk(c, ta, i))),
        pl.BlockSpec((1, tf, d), lambda i, c, te, ta: (te[tile0 + i], chunk(c, ta, i), 0)),
    ]
    in_specs.append(pl.BlockSpec(memory_space=pl.ANY))
    operands = [tile_expert, tile_active, xs_part, w_gu, w_gu, w_down, ys_prev]
    aliases = {len(operands) - 1: 0}
    grid_spec = pltpu.PrefetchScalarGridSpec(
        num_scalar_prefetch=2,
        grid=(p_part // tm, nch),
        in_specs=in_specs,
        out_specs=pl.BlockSpec((tm, d), lambda i, c, te, ta: (tile0 + i, 0)),
        scratch_shapes=[pltpu.VMEM((tm, d), F32)],
    )
    return pl.pallas_call(
        kern,
        grid_spec=grid_spec,
        out_shape=jax.ShapeDtypeStruct((n_tiles * tm, d), BF16),
        input_output_aliases=aliases,
        compiler_params=_cparams(("arbitrary", "arbitrary")),
        name="moe_grouped_ffn",
    )(*operands)


def _combine_kernel(x_ref, r_ref, y0_ref, y1_ref, g_ref, o_ref, *, final_norm):
    route = r_ref[...]
    x = (x_ref[...] + route[:, 2:3] * y0_ref[...].astype(F32)
         + route[:, 3:4] * y1_ref[...].astype(F32))
    o_ref[...] = _rms(x, g_ref[...]) if final_norm else x


def _combine(x, route, ysel, g, tm, final_norm):
    t, d = x.shape
    nt = t // tm
    kern = functools.partial(_combine_kernel, final_norm=final_norm)
    return pl.pallas_call(
        kern,
        grid=(nt,),
        in_specs=[pl.BlockSpec((tm, d), lambda i: (i, 0)),
                  pl.BlockSpec((tm, LANES), lambda i: (i, 0)),
                  pl.BlockSpec((tm, d), lambda i: (i, 0)),
                  pl.BlockSpec((tm, d), lambda i: (nt + i, 0)),
                  pl.BlockSpec((1, d), lambda i: (0, 0))],
        out_specs=pl.BlockSpec((tm, d), lambda i: (i, 0)),
        out_shape=jax.ShapeDtypeStruct((t, d), F32),
        compiler_params=_cparams(("parallel",)),
        name="moe_combine",
    )(x, route, ysel, ysel, g.reshape(1, d))


def _final_norm_kernel(x_ref, g_ref, o_ref):
    o_ref[...] = _rms(x_ref[...], g_ref[...])


def _final_norm(x, g, tm):
    t, d = x.shape
    return pl.pallas_call(
        _final_norm_kernel,
        grid=(t // tm,),
        in_specs=[pl.BlockSpec((tm, d), lambda i: (i, 0)), pl.BlockSpec((1, d), lambda i: (0, 0))],
        out_specs=pl.BlockSpec((tm, d), lambda i: (i, 0)),
        out_shape=jax.ShapeDtypeStruct((t, d), F32),
        compiler_params=_cparams(("parallel",)),
        name="final_norm",
    )(x, g.reshape(1, d))


def _t5_causal_bucket(dist):
    max_exact = NUM_BUCKETS // 2
    is_small = dist < max_exact
    d = jnp.maximum(dist, 1).astype(F32)
    large = max_exact + (jnp.log(d / max_exact) / math.log(MAX_DISTANCE / max_exact)
                         * (NUM_BUCKETS - max_exact)).astype(jnp.int32)
    large = jnp.minimum(large, NUM_BUCKETS - 1)
    return jnp.where(is_small, dist, large)


def _bias_vectors(rel_bias, seq, tq, tk):
    assert tq == tk and tk + 1 >= MAX_DISTANCE
    rb = rel_bias.astype(F32)
    bias_dist = rb[_t5_causal_bucket(jnp.arange(seq, dtype=jnp.int32))].T
    rel = (bias_dist - rb[NUM_BUCKETS - 1][:, None]) * LOG2E
    d_diag = jnp.arange(2 * tq, dtype=jnp.int32) - (tk - 1)
    d_sub = jnp.minimum(d_diag + tk, seq - 1)
    v_diag = jnp.where(d_diag >= 0, rel[:, jnp.clip(d_diag, 0, seq - 1)], NEG_BIG)
    v_sub = rel[:, d_sub]
    return jnp.stack([v_diag, v_sub], axis=1)[:, :, None, :]


def _moe_dispatch(route, tm):
    t = route.shape[0]
    expert = route[:, :2].astype(jnp.int32).reshape(-1)
    onehot = (expert[:, None] == jnp.arange(N_EXPERTS, dtype=jnp.int32)[None, :]).astype(jnp.int32)
    rank = jnp.take_along_axis(jnp.cumsum(onehot, axis=0), expert[:, None], axis=1)[:, 0] - 1
    counts = jnp.sum(onehot, axis=0)
    padded = ((counts + tm - 1) // tm) * tm
    ends = jnp.cumsum(padded)
    starts = ends - padded
    pos = starts[expert] + rank
    p = 2 * t + N_EXPERTS * tm
    row_token = jnp.zeros((p,), jnp.int32).at[pos].set(jnp.arange(2 * t, dtype=jnp.int32) // 2)
    tile_start = jnp.arange(p // tm, dtype=jnp.int32) * tm
    tile_active = (tile_start < ends[-1]).astype(jnp.int32)
    tile_expert = jnp.minimum(jnp.sum((tile_start[:, None] >= ends[None, :]).astype(jnp.int32), axis=1),
                              N_EXPERTS - 1)
    last_expert = tile_expert[jnp.maximum(ends[-1] // tm - 1, 0)]
    tile_expert = jnp.where(tile_active != 0, tile_expert, last_expert).astype(jnp.int32)
    pos_by_choice = pos.reshape(t, 2).T.reshape(-1)
    return pos_by_choice, row_token, tile_expert, tile_active


def kernel(x, rel_bias, mix_norm_g, w_in, lam_qk, subln_g, conv_w, w_o_attn, w_o_conv, w_o, ffn_norm_g,
           dense_w_gate_up, dense_w_down, router_w, expert_w_gate_up, expert_w_down, final_norm_g):
    batch, seq, d = x.shape
    t = batch * seq
    depth = w_in.shape[0]
    qk_w = N_HEADS * 2 * HEAD_DIM
    v_w = N_HEADS * V_HEAD_DIM
    tq = min(ATTN_TQ, seq)
    tk = tq
    tm = min(ROW_TILE, seq)

    xf = x.reshape(t, d)
    bias_vecs = _bias_vectors(rel_bias, seq, tq, tk)
    conv_cols = tuple(range(5))

    for l in range(depth):
        w3 = w_in[l].astype(BF16).reshape(d, -1, v_w).transpose(1, 0, 2)
        q4, k4, vt5, z = _in_proj(xf, mix_norm_g[l], w3, batch, seq, tk)

        lq = lam_qk[l].astype(F32)
        lam_init = 0.8 - 0.6 * math.exp(-0.3 * l)
        lam = (jnp.exp(jnp.sum(lq[0] * lq[1])) - jnp.exp(jnp.sum(lq[2] * lq[3])) + lam_init).reshape(1)
        gsub = subln_g[l].astype(F32) * (1.0 - lam_init)
        gfull = jnp.broadcast_to(gsub[:, None], (V_HEAD_DIM, tq))
        cast_weights = ()
        if l % 2 == 0 and l + 1 < depth:
            e = (l + 1) // 2
            cast_weights = (expert_w_gate_up[e].reshape(-1, expert_w_gate_up.shape[-1]),
                            expert_w_down[e].reshape(-1, expert_w_down.shape[-1]))
        a, cast_bf16 = _attention(q4, k4, vt5, bias_vecs, lam, gfull, tq, tk, cast_weights)
        if cast_weights:
            expert_bf16 = (cast_bf16[0].reshape(expert_w_gate_up.shape[1:]),
                           cast_bf16[1].reshape(expert_w_down.shape[1:]))

        xf = _mixer_out(z, a.reshape(t, v_w), conv_w[l].astype(F32), w_o_attn[l].astype(BF16),
                        w_o_conv[l].astype(BF16), w_o[l].astype(BF16), xf, conv_cols, tm, seq)

        last = l == depth - 1
        if l % 2 == 0:
            wgu = dense_w_gate_up[l // 2]
            ff = wgu.shape[1] // 2
            xf = _dense_ffn(xf, ffn_norm_g[l], wgu[:, :ff].astype(BF16), wgu[:, ff:].astype(BF16),
                            dense_w_down[l // 2].astype(BF16), tm)
            if last:
                xf = _final_norm(xf, final_norm_g, tm)
        else:
            mt = min(MOE_TILE, seq)
            rw_pad = jnp.zeros((d, LANES), F32).at[:, :N_EXPERTS].set(router_w[l // 2].astype(F32))
            hb, route = _router(xf, ffn_norm_g[l], rw_pad, tm)
            pos_by_choice, row_token, tile_expert, tile_active = _moe_dispatch(route, mt)
            ff = expert_w_down.shape[2]
            n_tiles = row_token.shape[0] // mt
            part_sizes = ([n_tiles // 8 * e for e in MOE_PART_EIGHTHS] if n_tiles % 8 == 0 else [n_tiles])
            ys = jnp.zeros((n_tiles * mt, d), BF16)
            tile0 = 0
            order = jnp.zeros((), jnp.int32)
            for part_tiles in part_sizes:
                rows = slice(tile0 * mt, (tile0 + part_tiles) * mt)
                xs_part = hb[row_token[rows] + order]
                order = (xs_part[0, 0] != xs_part[0, 0]).astype(jnp.int32)
                ys = _grouped_ffn(tile_expert, tile_active, xs_part, expert_bf16[0], expert_bf16[1],
                                  ys, tile0, n_tiles, mt, ff // 2)
                tile0 += part_tiles
            xf = _combine(xf, route, ys[pos_by_choice], final_norm_g, tm, final_norm=last)

    return xf.reshape(batch, seq, d)
```

```python
import functools
import math

import jax
import jax.numpy as jnp
from jax import lax
from jax.experimental import pallas as pl
from jax.experimental.pallas import tpu as pltpu

F32 = jnp.float32
BF16 = jnp.bfloat16

N_HEADS = 8
HEAD_DIM = 64
V_HEAD_DIM = 2 * HEAD_DIM
NUM_BUCKETS = 32
MAX_DISTANCE = 128
N_EXPERTS = 8
RMS_EPS = 1e-6

LANES = 128
BF16_SUBLANES = 16
VMEM_LIMIT_BYTES = 56 * 1024 * 1024

V_ROWS = V_HEAD_DIM + BF16_SUBLANES

LOG2E = math.log2(math.e)
NEG_BIG = -1e30

ATTN_TQ = 512
ATTN_QCHAIN = 256
ROW_TILE = 512
MOE_TILE = 512
MOE_PART_EIGHTHS = (1, 2, 5)


def _cparams(sem):
    return pltpu.CompilerParams(dimension_semantics=sem, vmem_limit_bytes=VMEM_LIMIT_BYTES)


def _rms(x, g):
    ms = jnp.mean(x * x, axis=-1, keepdims=True)
    return x * lax.rsqrt(ms + RMS_EPS) * g


def _in_proj_kernel(x_ref, g_ref, w_ref, q_ref, k_ref, vt_ref, z_ref):
    h = _rms(x_ref[...], g_ref[...]).astype(BF16)

    def proj(blk):
        return jnp.dot(h, w_ref[blk], preferred_element_type=F32)

    for dst, blk in ((q_ref, 0), (k_ref, 1)):
        y = proj(blk).astype(dst.dtype)
        for hh in range(N_HEADS):
            dst[0, hh] = y[:, hh * LANES:(hh + 1) * LANES]

    yt = proj(2).astype(vt_ref.dtype).T
    pad_rows = V_ROWS - V_HEAD_DIM
    row = lax.broadcasted_iota(jnp.int32, (pad_rows, yt.shape[1]), 0)
    ones_rows = jnp.where(row == 0, 1.0, 0.0).astype(vt_ref.dtype)
    for hh in range(N_HEADS):
        vt_ref[0, hh, 0, :V_HEAD_DIM, :] = yt[hh * V_HEAD_DIM:(hh + 1) * V_HEAD_DIM, :]
        vt_ref[0, hh, 0, V_HEAD_DIM:, :] = ones_rows

    width = w_ref.shape[2]
    for c in range(z_ref.shape[1] // width):
        z_ref[:, c * width:(c + 1) * width] = proj(3 + c).astype(z_ref.dtype)


def _in_proj(x, g, w3, batch, seq, tm):
    t, d = x.shape
    nblk, _, width = w3.shape
    assert width == N_HEADS * 2 * HEAD_DIM == N_HEADS * V_HEAD_DIM
    nt = seq // tm
    n_rest = (nblk - 3) * width
    head_spec = pl.BlockSpec((1, N_HEADS, tm, LANES), lambda i: (i // nt, 0, i % nt, 0))
    return pl.pallas_call(
        _in_proj_kernel,
        grid=(t // tm,),
        in_specs=[
            pl.BlockSpec((tm, d), lambda i: (i, 0)),
            pl.BlockSpec((1, d), lambda i: (0, 0)),
            pl.BlockSpec(w3.shape, lambda i: (0, 0, 0), pipeline_mode=pl.Buffered(1)),
        ],
        out_specs=[
            head_spec, head_spec,
            pl.BlockSpec((1, N_HEADS, 1, V_ROWS, tm), lambda i: (i // nt, 0, i % nt, 0, 0)),
            pl.BlockSpec((tm, n_rest), lambda i: (i, 0)),
        ],
        out_shape=[jax.ShapeDtypeStruct((batch, N_HEADS, seq, LANES), BF16),
                   jax.ShapeDtypeStruct((batch, N_HEADS, seq, LANES), BF16),
                   jax.ShapeDtypeStruct((batch, N_HEADS, nt, V_ROWS, tm), BF16),
                   jax.ShapeDtypeStruct((t, n_rest), BF16)],
        compiler_params=_cparams(("parallel",)),
        name="in_proj",
    )(x, g.reshape(1, d), w3)


def _attn_kernel(lam_ref, q_ref, qn_ref, k_ref, vt_ref, ub_ref, g_ref, *refs, tq, tk, n_cast):
    cast_in = refs[:n_cast]
    o_ref = refs[n_cast]
    cast_out = refs[n_cast + 1:2 * n_cast + 1]
    bias_ref, s0_ref, s1_ref, sd_ref, mc0_ref, mc1_ref, mcd_ref, acc_ref, m_ref = refs[2 * n_cast + 1:]
    bufs = ((s0_ref, mc0_ref), (s1_ref, mc1_ref))
    diag_buf = (sd_ref, mcd_ref)
    i = pl.program_id(2)
    nq = pl.num_programs(2)

    for w_ref, wb_ref in zip(cast_in, cast_out):
        wb_ref[...] = w_ref[...].astype(wb_ref.dtype)

    def masked_queries(ref):
        q = ref[0, 0].astype(F32) * (LOG2E * HEAD_DIM ** -0.5)
        lane = lax.broadcasted_iota(jnp.int32, q.shape, 1)
        return (jnp.where(lane < HEAD_DIM, q, 0.0).astype(BF16),
                jnp.where(lane >= HEAD_DIM, q, 0.0).astype(BF16))

    hk, hq = tk // 2, tq // 2
    nt = (((1,), (1,)), ((), ()))

    def produce(j, bias, buf, queries, diag=False):
        s_ref, mc_ref = buf
        row0 = pl.multiple_of(j * tk, tk)
        k = k_ref[0, 0, pl.ds(row0, tk), :]
        for mp in range(2):
            if diag:
                top = lax.dot_general(k[:hk], queries[mp], nt, preferred_element_type=F32) + bias[:hk]
                low = lax.dot_general(k[hk:], queries[mp][hq:], nt, preferred_element_type=F32) + bias[hk:, hq:]
                s_ref[mp, :hk, :tq] = top
                s_ref[mp, hk:, hq:tq] = low
                mc_ref[mp, :, :hq] = jnp.max(top[:, :hq], axis=0, keepdims=True)
                mc_ref[mp, :, hq:] = jnp.maximum(jnp.max(top[:, hq:], axis=0, keepdims=True),
                                                 jnp.max(low, axis=0, keepdims=True))
            else:
                s = lax.dot_general(k, queries[mp], nt, preferred_element_type=F32)
                if bias is not None:
                    s = s + bias
                s_ref[mp, :, :tq] = s
                mc_ref[mp] = jnp.max(s, axis=0, keepdims=True)

    def consume(j, buf, first=False):
        s_ref, mc_ref = buf
        vt = vt_ref[0, 0, j]
        for mp in range(2):
            for h in range(tq // ATTN_QCHAIN):
                cols = slice(h * ATTN_QCHAIN, (h + 1) * ATTN_QCHAIN)
                keys = slice(0, hk) if first and (h + 1) * ATTN_QCHAIN <= hq else slice(0, tk)
                if first:
                    m_new = mc_ref[mp, :, cols]
                else:
                    m_old = m_ref[mp, :, cols]
                    m_new = jnp.maximum(m_old, mc_ref[mp, :, cols])
                p = jnp.exp2(s_ref[mp, keys, cols] - m_new)
                pv = jnp.dot(vt[:, keys], p.astype(BF16), preferred_element_type=F32)
                if first:
                    acc_ref[mp, :, cols] = pv
                else:
                    acc_ref[mp, :, cols] = jnp.exp2(m_old - m_new) * acc_ref[mp, :, cols] + pv
                m_ref[mp, :, cols] = m_new

    qs = masked_queries(q_ref)

    @pl.when(i == 0)
    def _():
        for t in range(2):
            u = jnp.broadcast_to(ub_ref[0, t], (tk, 2 * tq))
            bias_ref[t] = pltpu.roll(u, 1, 1, stride=1, stride_axis=0)[:, tq:]
        produce(i, bias_ref[0], diag_buf, qs, diag=True)

    def steps(k0, count, from_start=False):
        for t in range(count):
            if from_start and t == 0:
                produce(i - 1, bias_ref[1], bufs[1], qs)
                consume(i, diag_buf, first=True)
                continue
            par = (t + 1) % 2
            f = k0 + t - 1
            prev = (i - 1 if t == 1 else f - 1) if from_start else f - 1
            produce(f, None, bufs[par], qs)
            consume(prev, bufs[1 - par])

    n_quads = i // 4
    rem = i - 4 * n_quads

    @pl.when(n_quads >= 1)
    def _():
        steps(0, 4, from_start=True)

    n_octets = (n_quads - 1) // 2

    def octet(g, carry):
        steps(4 + 8 * g, 8)
        return carry

    lax.fori_loop(0, n_octets, octet, 0)

    @pl.when(jnp.logical_and(n_quads >= 2, n_quads % 2 == 0))
    def _():
        steps(4 * n_quads - 4, 4)

    @pl.when(jnp.logical_and(n_quads == 0, rem >= 2))
    def _():
        steps(0, 2, from_start=True)

    @pl.when(jnp.logical_and(n_quads == 0, rem == 1))
    def _():
        steps(0, 1, from_start=True)

    @pl.when(jnp.logical_and(n_quads >= 1, rem >= 2))
    def _():
        steps(4 * n_quads, 2)

    @pl.when(jnp.logical_and(i >= 3, rem % 2 == 1))
    def _():
        steps(i - 1, 1)

    last = jnp.where(i >= 2, i - 2, 0)
    nxt = jnp.minimum(i + 1, nq - 1)

    def finish():
        den = slice(V_HEAD_DIM, V_HEAD_DIM + 1)
        o = (acc_ref[0, :V_HEAD_DIM] / acc_ref[0, den]
             - lam_ref[0] * (acc_ref[1, :V_HEAD_DIM] / acc_ref[1, den]))
        ms = jnp.mean(o * o, axis=0, keepdims=True)
        o = o * lax.rsqrt(ms + RMS_EPS) * g_ref[...]
        o_ref[0] = o.T.astype(o_ref.dtype)

    def last_step(j, buf, first=False):
        qs_next = masked_queries(qn_ref)
        consume(j, buf, first=first)
        finish()
        produce(nxt, bias_ref[0], diag_buf, qs_next, diag=True)

    @pl.when(i == 0)
    def _():
        last_step(0, diag_buf, first=True)

    for parity in range(2):
        @pl.when(jnp.logical_and(i >= 1, i % 2 == parity))
        def _():
            last_step(last, bufs[parity])


def _attention(q4, k4, vt5, bias_vecs, lam, gfull, tq, tk, cast_weights=()):
    batch, _, seq, _ = q4.shape
    nq = seq // tq
    nk = seq // tk
    n_steps = batch * N_HEADS * nq
    cast_specs = []
    for w in cast_weights:
        rows, cols = w.shape
        rb = BF16_SUBLANES * pl.cdiv(rows, BF16_SUBLANES * n_steps)
        assert rows % rb == 0
        last = rows // rb - 1
        cast_specs.append(pl.BlockSpec(
            (rb, cols), lambda b, h, i, last=last: (jnp.minimum((b * N_HEADS + h) * nq + i, last), 0)))
    assert tq == tk and (tq // 2) % ATTN_QCHAIN == 0
    kern = functools.partial(_attn_kernel, tq=tq, tk=tk, n_cast=len(cast_weights))
    outs = pl.pallas_call(
        kern,
        grid=(batch, N_HEADS, nq),
        in_specs=[
            pl.BlockSpec(memory_space=pltpu.SMEM),
            pl.BlockSpec((1, 1, tq, LANES), lambda b, h, i: (b, h, i, 0)),
            pl.BlockSpec((1, 1, tq, LANES), lambda b, h, i: (b, h, jnp.minimum(i + 1, nq - 1), 0)),
            pl.BlockSpec((1, 1, seq, LANES), lambda b, h, i: (b, h, 0, 0)),
            pl.BlockSpec((1, 1, nk, V_ROWS, tk), lambda b, h, i: (b, h, 0, 0, 0)),
            pl.BlockSpec((1, 2, 1, 2 * tq), lambda b, h, i: (h, 0, 0, 0)),
            pl.BlockSpec((V_HEAD_DIM, tq), lambda b, h, i: (0, 0)),
        ] + cast_specs,
        out_specs=[pl.BlockSpec((1, tq, V_HEAD_DIM), lambda b, h, i: (b, i, h))] + cast_specs,
        out_shape=[jax.ShapeDtypeStruct((batch, seq, N_HEADS * V_HEAD_DIM), BF16)]
        + [jax.ShapeDtypeStruct(w.shape, BF16) for w in cast_weights],
        scratch_shapes=[
            pltpu.VMEM((2, tk, tq), F32),
            pltpu.VMEM((2, tk, tq + LANES), F32),
            pltpu.VMEM((2, tk, tq + LANES), F32),
            pltpu.VMEM((2, tk, tq + LANES), F32),
            pltpu.VMEM((2, 1, tq), F32),
            pltpu.VMEM((2, 1, tq), F32),
            pltpu.VMEM((2, 1, tq), F32),
            pltpu.VMEM((2, V_ROWS, tq), F32),
            pltpu.VMEM((2, 1, tq), F32),
        ],
        compiler_params=_cparams(("parallel", "parallel", "arbitrary")),
        name="diff_attention",
    )(lam, q4, q4, k4, vt5, bias_vecs, gfull, *cast_weights)
    return outs[0], tuple(outs[1:])


def _mixer_out_kernel(ch_ref, cb_ref, cc_ref, ga_ref, gc_ref, chp_ref, ccp_ref, a_ref, cw_ref,
                      woa_ref, woc_ref, wo_ref, x_ref, o_ref, *, tm, seq):
    i = pl.program_id(0)
    u = cc_ref[...].astype(F32) * ch_ref[...].astype(F32)
    up = ccp_ref[...].astype(F32) * chp_ref[...].astype(F32)
    up = jnp.where((i * tm) % seq == 0, 0.0, up)
    row = lax.broadcasted_iota(jnp.int32, u.shape, 0)
    u1 = jnp.where(row == 0, up[7:8], pltpu.roll(u, 1, 0))
    u2 = jnp.where(row == 0, up[6:7], jnp.where(row == 1, up[7:8], pltpu.roll(u, 2, 0)))
    cw = cw_ref[...]
    y = cw[0:1] * u2 + cw[1:2] * u1 + cw[2:3] * u
    c = (cb_ref[...].astype(F32) * y).astype(BF16)
    pa = jnp.dot(a_ref[...], woa_ref[...], preferred_element_type=F32)
    pc = jnp.dot(c, woc_ref[...], preferred_element_type=F32)
    merged = jax.nn.sigmoid(ga_ref[...].astype(F32)) * pa + jax.nn.sigmoid(gc_ref[...].astype(F32)) * pc
    o_ref[...] = x_ref[...] + jnp.dot(merged.astype(BF16), wo_ref[...], preferred_element_type=F32)


def _mixer_out(z, a, conv_w, woa, woc, wo, x, cols, tm, seq):
    t, d = x.shape
    w = conv_w.shape[1]
    c_h, c_b, c_c, c_ga, c_gc = cols
    blk = lambda col: pl.BlockSpec((tm, w), lambda i: (i, col))
    prev = lambda col: pl.BlockSpec((8, w), lambda i: (jnp.maximum(i * (tm // 8) - 1, 0), col))
    full = lambda arr: pl.BlockSpec(arr.shape, lambda i: (0, 0))
    kern = functools.partial(_mixer_out_kernel, tm=tm, seq=seq)
    return pl.pallas_call(
        kern,
        grid=(t // tm,),
        in_specs=[blk(c_h), blk(c_b), blk(c_c), blk(c_ga), blk(c_gc), prev(c_h), prev(c_c),
                  pl.BlockSpec((tm, a.shape[1]), lambda i: (i, 0)),
                  full(conv_w), full(woa), full(woc), full(wo),
                  pl.BlockSpec((tm, d), lambda i: (i, 0))],
        out_specs=pl.BlockSpec((tm, d), lambda i: (i, 0)),
        out_shape=jax.ShapeDtypeStruct((t, d), F32),
        compiler_params=_cparams(("parallel",)),
        name="mixer_out",
    )(z, z, z, z, z, z, z, a, conv_w, woa, woc, wo, x)


def _dense_ffn_kernel(x_ref, g_ref, wg_ref, wu_ref, wd_ref, o_ref):
    x = x_ref[...]
    h = _rms(x, g_ref[...]).astype(BF16)
    gate = jnp.dot(h, wg_ref[...], preferred_element_type=F32)
    up = jnp.dot(h, wu_ref[...], preferred_element_type=F32)
    act = (gate * jax.nn.sigmoid(gate) * up).astype(BF16)
    o_ref[...] = x + jnp.dot(act, wd_ref[...], preferred_element_type=F32)


def _dense_ffn(x, g, wg, wu, wd, tm):
    t, d = x.shape
    resident = lambda arr: pl.BlockSpec(arr.shape, lambda i: (0, 0), pipeline_mode=pl.Buffered(1))
    return pl.pallas_call(
        _dense_ffn_kernel,
        grid=(t // tm,),
        in_specs=[pl.BlockSpec((tm, d), lambda i: (i, 0)),
                  pl.BlockSpec((1, d), lambda i: (0, 0)),
                  resident(wg), resident(wu), resident(wd)],
        out_specs=pl.BlockSpec((tm, d), lambda i: (i, 0)),
        out_shape=jax.ShapeDtypeStruct((t, d), F32),
        compiler_params=_cparams(("parallel",)),
        name="dense_ffn",
    )(x, g.reshape(1, d), wg, wu, wd)


def _split_bf16(v):
    hi = v.astype(BF16)
    lo = (v - hi.astype(F32)).astype(BF16)
    return hi, lo


def _router_kernel(x_ref, g_ref, rw_ref, h_ref, r_ref):
    h = _rms(x_ref[...], g_ref[...])
    h_ref[...] = h.astype(BF16)
    h_hi, h_lo = _split_bf16(h)
    w_hi, w_lo = _split_bf16(rw_ref[...])
    logits = (jnp.dot(h_hi, w_hi, preferred_element_type=F32)
              + jnp.dot(h_hi, w_lo, preferred_element_type=F32)
              + jnp.dot(h_lo, w_hi, preferred_element_type=F32))
    lane = lax.broadcasted_iota(jnp.int32, logits.shape, 1)
    logits = jnp.where(lane < N_EXPERTS, logits, -jnp.inf)
    v1 = jnp.max(logits, axis=-1, keepdims=True)
    i1 = jnp.min(jnp.where(logits == v1, lane, LANES), axis=-1, keepdims=True)
    rest = jnp.where(lane == i1, -jnp.inf, logits)
    v2 = jnp.max(rest, axis=-1, keepdims=True)
    i2 = jnp.min(jnp.where(rest == v2, lane, LANES), axis=-1, keepdims=True)
    e2 = jnp.exp(v2 - v1)
    g1 = 1.0 / (1.0 + e2)
    g2 = e2 / (1.0 + e2)
    out = jnp.where(lane == 0, i1.astype(F32),
                    jnp.where(lane == 1, i2.astype(F32),
                              jnp.where(lane == 2, g1, jnp.where(lane == 3, g2, 0.0))))
    r_ref[...] = out


def _router(x, g, rw_pad, tm):
    t, d = x.shape
    return pl.pallas_call(
        _router_kernel,
        grid=(t // tm,),
        in_specs=[pl.BlockSpec((tm, d), lambda i: (i, 0)),
                  pl.BlockSpec((1, d), lambda i: (0, 0)),
                  pl.BlockSpec(rw_pad.shape, lambda i: (0, 0))],
        out_specs=[pl.BlockSpec((tm, d), lambda i: (i, 0)),
                   pl.BlockSpec((tm, LANES), lambda i: (i, 0))],
        out_shape=[jax.ShapeDtypeStruct((t, d), BF16),
                   jax.ShapeDtypeStruct((t, LANES), F32)],
        compiler_params=_cparams(("parallel",)),
        name="moe_router",
    )(x, g.reshape(1, d), rw_pad)


def _gmm_kernel(te_ref, ta_ref, xs_ref, wg_ref, wu_ref, wd_ref, *rest, nch, tile0):
    o_ref, acc_ref = rest[-2:]
    i = tile0 + pl.program_id(0)
    c = pl.program_id(1)
    active = ta_ref[i] != 0

    @pl.when(active)
    def _():
        x = xs_ref[...]
        gate = jnp.dot(x, wg_ref[0], preferred_element_type=F32)
        up = jnp.dot(x, wu_ref[0], preferred_element_type=F32)
        act = (gate * jax.nn.sigmoid(gate) * up).astype(BF16)
        contrib = jnp.dot(act, wd_ref[0], preferred_element_type=F32)

        @pl.when(c == 0)
        def _():
            acc_ref[...] = contrib

        @pl.when(c != 0)
        def _():
            acc_ref[...] += contrib

        @pl.when(c == nch - 1)
        def _():
            o_ref[...] = acc_ref[...].astype(o_ref.dtype)

    @pl.when(jnp.logical_and(jnp.logical_not(active), c == nch - 1))
    def _():
        o_ref[...] = jnp.zeros(o_ref.shape, o_ref.dtype)


def _grouped_ffn(tile_expert, tile_active, xs_part, w_gu, w_down, ys_prev, tile0, n_tiles, tm, tf):
    p_part, d = xs_part.shape
    ff = w_down.shape[1]
    nch = ff // tf
    kern = functools.partial(_gmm_kernel, nch=nch, tile0=tile0)

    def chunk(c, ta, i):
        return jnp.where(ta[tile0 + i] != 0, c, nch - 1)

    in_specs = [
        pl.BlockSpec((tm, d), lambda i, c, te, ta: (i, 0)),
        pl.BlockSpec((1, d, tf), lambda i, c, te, ta: (te[tile0 + i], 0, chunk(c, ta, i))),
        pl.BlockSpec((1, d, tf), lambda i, c, te, ta: (te[tile0 + i], 0, nch + chunk(c, ta, i))),
        pl.BlockSpec((1, tf, d), lambda i, c, te, ta: (te[tile0 + i], chunk(c, ta, i), 0)),
    ]
    in_specs.append(pl.BlockSpec(memory_space=pl.ANY))
    operands = [tile_expert, tile_active, xs_part, w_gu, w_gu, w_down, ys_prev]
    aliases = {len(operands) - 1: 0}
    grid_spec = pltpu.PrefetchScalarGridSpec(
        num_scalar_prefetch=2,
        grid=(p_part // tm, nch),
        in_specs=in_specs,
        out_specs=pl.BlockSpec((tm, d), lambda i, c, te, ta: (tile0 + i, 0)),
        scratch_shapes=[pltpu.VMEM((tm, d), F32)],
    )
    return pl.pallas_call(
        kern,
        grid_spec=grid_spec,
        out_shape=jax.ShapeDtypeStruct((n_tiles * tm, d), BF16),
        input_output_aliases=aliases,
        compiler_params=_cparams(("arbitrary", "arbitrary")),
        name="moe_grouped_ffn",
    )(*operands)


def _combine_kernel(x_ref, r_ref, y0_ref, y1_ref, g_ref, o_ref, *, final_norm):
    route = r_ref[...]
    x = (x_ref[...] + route[:, 2:3] * y0_ref[...].astype(F32)
         + route[:, 3:4] * y1_ref[...].astype(F32))
    o_ref[...] = _rms(x, g_ref[...]) if final_norm else x


def _combine(x, route, ysel, g, tm, final_norm):
    t, d = x.shape
    nt = t // tm
    kern = functools.partial(_combine_kernel, final_norm=final_norm)
    return pl.pallas_call(
        kern,
        grid=(nt,),
        in_specs=[pl.BlockSpec((tm, d), lambda i: (i, 0)),
                  pl.BlockSpec((tm, LANES), lambda i: (i, 0)),
                  pl.BlockSpec((tm, d), lambda i: (i, 0)),
                  pl.BlockSpec((tm, d), lambda i: (nt + i, 0)),
                  pl.BlockSpec((1, d), lambda i: (0, 0))],
        out_specs=pl.BlockSpec((tm, d), lambda i: (i, 0)),
        out_shape=jax.ShapeDtypeStruct((t, d), F32),
        compiler_params=_cparams(("parallel",)),
        name="moe_combine",
    )(x, route, ysel, ysel, g.reshape(1, d))


def _final_norm_kernel(x_ref, g_ref, o_ref):
    o_ref[...] = _rms(x_ref[...], g_ref[...])


def _final_norm(x, g, tm):
    t, d = x.shape
    return pl.pallas_call(
        _final_norm_kernel,
        grid=(t // tm,),
        in_specs=[pl.BlockSpec((tm, d), lambda i: (i, 0)), pl.BlockSpec((1, d), lambda i: (0, 0))],
        out_specs=pl.BlockSpec((tm, d), lambda i: (i, 0)),
        out_shape=jax.ShapeDtypeStruct((t, d), F32),
        compiler_params=_cparams(("parallel",)),
        name="final_norm",
    )(x, g.reshape(1, d))


def _t5_causal_bucket(dist):
    max_exact = NUM_BUCKETS // 2
    is_small = dist < max_exact
    d = jnp.maximum(dist, 1).astype(F32)
    large = max_exact + (jnp.log(d / max_exact) / math.log(MAX_DISTANCE / max_exact)
                         * (NUM_BUCKETS - max_exact)).astype(jnp.int32)
    large = jnp.minimum(large, NUM_BUCKETS - 1)
    return jnp.where(is_small, dist, large)


def _bias_vectors(rel_bias, seq, tq, tk):
    assert tq == tk and tk + 1 >= MAX_DISTANCE
    rb = rel_bias.astype(F32)
    bias_dist = rb[_t5_causal_bucket(jnp.arange(seq, dtype=jnp.int32))].T
    rel = (bias_dist - rb[NUM_BUCKETS - 1][:, None]) * LOG2E
    d_diag = jnp.arange(2 * tq, dtype=jnp.int32) - (tk - 1)
    d_sub = jnp.minimum(d_diag + tk, seq - 1)
    v_diag = jnp.where(d_diag >= 0, rel[:, jnp.clip(d_diag, 0, seq - 1)], NEG_BIG)
    v_sub = rel[:, d_sub]
    return jnp.stack([v_diag, v_sub], axis=1)[:, :, None, :]


def _moe_dispatch(route, tm):
    t = route.shape[0]
    expert = route[:, :2].astype(jnp.int32).reshape(-1)
    onehot = (expert[:, None] == jnp.arange(N_EXPERTS, dtype=jnp.int32)[None, :]).astype(jnp.int32)
    rank = jnp.take_along_axis(jnp.cumsum(onehot, axis=0), expert[:, None], axis=1)[:, 0] - 1
    counts = jnp.sum(onehot, axis=0)
    padded = ((counts + tm - 1) // tm) * tm
    ends = jnp.cumsum(padded)
    starts = ends - padded
    pos = starts[expert] + rank
    p = 2 * t + N_EXPERTS * tm
    row_token = jnp.zeros((p,), jnp.int32).at[pos].set(jnp.arange(2 * t, dtype=jnp.int32) // 2)
    tile_start = jnp.arange(p // tm, dtype=jnp.int32) * tm
    tile_active = (tile_start < ends[-1]).astype(jnp.int32)
    tile_expert = jnp.minimum(jnp.sum((tile_start[:, None] >= ends[None, :]).astype(jnp.int32), axis=1),
                              N_EXPERTS - 1)
    last_expert = tile_expert[jnp.maximum(ends[-1] // tm - 1, 0)]
    tile_expert = jnp.where(tile_active != 0, tile_expert, last_expert).astype(jnp.int32)
    pos_by_choice = pos.reshape(t, 2).T.reshape(-1)
    return pos_by_choice, row_token, tile_expert, tile_active


def kernel(x, rel_bias, mix_norm_g, w_in, lam_qk, subln_g, conv_w, w_o_attn, w_o_conv, w_o, ffn_norm_g,
           dense_w_gate_up, dense_w_down, router_w, expert_w_gate_up, expert_w_down, final_norm_g):
    batch, seq, d = x.shape
    t = batch * seq
    depth = w_in.shape[0]
    qk_w = N_HEADS * 2 * HEAD_DIM
    v_w = N_HEADS * V_HEAD_DIM
    tq = min(ATTN_TQ, seq)
    tk = tq
    tm = min(ROW_TILE, seq)

    xf = x.reshape(t, d)
    bias_vecs = _bias_vectors(rel_bias, seq, tq, tk)
    conv_cols = tuple(range(5))

    for l in range(depth):
        w3 = w_in[l].astype(BF16).reshape(d, -1, v_w).transpose(1, 0, 2)
        q4, k4, vt5, z = _in_proj(xf, mix_norm_g[l], w3, batch, seq, tk)

        lq = lam_qk[l].astype(F32)
        lam_init = 0.8 - 0.6 * math.exp(-0.3 * l)
        lam = (jnp.exp(jnp.sum(lq[0] * lq[1])) - jnp.exp(jnp.sum(lq[2] * lq[3])) + lam_init).reshape(1)
        gsub = subln_g[l].astype(F32) * (1.0 - lam_init)
        gfull = jnp.broadcast_to(gsub[:, None], (V_HEAD_DIM, tq))
        cast_weights = ()
        if l % 2 == 0 and l + 1 < depth:
            e = (l + 1) // 2
            cast_weights = (expert_w_gate_up[e].reshape(-1, expert_w_gate_up.shape[-1]),
                            expert_w_down[e].reshape(-1, expert_w_down.shape[-1]))
        a, cast_bf16 = _attention(q4, k4, vt5, bias_vecs, lam, gfull, tq, tk, cast_weights)
        if cast_weights:
            expert_bf16 = (cast_bf16[0].reshape(expert_w_gate_up.shape[1:]),
                           cast_bf16[1].reshape(expert_w_down.shape[1:]))

        xf = _mixer_out(z, a.reshape(t, v_w), conv_w[l].astype(F32), w_o_attn[l].astype(BF16),
                        w_o_conv[l].astype(BF16), w_o[l].astype(BF16), xf, conv_cols, tm, seq)

        last = l == depth - 1
        if l % 2 == 0:
            wgu = dense_w_gate_up[l // 2]
            ff = wgu.shape[1] // 2
            xf = _dense_ffn(xf, ffn_norm_g[l], wgu[:, :ff].astype(BF16), wgu[:, ff:].astype(BF16),
                            dense_w_down[l // 2].astype(BF16), tm)
            if last:
                xf = _final_norm(xf, final_norm_g, tm)
        else:
            mt = min(MOE_TILE, seq)
            rw_pad = jnp.zeros((d, LANES), F32).at[:, :N_EXPERTS].set(router_w[l // 2].astype(F32))
            hb, route = _router(xf, ffn_norm_g[l], rw_pad, tm)
            pos_by_choice, row_token, tile_expert, tile_active = _moe_dispatch(route, mt)
            ff = expert_w_down.shape[2]
            n_tiles = row_token.shape[0] // mt
            part_sizes = ([n_tiles // 8 * e for e in MOE_PART_EIGHTHS] if n_tiles % 8 == 0 else [n_tiles])
            ys = jnp.zeros((n_tiles * mt, d), BF16)
            tile0 = 0
            order = jnp.zeros((), jnp.int32)
            for part_tiles in part_sizes:
                rows = slice(tile0 * mt, (tile0 + part_tiles) * mt)
                xs_part = hb[row_token[rows] + order]
                order = (xs_part[0, 0] != xs_part[0, 0]).astype(jnp.int32)
                ys = _grouped_ffn(tile_expert, tile_active, xs_part, expert_bf16[0], expert_bf16[1],
                                  ys, tile0, n_tiles, mt, ff // 2)
                tile0 += part_tiles
            xf = _combine(xf, route, ys[pos_by_choice], final_norm_g, tm, final_norm=last)

    return xf.reshape(batch, seq, d)
```

```python
import functools
import math

import jax
import jax.numpy as jnp
from jax import lax
from jax.experimental import pallas as pl
from jax.experimental.pallas import tpu as pltpu

F32 = jnp.float32
BF16 = jnp.bfloat16

N_HEADS = 8
HEAD_DIM = 64
V_HEAD_DIM = 2 * HEAD_DIM
NUM_BUCKETS = 32
MAX_DISTANCE = 128
N_EXPERTS = 8
RMS_EPS = 1e-6

LANES = 128
BF16_SUBLANES = 16
VMEM_LIMIT_BYTES = 56 * 1024 * 1024

V_ROWS = V_HEAD_DIM + BF16_SUBLANES

LOG2E = math.log2(math.e)
NEG_BIG = -1e30

ATTN_TQ = 512
ATTN_QCHAIN = 256
ATTN_BLOCKS_PER_STEP = 2
ROW_TILE = 512
MOE_TILE = 512
MOE_PART_EIGHTHS = (1, 2, 5)


def _cparams(sem):
    return pltpu.CompilerParams(dimension_semantics=sem, vmem_limit_bytes=VMEM_LIMIT_BYTES)


def _rms(x, g):
    ms = jnp.mean(x * x, axis=-1, keepdims=True)
    return x * lax.rsqrt(ms + RMS_EPS) * g


def _in_proj_kernel(x_ref, g_ref, w_ref, q_ref, k_ref, vt_ref, z_ref):
    h = _rms(x_ref[...], g_ref[...]).astype(BF16)

    def proj(blk):
        return jnp.dot(h, w_ref[blk], preferred_element_type=F32)

    for dst, blk in ((q_ref, 0), (k_ref, 1)):
        y = proj(blk).astype(dst.dtype)
        for hh in range(N_HEADS):
            dst[0, hh] = y[:, hh * LANES:(hh + 1) * LANES]

    yt = proj(2).astype(vt_ref.dtype).T
    pad_rows = V_ROWS - V_HEAD_DIM
    row = lax.broadcasted_iota(jnp.int32, (pad_rows, yt.shape[1]), 0)
    ones_rows = jnp.where(row == 0, 1.0, 0.0).astype(vt_ref.dtype)
    for hh in range(N_HEADS):
        vt_ref[0, hh, 0, :V_HEAD_DIM, :] = yt[hh * V_HEAD_DIM:(hh + 1) * V_HEAD_DIM, :]
        vt_ref[0, hh, 0, V_HEAD_DIM:, :] = ones_rows

    width = w_ref.shape[2]
    for c in range(z_ref.shape[1] // width):
        z_ref[:, c * width:(c + 1) * width] = proj(3 + c).astype(z_ref.dtype)


def _in_proj(x, g, w3, batch, seq, tm):
    t, d = x.shape
    nblk, _, width = w3.shape
    assert width == N_HEADS * 2 * HEAD_DIM == N_HEADS * V_HEAD_DIM
    nt = seq // tm
    n_rest = (nblk - 3) * width
    head_spec = pl.BlockSpec((1, N_HEADS, tm, LANES), lambda i: (i // nt, 0, i % nt, 0))
    return pl.pallas_call(
        _in_proj_kernel,
        grid=(t // tm,),
        in_specs=[
            pl.BlockSpec((tm, d), lambda i: (i, 0)),
            pl.BlockSpec((1, d), lambda i: (0, 0)),
            pl.BlockSpec(w3.shape, lambda i: (0, 0, 0), pipeline_mode=pl.Buffered(1)),
        ],
        out_specs=[
            head_spec, head_spec,
            pl.BlockSpec((1, N_HEADS, 1, V_ROWS, tm), lambda i: (i // nt, 0, i % nt, 0, 0)),
            pl.BlockSpec((tm, n_rest), lambda i: (i, 0)),
        ],
        out_shape=[jax.ShapeDtypeStruct((batch, N_HEADS, seq, LANES), BF16),
                   jax.ShapeDtypeStruct((batch, N_HEADS, seq, LANES), BF16),
                   jax.ShapeDtypeStruct((batch, N_HEADS, nt, V_ROWS, tm), BF16),
                   jax.ShapeDtypeStruct((t, n_rest), BF16)],
        compiler_params=_cparams(("parallel",)),
        name="in_proj",
    )(x, g.reshape(1, d), w3)


def _attn_kernel(*refs, tq, tk, n_cast, n_sub):
    cast_in = refs[7:7 + n_cast]
    cast_out = refs[8 + n_cast:8 + 2 * n_cast]
    for w_ref, wb_ref in zip(cast_in, cast_out):
        wb_ref[...] = w_ref[...].astype(wb_ref.dtype)

    def body(sb, carry):
        _attn_block(sb, *refs, tq=tq, tk=tk, n_cast=n_cast, n_sub=n_sub)
        return carry

    lax.fori_loop(0, n_sub, body, 0)


def _attn_block(sb, lam_ref, q_ref, qn_ref, k_ref, vt_ref, ub_ref, g_ref, *refs, tq, tk, n_cast, n_sub):
    o_ref = refs[n_cast]
    bias_ref, s0_ref, s1_ref, sd_ref, mc0_ref, mc1_ref, mcd_ref, acc_ref, m_ref = refs[2 * n_cast + 1:]
    bufs = ((s0_ref, mc0_ref), (s1_ref, mc1_ref))
    diag_buf = (sd_ref, mcd_ref)
    i = pl.program_id(2) * n_sub + sb
    nq = pl.num_programs(2) * n_sub
    rows = pl.ds(pl.multiple_of(sb * tq, tq), tq)
    rows_next = pl.ds(pl.multiple_of(jnp.minimum(sb + 1, n_sub - 1) * tq, tq), tq)

    def masked_queries(q):
        q = q.astype(F32) * (LOG2E * HEAD_DIM ** -0.5)
        lane = lax.broadcasted_iota(jnp.int32, q.shape, 1)
        return (jnp.where(lane < HEAD_DIM, q, 0.0).astype(BF16),
                jnp.where(lane >= HEAD_DIM, q, 0.0).astype(BF16))

    hk, hq = tk // 2, tq // 2
    nt = (((1,), (1,)), ((), ()))

    def produce(j, bias, buf, queries, diag=False):
        s_ref, mc_ref = buf
        row0 = pl.multiple_of(j * tk, tk)
        k = k_ref[0, 0, pl.ds(row0, tk), :]
        for mp in range(2):
            if diag:
                top = lax.dot_general(k[:hk], queries[mp], nt, preferred_element_type=F32) + bias[:hk]
                low = lax.dot_general(k[hk:], queries[mp][hq:], nt, preferred_element_type=F32) + bias[hk:, hq:]
                s_ref[mp, :hk, :] = top
                s_ref[mp, hk:, hq:] = low
                mc_ref[mp, :, :hq] = jnp.max(top[:, :hq], axis=0, keepdims=True)
                mc_ref[mp, :, hq:] = jnp.maximum(jnp.max(top[:, hq:], axis=0, keepdims=True),
                                                 jnp.max(low, axis=0, keepdims=True))
            else:
                s = lax.dot_general(k, queries[mp], nt, preferred_element_type=F32)
                if bias is not None:
                    s = s + bias
                s_ref[mp] = s
                mc_ref[mp] = jnp.max(s, axis=0, keepdims=True)

    def consume(j, buf, first=False):
        s_ref, mc_ref = buf
        vt = vt_ref[0, 0, j]
        for mp in range(2):
            for h in range(tq // ATTN_QCHAIN):
                cols = slice(h * ATTN_QCHAIN, (h + 1) * ATTN_QCHAIN)
                keys = slice(0, hk) if first and (h + 1) * ATTN_QCHAIN <= hq else slice(0, tk)
                if first:
                    m_new = mc_ref[mp, :, cols]
                else:
                    m_old = m_ref[mp, :, cols]
                    m_new = jnp.maximum(m_old, mc_ref[mp, :, cols])
                p = jnp.exp2(s_ref[mp, keys, cols] - m_new)
                pv = jnp.dot(vt[:, keys], p.astype(BF16), preferred_element_type=F32)
                if first:
                    acc_ref[mp, :, cols] = pv
                else:
                    acc_ref[mp, :, cols] = jnp.exp2(m_old - m_new) * acc_ref[mp, :, cols] + pv
                m_ref[mp, :, cols] = m_new

    qs = masked_queries(q_ref[0, 0, rows, :])

    @pl.when(i == 0)
    def _():
        for t in range(2):
            u = jnp.broadcast_to(ub_ref[0, t], (tk, 2 * tq))
            bias_ref[t] = pltpu.roll(u, 1, 1, stride=1, stride_axis=0)[:, tq:]
        produce(i, bias_ref[0], diag_buf, qs, diag=True)

    def steps(k0, count, from_start=False):
        for t in range(count):
            if from_start and t == 0:
                produce(i - 1, bias_ref[1], bufs[1], qs)
                consume(i, diag_buf, first=True)
                continue
            par = (t + 1) % 2
            f = k0 + t - 1
            prev = (i - 1 if t == 1 else f - 1) if from_start else f - 1
            produce(f, None, bufs[par], qs)
            consume(prev, bufs[1 - par])

    n_quads = i // 4
    rem = i - 4 * n_quads

    @pl.when(n_quads >= 1)
    def _():
        steps(0, 4, from_start=True)

    n_rest = jnp.maximum(n_quads - 1, 0)
    n_hex = n_rest // 4

    def hexdec(g, carry):
        steps(4 + 16 * g, 16)
        return carry

    lax.fori_loop(0, n_hex, hexdec, 0)
    k_rest = 4 + 16 * n_hex

    @pl.when(n_rest % 4 >= 2)
    def _():
        steps(k_rest, 8)

    @pl.when(n_rest % 2 == 1)
    def _():
        steps(k_rest + 8 * ((n_rest % 4) // 2), 4)

    @pl.when(jnp.logical_and(n_quads == 0, rem >= 2))
    def _():
        steps(0, 2, from_start=True)

    @pl.when(jnp.logical_and(n_quads == 0, rem == 1))
    def _():
        steps(0, 1, from_start=True)

    @pl.when(jnp.logical_and(n_quads >= 1, rem >= 2))
    def _():
        steps(4 * n_quads, 2)

    @pl.when(jnp.logical_and(i >= 3, rem % 2 == 1))
    def _():
        steps(i - 1, 1)

    last = jnp.where(i >= 2, i - 2, 0)
    nxt = jnp.minimum(i + 1, nq - 1)

    def finish():
        den = slice(V_HEAD_DIM, V_HEAD_DIM + 1)
        o = (acc_ref[0, :V_HEAD_DIM] / acc_ref[0, den]
             - lam_ref[0] * (acc_ref[1, :V_HEAD_DIM] / acc_ref[1, den]))
        ms = jnp.mean(o * o, axis=0, keepdims=True)
        o = o * lax.rsqrt(ms + RMS_EPS) * g_ref[...]
        o_ref[0, rows, :] = o.T.astype(o_ref.dtype)

    def last_step(j, buf, first=False):
        qs_next = masked_queries(jnp.where(sb == n_sub - 1, qn_ref[0, 0], q_ref[0, 0, rows_next, :]))
        consume(j, buf, first=first)
        finish()
        produce(nxt, bias_ref[0], diag_buf, qs_next, diag=True)

    @pl.when(i == 0)
    def _():
        last_step(0, diag_buf, first=True)

    for parity in range(2):
        @pl.when(jnp.logical_and(i >= 1, i % 2 == parity))
        def _():
            last_step(last, bufs[parity])


def _attention(q4, k4, vt5, bias_vecs, lam, gfull, tq, tk, cast_weights=()):
    batch, _, seq, _ = q4.shape
    nk = seq // tk
    n_sub = ATTN_BLOCKS_PER_STEP if (seq // tq) % ATTN_BLOCKS_PER_STEP == 0 else 1
    nq = seq // (tq * n_sub)
    n_steps = batch * N_HEADS * nq
    cast_specs = []
    for w in cast_weights:
        rows, cols = w.shape
        rb = BF16_SUBLANES * pl.cdiv(rows, BF16_SUBLANES * n_steps)
        assert rows % rb == 0
        last = rows // rb - 1
        cast_specs.append(pl.BlockSpec(
            (rb, cols), lambda b, h, i, last=last: (jnp.minimum((b * N_HEADS + h) * nq + i, last), 0)))
    assert tq == tk and (tq // 2) % ATTN_QCHAIN == 0
    kern = functools.partial(_attn_kernel, tq=tq, tk=tk, n_cast=len(cast_weights), n_sub=n_sub)
    outs = pl.pallas_call(
        kern,
        grid=(batch, N_HEADS, nq),
        in_specs=[
            pl.BlockSpec(memory_space=pltpu.SMEM),
            pl.BlockSpec((1, 1, n_sub * tq, LANES), lambda b, h, i: (b, h, i, 0)),
            pl.BlockSpec((1, 1, tq, LANES), lambda b, h, i: (b, h, jnp.minimum((i + 1) * n_sub, nq * n_sub - 1), 0)),
            pl.BlockSpec((1, 1, seq, LANES), lambda b, h, i: (b, h, 0, 0)),
            pl.BlockSpec((1, 1, nk, V_ROWS, tk), lambda b, h, i: (b, h, 0, 0, 0)),
            pl.BlockSpec((1, 2, 1, 2 * tq), lambda b, h, i: (h, 0, 0, 0)),
            pl.BlockSpec((V_HEAD_DIM, tq), lambda b, h, i: (0, 0)),
        ] + cast_specs,
        out_specs=[pl.BlockSpec((1, n_sub * tq, V_HEAD_DIM), lambda b, h, i: (b, i, h))] + cast_specs,
        out_shape=[jax.ShapeDtypeStruct((batch, seq, N_HEADS * V_HEAD_DIM), BF16)]
        + [jax.ShapeDtypeStruct(w.shape, BF16) for w in cast_weights],
        scratch_shapes=[
            pltpu.VMEM((2, tk, tq), F32),
            pltpu.VMEM((2, tk, tq), F32),
            pltpu.VMEM((2, tk, tq), F32),
            pltpu.VMEM((2, tk, tq), F32),
            pltpu.VMEM((2, 1, tq), F32),
            pltpu.VMEM((2, 1, tq), F32),
            pltpu.VMEM((2, 1, tq), F32),
            pltpu.VMEM((2, V_ROWS, tq), F32),
            pltpu.VMEM((2, 1, tq), F32),
        ],
        compiler_params=_cparams(("parallel", "parallel", "arbitrary")),
        name="diff_attention",
    )(lam, q4, q4, k4, vt5, bias_vecs, gfull, *cast_weights)
    return outs[0], tuple(outs[1:])


def _mixer_out_kernel(ch_ref, cb_ref, cc_ref, ga_ref, gc_ref, chp_ref, ccp_ref, a_ref, cw_ref,
                      woa_ref, woc_ref, wo_ref, x_ref, o_ref, *, tm, seq):
    i = pl.program_id(0)
    u = cc_ref[...].astype(F32) * ch_ref[...].astype(F32)
    up = ccp_ref[...].astype(F32) * chp_ref[...].astype(F32)
    up = jnp.where((i * tm) % seq == 0, 0.0, up)
    row = lax.broadcasted_iota(jnp.int32, u.shape, 0)
    u1 = jnp.where(row == 0, up[7:8], pltpu.roll(u, 1, 0))
    u2 = jnp.where(row == 0, up[6:7], jnp.where(row == 1, up[7:8], pltpu.roll(u, 2, 0)))
    cw = cw_ref[...]
    y = cw[0:1] * u2 + cw[1:2] * u1 + cw[2:3] * u
    c = (cb_ref[...].astype(F32) * y).astype(BF16)
    pa = jnp.dot(a_ref[...], woa_ref[...], preferred_element_type=F32)
    pc = jnp.dot(c, woc_ref[...], preferred_element_type=F32)
    merged = jax.nn.sigmoid(ga_ref[...].astype(F32)) * pa + jax.nn.sigmoid(gc_ref[...].astype(F32)) * pc
    o_ref[...] = x_ref[...] + jnp.dot(merged.astype(BF16), wo_ref[...], preferred_element_type=F32)


def _mixer_out(z, a, conv_w, woa, woc, wo, x, cols, tm, seq):
    t, d = x.shape
    w = conv_w.shape[1]
    c_h, c_b, c_c, c_ga, c_gc = cols
    blk = lambda col: pl.BlockSpec((tm, w), lambda i: (i, col))
    prev = lambda col: pl.BlockSpec((8, w), lambda i: (jnp.maximum(i * (tm // 8) - 1, 0), col))
    full = lambda arr: pl.BlockSpec(arr.shape, lambda i: (0, 0))
    kern = functools.partial(_mixer_out_kernel, tm=tm, seq=seq)
    return pl.pallas_call(
        kern,
        grid=(t // tm,),
        in_specs=[blk(c_h), blk(c_b), blk(c_c), blk(c_ga), blk(c_gc), prev(c_h), prev(c_c),
                  pl.BlockSpec((tm, a.shape[1]), lambda i: (i, 0)),
                  full(conv_w), full(woa), full(woc), full(wo),
                  pl.BlockSpec((tm, d), lambda i: (i, 0))],
        out_specs=pl.BlockSpec((tm, d), lambda i: (i, 0)),
        out_shape=jax.ShapeDtypeStruct((t, d), F32),
        compiler_params=_cparams(("parallel",)),
        name="mixer_out",
    )(z, z, z, z, z, z, z, a, conv_w, woa, woc, wo, x)


def _dense_ffn_kernel(x_ref, g_ref, wg_ref, wu_ref, wd_ref, o_ref):
    x = x_ref[...]
    h = _rms(x, g_ref[...]).astype(BF16)
    gate = jnp.dot(h, wg_ref[...], preferred_element_type=F32)
    up = jnp.dot(h, wu_ref[...], preferred_element_type=F32)
    act = (gate * jax.nn.sigmoid(gate) * up).astype(BF16)
    o_ref[...] = x + jnp.dot(act, wd_ref[...], preferred_element_type=F32)


def _dense_ffn(x, g, wg, wu, wd, tm):
    t, d = x.shape
    resident = lambda arr: pl.BlockSpec(arr.shape, lambda i: (0, 0), pipeline_mode=pl.Buffered(1))
    return pl.pallas_call(
        _dense_ffn_kernel,
        grid=(t // tm,),
        in_specs=[pl.BlockSpec((tm, d), lambda i: (i, 0)),
                  pl.BlockSpec((1, d), lambda i: (0, 0)),
                  resident(wg), resident(wu), resident(wd)],
        out_specs=pl.BlockSpec((tm, d), lambda i: (i, 0)),
        out_shape=jax.ShapeDtypeStruct((t, d), F32),
        compiler_params=_cparams(("parallel",)),
        name="dense_ffn",
    )(x, g.reshape(1, d), wg, wu, wd)


def _split_bf16(v):
    hi = v.astype(BF16)
    lo = (v - hi.astype(F32)).astype(BF16)
    return hi, lo


def _router_kernel(x_ref, g_ref, rw_ref, h_ref, r_ref):
    h = _rms(x_ref[...], g_ref[...])
    h_ref[...] = h.astype(BF16)
    h_hi, h_lo = _split_bf16(h)
    w_hi, w_lo = _split_bf16(rw_ref[...])
    logits = (jnp.dot(h_hi, w_hi, preferred_element_type=F32)
              + jnp.dot(h_hi, w_lo, preferred_element_type=F32)
              + jnp.dot(h_lo, w_hi, preferred_element_type=F32))
    lane = lax.broadcasted_iota(jnp.int32, logits.shape, 1)
    logits = jnp.where(lane < N_EXPERTS, logits, -jnp.inf)
    v1 = jnp.max(logits, axis=-1, keepdims=True)
    i1 = jnp.min(jnp.where(logits == v1, lane, LANES), axis=-1, keepdims=True)
    rest = jnp.where(lane == i1, -jnp.inf, logits)
    v2 = jnp.max(rest, axis=-1, keepdims=True)
    i2 = jnp.min(jnp.where(rest == v2, lane, LANES), axis=-1, keepdims=True)
    e2 = jnp.exp(v2 - v1)
    g1 = 1.0 / (1.0 + e2)
    g2 = e2 / (1.0 + e2)
    out = jnp.where(lane == 0, i1.astype(F32),
                    jnp.where(lane == 1, i2.astype(F32),
                              jnp.where(lane == 2, g1, jnp.where(lane == 3, g2, 0.0))))
    r_ref[...] = out


def _router(x, g, rw_pad, tm):
    t, d = x.shape
    return pl.pallas_call(
        _router_kernel,
        grid=(t // tm,),
        in_specs=[pl.BlockSpec((tm, d), lambda i: (i, 0)),
                  pl.BlockSpec((1, d), lambda i: (0, 0)),
                  pl.BlockSpec(rw_pad.shape, lambda i: (0, 0))],
        out_specs=[pl.BlockSpec((tm, d), lambda i: (i, 0)),
                   pl.BlockSpec((tm, LANES), lambda i: (i, 0))],
        out_shape=[jax.ShapeDtypeStruct((t, d), BF16),
                   jax.ShapeDtypeStruct((t, LANES), F32)],
        compiler_params=_cparams(("parallel",)),
        name="moe_router",
    )(x, g.reshape(1, d), rw_pad)


def _gmm_kernel(te_ref, ta_ref, xs_ref, wg_ref, wu_ref, wd_ref, *rest, nch, tile0):
    o_ref, acc_ref = rest[-2:]
    i = tile0 + pl.program_id(0)
    c = pl.program_id(1)
    active = ta_ref[i] != 0

    @pl.when(active)
    def _():
        x = xs_ref[...]
        gate = jnp.dot(x, wg_ref[0], preferred_element_type=F32)
        up = jnp.dot(x, wu_ref[0], preferred_element_type=F32)
        act = (gate * jax.nn.sigmoid(gate) * up).astype(BF16)
        contrib = jnp.dot(act, wd_ref[0], preferred_element_type=F32)

        @pl.when(c == 0)
        def _():
            acc_ref[...] = contrib

        @pl.when(c != 0)
        def _():
            acc_ref[...] += contrib

        @pl.when(c == nch - 1)
        def _():
            o_ref[...] = acc_ref[...].astype(o_ref.dtype)

    @pl.when(jnp.logical_and(jnp.logical_not(active), c == nch - 1))
    def _():
        o_ref[...] = jnp.zeros(o_ref.shape, o_ref.dtype)


def _grouped_ffn(tile_expert, tile_active, xs_part, w_gu, w_down, ys_prev, tile0, n_tiles, tm, tf):
    p_part, d = xs_part.shape
    ff = w_down.shape[1]
    nch = ff // tf
    kern = functools.partial(_gmm_kernel, nch=nch, tile0=tile0)

    def chunk(c, ta, i):
        return jnp.where(ta[tile0 + i] != 0, c, nch - 1)

    in_specs = [
        pl.BlockSpec((tm, d), lambda i, c, te, ta: (i, 0)),
        pl.BlockSpec((1, d, tf), lambda i, c, te, ta: (te[tile0 + i], 0, chunk(c, ta, i))),
        pl.BlockSpec((1, d, tf), lambda i, c, te, ta: (te[tile0 + i], 0, nch + chunk(c, ta, i))),
        pl.BlockSpec((1, tf, d), lambda i, c, te, ta: (te[tile0 + i], chunk(c, ta, i), 0)),
    ]
    in_specs.append(pl.BlockSpec(memory_space=pl.ANY))
    operands = [tile_expert, tile_active, xs_part, w_gu, w_gu, w_down, ys_prev]
    aliases = {len(operands) - 1: 0}
    grid_spec = pltpu.PrefetchScalarGridSpec(
        num_scalar_prefetch=2,
        grid=(p_part // tm, nch),
        in_specs=in_specs,
        out_specs=pl.BlockSpec((tm, d), lambda i, c, te, ta: (tile0 + i, 0)),
        scratch_shapes=[pltpu.VMEM((tm, d), F32)],
    )
    return pl.pallas_call(
        kern,
        grid_spec=grid_spec,
        out_shape=jax.ShapeDtypeStruct((n_tiles * tm, d), BF16),
        input_output_aliases=aliases,
        compiler_params=_cparams(("arbitrary", "arbitrary")),
        name="moe_grouped_ffn",
    )(*operands)


def _combine_kernel(x_ref, r_ref, y0_ref, y1_ref, g_ref, o_ref, *, final_norm):
    route = r_ref[...]
    x = (x_ref[...] + route[:, 2:3] * y0_ref[...].astype(F32)
         + route[:, 3:4] * y1_ref[...].astype(F32))
    o_ref[...] = _rms(x, g_ref[...]) if final_norm else x


def _combine(x, route, ysel, g, tm, final_norm):
    t, d = x.shape
    nt = t // tm
    kern = functools.partial(_combine_kernel, final_norm=final_norm)
    return pl.pallas_call(
        kern,
        grid=(nt,),
        in_specs=[pl.BlockSpec((tm, d), lambda i: (i, 0)),
                  pl.BlockSpec((tm, LANES), lambda i: (i, 0)),
                  pl.BlockSpec((tm, d), lambda i: (i, 0)),
                  pl.BlockSpec((tm, d), lambda i: (nt + i, 0)),
                  pl.BlockSpec((1, d), lambda i: (0, 0))],
        out_specs=pl.BlockSpec((tm, d), lambda i: (i, 0)),
        out_shape=jax.ShapeDtypeStruct((t, d), F32),
        compiler_params=_cparams(("parallel",)),
        name="moe_combine",
    )(x, route, ysel, ysel, g.reshape(1, d))


def _final_norm_kernel(x_ref, g_ref, o_ref):
    o_ref[...] = _rms(x_ref[...], g_ref[...])


def _final_norm(x, g, tm):
    t, d = x.shape
    return pl.pallas_call(
        _final_norm_kernel,
        grid=(t // tm,),
        in_specs=[pl.BlockSpec((tm, d), lambda i: (i, 0)), pl.BlockSpec((1, d), lambda i: (0, 0))],
        out_specs=pl.BlockSpec((tm, d), lambda i: (i, 0)),
        out_shape=jax.ShapeDtypeStruct((t, d), F32),
        compiler_params=_cparams(("parallel",)),
        name="final_norm",
    )(x, g.reshape(1, d))


def _t5_causal_bucket(dist):
    max_exact = NUM_BUCKETS // 2
    is_small = dist < max_exact
    d = jnp.maximum(dist, 1).astype(F32)
    large = max_exact + (jnp.log(d / max_exact) / math.log(MAX_DISTANCE / max_exact)
                         * (NUM_BUCKETS - max_exact)).astype(jnp.int32)
    large = jnp.minimum(large, NUM_BUCKETS - 1)
    return jnp.where(is_small, dist, large)


def _bias_vectors(rel_bias, seq, tq, tk):
    assert tq == tk and tk + 1 >= MAX_DISTANCE
    rb = rel_bias.astype(F32)
    bias_dist = rb[_t5_causal_bucket(jnp.arange(seq, dtype=jnp.int32))].T
    rel = (bias_dist - rb[NUM_BUCKETS - 1][:, None]) * LOG2E
    d_diag = jnp.arange(2 * tq, dtype=jnp.int32) - (tk - 1)
    d_sub = jnp.minimum(d_diag + tk, seq - 1)
    v_diag = jnp.where(d_diag >= 0, rel[:, jnp.clip(d_diag, 0, seq - 1)], NEG_BIG)
    v_sub = rel[:, d_sub]
    return jnp.stack([v_diag, v_sub], axis=1)[:, :, None, :]


def _moe_dispatch(route, tm):
    t = route.shape[0]
    expert = route[:, :2].astype(jnp.int32).reshape(-1)
    onehot = (expert[:, None] == jnp.arange(N_EXPERTS, dtype=jnp.int32)[None, :]).astype(jnp.int32)
    rank = jnp.take_along_axis(jnp.cumsum(onehot, axis=0), expert[:, None], axis=1)[:, 0] - 1
    counts = jnp.sum(onehot, axis=0)
    padded = ((counts + tm - 1) // tm) * tm
    ends = jnp.cumsum(padded)
    starts = ends - padded
    pos = starts[expert] + rank
    p = 2 * t + N_EXPERTS * tm
    row_token = jnp.zeros((p,), jnp.int32).at[pos].set(jnp.arange(2 * t, dtype=jnp.int32) // 2)
    tile_start = jnp.arange(p // tm, dtype=jnp.int32) * tm
    tile_active = (tile_start < ends[-1]).astype(jnp.int32)
    tile_expert = jnp.minimum(jnp.sum((tile_start[:, None] >= ends[None, :]).astype(jnp.int32), axis=1),
                              N_EXPERTS - 1)
    last_expert = tile_expert[jnp.maximum(ends[-1] // tm - 1, 0)]
    tile_expert = jnp.where(tile_active != 0, tile_expert, last_expert).astype(jnp.int32)
    pos_by_choice = pos.reshape(t, 2).T.reshape(-1)
    return pos_by_choice, row_token, tile_expert, tile_active


def kernel(x, rel_bias, mix_norm_g, w_in, lam_qk, subln_g, conv_w, w_o_attn, w_o_conv, w_o, ffn_norm_g,
           dense_w_gate_up, dense_w_down, router_w, expert_w_gate_up, expert_w_down, final_norm_g):
    batch, seq, d = x.shape
    t = batch * seq
    depth = w_in.shape[0]
    qk_w = N_HEADS * 2 * HEAD_DIM
    v_w = N_HEADS * V_HEAD_DIM
    tq = min(ATTN_TQ, seq)
    tk = tq
    tm = min(ROW_TILE, seq)

    xf = x.reshape(t, d)
    bias_vecs = _bias_vectors(rel_bias, seq, tq, tk)
    conv_cols = tuple(range(5))

    for l in range(depth):
        w3 = w_in[l].astype(BF16).reshape(d, -1, v_w).transpose(1, 0, 2)
        q4, k4, vt5, z = _in_proj(xf, mix_norm_g[l], w3, batch, seq, tk)

        lq = lam_qk[l].astype(F32)
        lam_init = 0.8 - 0.6 * math.exp(-0.3 * l)
        lam = (jnp.exp(jnp.sum(lq[0] * lq[1])) - jnp.exp(jnp.sum(lq[2] * lq[3])) + lam_init).reshape(1)
        gsub = subln_g[l].astype(F32) * (1.0 - lam_init)
        gfull = jnp.broadcast_to(gsub[:, None], (V_HEAD_DIM, tq))
        cast_weights = ()
        if l % 2 == 0 and l + 1 < depth:
            e = (l + 1) // 2
            cast_weights = (expert_w_gate_up[e].reshape(-1, expert_w_gate_up.shape[-1]),
                            expert_w_down[e].reshape(-1, expert_w_down.shape[-1]))
        a, cast_bf16 = _attention(q4, k4, vt5, bias_vecs, lam, gfull, tq, tk, cast_weights)
        if cast_weights:
            expert_bf16 = (cast_bf16[0].reshape(expert_w_gate_up.shape[1:]),
                           cast_bf16[1].reshape(expert_w_down.shape[1:]))

        xf = _mixer_out(z, a.reshape(t, v_w), conv_w[l].astype(F32), w_o_attn[l].astype(BF16),
                        w_o_conv[l].astype(BF16), w_o[l].astype(BF16), xf, conv_cols, tm, seq)

        last = l == depth - 1
        if l % 2 == 0:
            wgu = dense_w_gate_up[l // 2]
            ff = wgu.shape[1] // 2
            xf = _dense_ffn(xf, ffn_norm_g[l], wgu[:, :ff].astype(BF16), wgu[:, ff:].astype(BF16),
                            dense_w_down[l // 2].astype(BF16), tm)
            if last:
                xf = _final_norm(xf, final_norm_g, tm)
        else:
            mt = min(MOE_TILE, seq)
            rw_pad = jnp.zeros((d, LANES), F32).at[:, :N_EXPERTS].set(router_w[l // 2].astype(F32))
            hb, route = _router(xf, ffn_norm_g[l], rw_pad, tm)
            pos_by_choice, row_token, tile_expert, tile_active = _moe_dispatch(route, mt)
            ff = expert_w_down.shape[2]
            n_tiles = row_token.shape[0] // mt
            part_sizes = ([n_tiles // 8 * e for e in MOE_PART_EIGHTHS] if n_tiles % 8 == 0 else [n_tiles])
            ys = jnp.zeros((n_tiles * mt, d), BF16)
            tile0 = 0
            order = jnp.zeros((), jnp.int32)
            for part_tiles in part_sizes:
                rows = slice(tile0 * mt, (tile0 + part_tiles) * mt)
                xs_part = hb[row_token[rows] + order]
                order = (xs_part[0, 0] != xs_part[0, 0]).astype(jnp.int32)
                ys = _grouped_ffn(tile_expert, tile_active, xs_part, expert_bf16[0], expert_bf16[1],
                                  ys, tile0, n_tiles, mt, ff // 2)
                tile0 += part_tiles
            xf = _combine(xf, route, ys[pos_by_choice], final_norm_g, tm, final_norm=last)

    return xf.reshape(batch, seq, d)
```

```python
import functools
import math

import jax
import jax.numpy as jnp
from jax import lax
from jax.experimental import pallas as pl
from jax.experimental.pallas import tpu as pltpu

F32 = jnp.float32
BF16 = jnp.bfloat16

N_HEADS = 8
HEAD_DIM = 64
V_HEAD_DIM = 2 * HEAD_DIM
NUM_BUCKETS = 32
MAX_DISTANCE = 128
N_EXPERTS = 8
RMS_EPS = 1e-6

LANES = 128
BF16_SUBLANES = 16
VMEM_LIMIT_BYTES = 56 * 1024 * 1024

V_ROWS = V_HEAD_DIM + BF16_SUBLANES

LOG2E = math.log2(math.e)
NEG_BIG = -1e30

ATTN_TQ = 512
ATTN_QCHAIN = 256
ATTN_BLOCKS_PER_STEP = 2
ROW_TILE = 512
MOE_TILE = 512
MOE_PART_EIGHTHS = (1, 2, 5)


def _cparams(sem):
    return pltpu.CompilerParams(dimension_semantics=sem, vmem_limit_bytes=VMEM_LIMIT_BYTES)


def _rms(x, g):
    ms = jnp.mean(x * x, axis=-1, keepdims=True)
    return x * lax.rsqrt(ms + RMS_EPS) * g


def _in_proj_kernel(x_ref, g_ref, w_ref, q_ref, k_ref, vt_ref, z_ref):
    h = _rms(x_ref[...], g_ref[...]).astype(BF16)

    def proj(blk):
        return jnp.dot(h, w_ref[blk], preferred_element_type=F32)

    for dst, blk in ((q_ref, 0), (k_ref, 1)):
        y = proj(blk).astype(dst.dtype)
        for hh in range(N_HEADS):
            dst[0, hh] = y[:, hh * LANES:(hh + 1) * LANES]

    yt = proj(2).astype(vt_ref.dtype).T
    pad_rows = V_ROWS - V_HEAD_DIM
    row = lax.broadcasted_iota(jnp.int32, (pad_rows, yt.shape[1]), 0)
    ones_rows = jnp.where(row == 0, 1.0, 0.0).astype(vt_ref.dtype)
    for hh in range(N_HEADS):
        vt_ref[0, hh, 0, :V_HEAD_DIM, :] = yt[hh * V_HEAD_DIM:(hh + 1) * V_HEAD_DIM, :]
        vt_ref[0, hh, 0, V_HEAD_DIM:, :] = ones_rows

    width = w_ref.shape[2]
    for c in range(z_ref.shape[1] // width):
        z_ref[:, c * width:(c + 1) * width] = proj(3 + c).astype(z_ref.dtype)


def _in_proj(x, g, w3, batch, seq, tm):
    t, d = x.shape
    nblk, _, width = w3.shape
    assert width == N_HEADS * 2 * HEAD_DIM == N_HEADS * V_HEAD_DIM
    nt = seq // tm
    n_rest = (nblk - 3) * width
    head_spec = pl.BlockSpec((1, N_HEADS, tm, LANES), lambda i: (i // nt, 0, i % nt, 0))
    return pl.pallas_call(
        _in_proj_kernel,
        grid=(t // tm,),
        in_specs=[
            pl.BlockSpec((tm, d), lambda i: (i, 0)),
            pl.BlockSpec((1, d), lambda i: (0, 0)),
            pl.BlockSpec(w3.shape, lambda i: (0, 0, 0), pipeline_mode=pl.Buffered(1)),
        ],
        out_specs=[
            head_spec, head_spec,
            pl.BlockSpec((1, N_HEADS, 1, V_ROWS, tm), lambda i: (i // nt, 0, i % nt, 0, 0)),
            pl.BlockSpec((tm, n_rest), lambda i: (i, 0)),
        ],
        out_shape=[jax.ShapeDtypeStruct((batch, N_HEADS, seq, LANES), BF16),
                   jax.ShapeDtypeStruct((batch, N_HEADS, seq, LANES), BF16),
                   jax.ShapeDtypeStruct((batch, N_HEADS, nt, V_ROWS, tm), BF16),
                   jax.ShapeDtypeStruct((t, n_rest), BF16)],
        compiler_params=_cparams(("parallel",)),
        name="in_proj",
    )(x, g.reshape(1, d), w3)


def _attn_kernel(*refs, tq, tk, n_cast, n_sub):
    cast_in = refs[7:7 + n_cast]
    cast_out = refs[8 + n_cast:8 + 2 * n_cast]
    for w_ref, wb_ref in zip(cast_in, cast_out):
        wb_ref[...] = w_ref[...].astype(wb_ref.dtype)

    def body(sb, carry):
        _attn_block(sb, *refs, tq=tq, tk=tk, n_cast=n_cast, n_sub=n_sub)
        return carry

    lax.fori_loop(0, n_sub, body, 0)


def _attn_block(sb, lam_ref, q_ref, qn_ref, k_ref, vt_ref, ub_ref, g_ref, *refs, tq, tk, n_cast, n_sub):
    o_ref = refs[n_cast]
    bias_ref, s0_ref, s1_ref, sd_ref, mc0_ref, mc1_ref, mcd_ref, acc_ref, m_ref = refs[2 * n_cast + 1:]
    bufs = ((s0_ref, mc0_ref), (s1_ref, mc1_ref))
    diag_buf = (sd_ref, mcd_ref)
    i = pl.program_id(2) * n_sub + sb
    nq = pl.num_programs(2) * n_sub
    rows = pl.ds(pl.multiple_of(sb * tq, tq), tq)
    rows_next = pl.ds(pl.multiple_of(jnp.minimum(sb + 1, n_sub - 1) * tq, tq), tq)

    def masked_queries(q):
        q = q.astype(F32) * (LOG2E * HEAD_DIM ** -0.5)
        lane = lax.broadcasted_iota(jnp.int32, q.shape, 1)
        return (jnp.where(lane < HEAD_DIM, q, 0.0).astype(BF16),
                jnp.where(lane >= HEAD_DIM, q, 0.0).astype(BF16))

    hk, hq = tk // 2, tq // 2
    nt = (((1,), (1,)), ((), ()))

    def produce(j, bias, buf, queries, diag=False):
        s_ref, mc_ref = buf
        row0 = pl.multiple_of(j * tk, tk)
        k = k_ref[0, 0, pl.ds(row0, tk), :]
        for mp in range(2):
            if diag:
                top = lax.dot_general(k[:hk], queries[mp], nt, preferred_element_type=F32) + bias[:hk]
                low = lax.dot_general(k[hk:], queries[mp][hq:], nt, preferred_element_type=F32) + bias[hk:, hq:]
                s_ref[mp, :hk, :] = top
                s_ref[mp, hk:, hq:] = low
                mc_ref[mp, :, :hq] = jnp.max(top[:, :hq], axis=0, keepdims=True)
                mc_ref[mp, :, hq:] = jnp.maximum(jnp.max(top[:, hq:], axis=0, keepdims=True),
                                                 jnp.max(low, axis=0, keepdims=True))
            else:
                s = lax.dot_general(k, queries[mp], nt, preferred_element_type=F32)
                if bias is not None:
                    s = s + bias
                s_ref[mp] = s
                mc_ref[mp] = jnp.max(s, axis=0, keepdims=True)

    def consume(j, buf, first=False):
        s_ref, mc_ref = buf
        vt = vt_ref[0, 0, j]
        for mp in range(2):
            for h in range(tq // ATTN_QCHAIN):
                cols = slice(h * ATTN_QCHAIN, (h + 1) * ATTN_QCHAIN)
                keys = slice(0, hk) if first and (h + 1) * ATTN_QCHAIN <= hq else slice(0, tk)
                if first:
                    m_new = mc_ref[mp, :, cols]
                else:
                    m_old = m_ref[mp, :, cols]
                    m_new = jnp.maximum(m_old, mc_ref[mp, :, cols])
                p = jnp.exp2(s_ref[mp, keys, cols] - m_new)
                pv = jnp.dot(vt[:, keys], p.astype(BF16), preferred_element_type=F32)
                if first:
                    acc_ref[mp, :, cols] = pv
                else:
                    acc_ref[mp, :, cols] = jnp.exp2(m_old - m_new) * acc_ref[mp, :, cols] + pv
                m_ref[mp, :, cols] = m_new

    qs = masked_queries(q_ref[0, 0, rows, :])

    @pl.when(i == 0)
    def _():
        for t in range(2):
            u = jnp.broadcast_to(ub_ref[0, t], (tk, 2 * tq))
            bias_ref[t] = pltpu.roll(u, 1, 1, stride=1, stride_axis=0)[:, tq:]
        produce(i, bias_ref[0], diag_buf, qs, diag=True)

    def steps(k0, count, from_start=False):
        for t in range(count):
            if from_start and t == 0:
                produce(i - 1, bias_ref[1], bufs[1], qs)
                consume(i, diag_buf, first=True)
                continue
            par = (t + 1) % 2
            f = k0 + t - 1
            prev = (i - 1 if t == 1 else f - 1) if from_start else f - 1
            produce(f, None, bufs[par], qs)
            consume(prev, bufs[1 - par])

    n_quads = i // 4
    rem = i - 4 * n_quads

    @pl.when(n_quads == 1)
    def _():
        steps(0, 4, from_start=True)

    @pl.when(n_quads >= 2)
    def _():
        steps(0, 8, from_start=True)

    k_first = jnp.where(n_quads >= 2, 8, 4)
    n_rest = jnp.maximum(n_quads - 2, 0)
    n_hex = n_rest // 4

    def hexdec(g, carry):
        steps(8 + 16 * g, 16)
        return carry

    lax.fori_loop(0, n_hex, hexdec, 0)
    k_rest = k_first + 16 * n_hex

    @pl.when(n_rest % 4 >= 2)
    def _():
        steps(k_rest, 8)

    @pl.when(n_rest % 2 == 1)
    def _():
        steps(k_rest + 8 * ((n_rest % 4) // 2), 4)

    @pl.when(jnp.logical_and(n_quads == 0, rem >= 2))
    def _():
        steps(0, 2, from_start=True)

    @pl.when(jnp.logical_and(n_quads == 0, rem == 1))
    def _():
        steps(0, 1, from_start=True)

    @pl.when(jnp.logical_and(n_quads >= 1, rem >= 2))
    def _():
        steps(4 * n_quads, 2)

    @pl.when(jnp.logical_and(i >= 3, rem % 2 == 1))
    def _():
        steps(i - 1, 1)

    last = jnp.where(i >= 2, i - 2, 0)
    nxt = jnp.minimum(i + 1, nq - 1)

    def finish():
        den = slice(V_HEAD_DIM, V_HEAD_DIM + 1)
        o = (acc_ref[0, :V_HEAD_DIM] / acc_ref[0, den]
             - lam_ref[0] * (acc_ref[1, :V_HEAD_DIM] / acc_ref[1, den]))
        ms = jnp.mean(o * o, axis=0, keepdims=True)
        o = o * lax.rsqrt(ms + RMS_EPS) * g_ref[...]
        o_ref[0, rows, :] = o.T.astype(o_ref.dtype)

    def last_step(j, buf, first=False):
        qs_next = masked_queries(jnp.where(sb == n_sub - 1, qn_ref[0, 0], q_ref[0, 0, rows_next, :]))
        consume(j, buf, first=first)
        finish()
        produce(nxt, bias_ref[0], diag_buf, qs_next, diag=True)

    @pl.when(i == 0)
    def _():
        last_step(0, diag_buf, first=True)

    for parity in range(2):
        @pl.when(jnp.logical_and(i >= 1, i % 2 == parity))
        def _():
            last_step(last, bufs[parity])


def _attention(q4, k4, vt5, bias_vecs, lam, gfull, tq, tk, cast_weights=()):
    batch, _, seq, _ = q4.shape
    nk = seq // tk
    n_sub = ATTN_BLOCKS_PER_STEP if (seq // tq) % ATTN_BLOCKS_PER_STEP == 0 else 1
    nq = seq // (tq * n_sub)
    n_steps = batch * N_HEADS * nq
    cast_specs = []
    for w in cast_weights:
        rows, cols = w.shape
        rb = BF16_SUBLANES * pl.cdiv(rows, BF16_SUBLANES * n_steps)
        assert rows % rb == 0
        last = rows // rb - 1
        cast_specs.append(pl.BlockSpec(
            (rb, cols), lambda b, h, i, last=last: (jnp.minimum((b * N_HEADS + h) * nq + i, last), 0)))
    assert tq == tk and (tq // 2) % ATTN_QCHAIN == 0
    kern = functools.partial(_attn_kernel, tq=tq, tk=tk, n_cast=len(cast_weights), n_sub=n_sub)
    outs = pl.pallas_call(
        kern,
        grid=(batch, N_HEADS, nq),
        in_specs=[
            pl.BlockSpec(memory_space=pltpu.SMEM),
            pl.BlockSpec((1, 1, n_sub * tq, LANES), lambda b, h, i: (b, h, i, 0)),
            pl.BlockSpec((1, 1, tq, LANES), lambda b, h, i: (b, h, jnp.minimum((i + 1) * n_sub, nq * n_sub - 1), 0)),
            pl.BlockSpec((1, 1, seq, LANES), lambda b, h, i: (b, h, 0, 0)),
            pl.BlockSpec((1, 1, nk, V_ROWS, tk), lambda b, h, i: (b, h, 0, 0, 0)),
            pl.BlockSpec((1, 2, 1, 2 * tq), lambda b, h, i: (h, 0, 0, 0)),
            pl.BlockSpec((V_HEAD_DIM, tq), lambda b, h, i: (0, 0)),
        ] + cast_specs,
        out_specs=[pl.BlockSpec((1, n_sub * tq, V_HEAD_DIM), lambda b, h, i: (b, i, h))] + cast_specs,
        out_shape=[jax.ShapeDtypeStruct((batch, seq, N_HEADS * V_HEAD_DIM), BF16)]
        + [jax.ShapeDtypeStruct(w.shape, BF16) for w in cast_weights],
        scratch_shapes=[
            pltpu.VMEM((2, tk, tq), F32),
            pltpu.VMEM((2, tk, tq), F32),
            pltpu.VMEM((2, tk, tq), F32),
            pltpu.VMEM((2, tk, tq), F32),
            pltpu.VMEM((2, 1, tq), F32),
            pltpu.VMEM((2, 1, tq), F32),
            pltpu.VMEM((2, 1, tq), F32),
            pltpu.VMEM((2, V_ROWS, tq), F32),
            pltpu.VMEM((2, 1, tq), F32),
        ],
        compiler_params=_cparams(("parallel", "parallel", "arbitrary")),
        name="diff_attention",
    )(lam, q4, q4, k4, vt5, bias_vecs, gfull, *cast_weights)
    return outs[0], tuple(outs[1:])


def _mixer_out_kernel(ch_ref, cb_ref, cc_ref, ga_ref, gc_ref, chp_ref, ccp_ref, a_ref, cw_ref,
                      woa_ref, woc_ref, wo_ref, x_ref, o_ref, *, tm, seq):
    i = pl.program_id(0)
    u = cc_ref[...].astype(F32) * ch_ref[...].astype(F32)
    up = ccp_ref[...].astype(F32) * chp_ref[...].astype(F32)
    up = jnp.where((i * tm) % seq == 0, 0.0, up)
    row = lax.broadcasted_iota(jnp.int32, u.shape, 0)
    u1 = jnp.where(row == 0, up[7:8], pltpu.roll(u, 1, 0))
    u2 = jnp.where(row == 0, up[6:7], jnp.where(row == 1, up[7:8], pltpu.roll(u, 2, 0)))
    cw = cw_ref[...]
    y = cw[0:1] * u2 + cw[1:2] * u1 + cw[2:3] * u
    c = (cb_ref[...].astype(F32) * y).astype(BF16)
    pa = jnp.dot(a_ref[...], woa_ref[...], preferred_element_type=F32)
    pc = jnp.dot(c, woc_ref[...], preferred_element_type=F32)
    merged = jax.nn.sigmoid(ga_ref[...].astype(F32)) * pa + jax.nn.sigmoid(gc_ref[...].astype(F32)) * pc
    o_ref[...] = x_ref[...] + jnp.dot(merged.astype(BF16), wo_ref[...], preferred_element_type=F32)


def _mixer_out(z, a, conv_w, woa, woc, wo, x, cols, tm, seq):
    t, d = x.shape
    w = conv_w.shape[1]
    c_h, c_b, c_c, c_ga, c_gc = cols
    blk = lambda col: pl.BlockSpec((tm, w), lambda i: (i, col))
    prev = lambda col: pl.BlockSpec((8, w), lambda i: (jnp.maximum(i * (tm // 8) - 1, 0), col))
    full = lambda arr: pl.BlockSpec(arr.shape, lambda i: (0, 0))
    kern = functools.partial(_mixer_out_kernel, tm=tm, seq=seq)
    return pl.pallas_call(
        kern,
        grid=(t // tm,),
        in_specs=[blk(c_h), blk(c_b), blk(c_c), blk(c_ga), blk(c_gc), prev(c_h), prev(c_c),
                  pl.BlockSpec((tm, a.shape[1]), lambda i: (i, 0)),
                  full(conv_w), full(woa), full(woc), full(wo),
                  pl.BlockSpec((tm, d), lambda i: (i, 0))],
        out_specs=pl.BlockSpec((tm, d), lambda i: (i, 0)),
        out_shape=jax.ShapeDtypeStruct((t, d), F32),
        compiler_params=_cparams(("parallel",)),
        name="mixer_out",
    )(z, z, z, z, z, z, z, a, conv_w, woa, woc, wo, x)


def _dense_ffn_kernel(x_ref, g_ref, wg_ref, wu_ref, wd_ref, o_ref):
    x = x_ref[...]
    h = _rms(x, g_ref[...]).astype(BF16)
    gate = jnp.dot(h, wg_ref[...], preferred_element_type=F32)
    up = jnp.dot(h, wu_ref[...], preferred_element_type=F32)
    act = (gate * jax.nn.sigmoid(gate) * up).astype(BF16)
    o_ref[...] = x + jnp.dot(act, wd_ref[...], preferred_element_type=F32)


def _dense_ffn(x, g, wg, wu, wd, tm):
    t, d = x.shape
    resident = lambda arr: pl.BlockSpec(arr.shape, lambda i: (0, 0), pipeline_mode=pl.Buffered(1))
    return pl.pallas_call(
        _dense_ffn_kernel,
        grid=(t // tm,),
        in_specs=[pl.BlockSpec((tm, d), lambda i: (i, 0)),
                  pl.BlockSpec((1, d), lambda i: (0, 0)),
                  resident(wg), resident(wu), resident(wd)],
        out_specs=pl.BlockSpec((tm, d), lambda i: (i, 0)),
        out_shape=jax.ShapeDtypeStruct((t, d), F32),
        compiler_params=_cparams(("parallel",)),
        name="dense_ffn",
    )(x, g.reshape(1, d), wg, wu, wd)


def _split_bf16(v):
    hi = v.astype(BF16)
    lo = (v - hi.astype(F32)).astype(BF16)
    return hi, lo


def _router_kernel(x_ref, g_ref, rw_ref, h_ref, r_ref):
    h = _rms(x_ref[...], g_ref[...])
    h_ref[...] = h.astype(BF16)
    h_hi, h_lo = _split_bf16(h)
    w_hi, w_lo = _split_bf16(rw_ref[...])
    logits = (jnp.dot(h_hi, w_hi, preferred_element_type=F32)
              + jnp.dot(h_hi, w_lo, preferred_element_type=F32)
              + jnp.dot(h_lo, w_hi, preferred_element_type=F32))
    lane = lax.broadcasted_iota(jnp.int32, logits.shape, 1)
    logits = jnp.where(lane < N_EXPERTS, logits, -jnp.inf)
    v1 = jnp.max(logits, axis=-1, keepdims=True)
    i1 = jnp.min(jnp.where(logits == v1, lane, LANES), axis=-1, keepdims=True)
    rest = jnp.where(lane == i1, -jnp.inf, logits)
    v2 = jnp.max(rest, axis=-1, keepdims=True)
    i2 = jnp.min(jnp.where(rest == v2, lane, LANES), axis=-1, keepdims=True)
    e2 = jnp.exp(v2 - v1)
    g1 = 1.0 / (1.0 + e2)
    g2 = e2 / (1.0 + e2)
    out = jnp.where(lane == 0, i1.astype(F32),
                    jnp.where(lane == 1, i2.astype(F32),
                              jnp.where(lane == 2, g1, jnp.where(lane == 3, g2, 0.0))))
    r_ref[...] = out


def _router(x, g, rw_pad, tm):
    t, d = x.shape
    return pl.pallas_call(
        _router_kernel,
        grid=(t // tm,),
        in_specs=[pl.BlockSpec((tm, d), lambda i: (i, 0)),
                  pl.BlockSpec((1, d), lambda i: (0, 0)),
                  pl.BlockSpec(rw_pad.shape, lambda i: (0, 0))],
        out_specs=[pl.BlockSpec((tm, d), lambda i: (i, 0)),
                   pl.BlockSpec((tm, LANES), lambda i: (i, 0))],
        out_shape=[jax.ShapeDtypeStruct((t, d), BF16),
                   jax.ShapeDtypeStruct((t, LANES), F32)],
        compiler_params=_cparams(("parallel",)),
        name="moe_router",
    )(x, g.reshape(1, d), rw_pad)


def _gmm_kernel(te_ref, ta_ref, xs_ref, wg_ref, wu_ref, wd_ref, *rest, nch, tile0):
    o_ref, acc_ref = rest[-2:]
    i = tile0 + pl.program_id(0)
    c = pl.program_id(1)
    active = ta_ref[i] != 0

    @pl.when(active)
    def _():
        x = xs_ref[...]
        gate = jnp.dot(x, wg_ref[0], preferred_element_type=F32)
        up = jnp.dot(x, wu_ref[0], preferred_element_type=F32)
        act = (gate * jax.nn.sigmoid(gate) * up).astype(BF16)
        contrib = jnp.dot(act, wd_ref[0], preferred_element_type=F32)

        @pl.when(c == 0)
        def _():
            acc_ref[...] = contrib

        @pl.when(c != 0)
        def _():
            acc_ref[...] += contrib

        @pl.when(c == nch - 1)
        def _():
            o_ref[...] = acc_ref[...].astype(o_ref.dtype)

    @pl.when(jnp.logical_and(jnp.logical_not(active), c == nch - 1))
    def _():
        o_ref[...] = jnp.zeros(o_ref.shape, o_ref.dtype)


def _grouped_ffn(tile_expert, tile_active, xs_part, w_gu, w_down, ys_prev, tile0, n_tiles, tm, tf):
    p_part, d = xs_part.shape
    ff = w_down.shape[1]
    nch = ff // tf
    kern = functools.partial(_gmm_kernel, nch=nch, tile0=tile0)

    def chunk(c, ta, i):
        return jnp.where(ta[tile0 + i] != 0, c, nch - 1)

    in_specs = [
        pl.BlockSpec((tm, d), lambda i, c, te, ta: (i, 0)),
        pl.BlockSpec((1, d, tf), lambda i, c, te, ta: (te[tile0 + i], 0, chunk(c, ta, i))),
        pl.BlockSpec((1, d, tf), lambda i, c, te, ta: (te[tile0 + i], 0, nch + chunk(c, ta, i))),
        pl.BlockSpec((1, tf, d), lambda i, c, te, ta: (te[tile0 + i], chunk(c, ta, i), 0)),
    ]
    in_specs.append(pl.BlockSpec(memory_space=pl.ANY))
    operands = [tile_expert, tile_active, xs_part, w_gu, w_gu, w_down, ys_prev]
    aliases = {len(operands) - 1: 0}
    grid_spec = pltpu.PrefetchScalarGridSpec(
        num_scalar_prefetch=2,
        grid=(p_part // tm, nch),
        in_specs=in_specs,
        out_specs=pl.BlockSpec((tm, d), lambda i, c, te, ta: (tile0 + i, 0)),
        scratch_shapes=[pltpu.VMEM((tm, d), F32)],
    )
    return pl.pallas_call(
        kern,
        grid_spec=grid_spec,
        out_shape=jax.ShapeDtypeStruct((n_tiles * tm, d), BF16),
        input_output_aliases=aliases,
        compiler_params=_cparams(("arbitrary", "arbitrary")),
        name="moe_grouped_ffn",
    )(*operands)


def _combine_kernel(x_ref, r_ref, y0_ref, y1_ref, g_ref, o_ref, *, final_norm):
    route = r_ref[...]
    x = (x_ref[...] + route[:, 2:3] * y0_ref[...].astype(F32)
         + route[:, 3:4] * y1_ref[...].astype(F32))
    o_ref[...] = _rms(x, g_ref[...]) if final_norm else x


def _combine(x, route, ysel, g, tm, final_norm):
    t, d = x.shape
    nt = t // tm
    kern = functools.partial(_combine_kernel, final_norm=final_norm)
    return pl.pallas_call(
        kern,
        grid=(nt,),
        in_specs=[pl.BlockSpec((tm, d), lambda i: (i, 0)),
                  pl.BlockSpec((tm, LANES), lambda i: (i, 0)),
                  pl.BlockSpec((tm, d), lambda i: (i, 0)),
                  pl.BlockSpec((tm, d), lambda i: (nt + i, 0)),
                  pl.BlockSpec((1, d), lambda i: (0, 0))],
        out_specs=pl.BlockSpec((tm, d), lambda i: (i, 0)),
        out_shape=jax.ShapeDtypeStruct((t, d), F32),
        compiler_params=_cparams(("parallel",)),
        name="moe_combine",
    )(x, route, ysel, ysel, g.reshape(1, d))


def _final_norm_kernel(x_ref, g_ref, o_ref):
    o_ref[...] = _rms(x_ref[...], g_ref[...])


def _final_norm(x, g, tm):
    t, d = x.shape
    return pl.pallas_call(
        _final_norm_kernel,
        grid=(t // tm,),
        in_specs=[pl.BlockSpec((tm, d), lambda i: (i, 0)), pl.BlockSpec((1, d), lambda i: (0, 0))],
        out_specs=pl.BlockSpec((tm, d), lambda i: (i, 0)),
        out_shape=jax.ShapeDtypeStruct((t, d), F32),
        compiler_params=_cparams(("parallel",)),
        name="final_norm",
    )(x, g.reshape(1, d))


def _t5_causal_bucket(dist):
    max_exact = NUM_BUCKETS // 2
    is_small = dist < max_exact
    d = jnp.maximum(dist, 1).astype(F32)
    large = max_exact + (jnp.log(d / max_exact) / math.log(MAX_DISTANCE / max_exact)
                         * (NUM_BUCKETS - max_exact)).astype(jnp.int32)
    large = jnp.minimum(large, NUM_BUCKETS - 1)
    return jnp.where(is_small, dist, large)


def _bias_vectors(rel_bias, seq, tq, tk):
    assert tq == tk and tk + 1 >= MAX_DISTANCE
    rb = rel_bias.astype(F32)
    bias_dist = rb[_t5_causal_bucket(jnp.arange(seq, dtype=jnp.int32))].T
    rel = (bias_dist - rb[NUM_BUCKETS - 1][:, None]) * LOG2E
    d_diag = jnp.arange(2 * tq, dtype=jnp.int32) - (tk - 1)
    d_sub = jnp.minimum(d_diag + tk, seq - 1)
    v_diag = jnp.where(d_diag >= 0, rel[:, jnp.clip(d_diag, 0, seq - 1)], NEG_BIG)
    v_sub = rel[:, d_sub]
    return jnp.stack([v_diag, v_sub], axis=1)[:, :, None, :]


def _moe_dispatch(route, tm):
    t = route.shape[0]
    expert = route[:, :2].astype(jnp.int32).reshape(-1)
    onehot = (expert[:, None] == jnp.arange(N_EXPERTS, dtype=jnp.int32)[None, :]).astype(jnp.int32)
    rank = jnp.take_along_axis(jnp.cumsum(onehot, axis=0), expert[:, None], axis=1)[:, 0] - 1
    counts = jnp.sum(onehot, axis=0)
    padded = ((counts + tm - 1) // tm) * tm
    ends = jnp.cumsum(padded)
    starts = ends - padded
    pos = starts[expert] + rank
    p = 2 * t + N_EXPERTS * tm
    row_token = jnp.zeros((p,), jnp.int32).at[pos].set(jnp.arange(2 * t, dtype=jnp.int32) // 2)
    tile_start = jnp.arange(p // tm, dtype=jnp.int32) * tm
    tile_active = (tile_start < ends[-1]).astype(jnp.int32)
    tile_expert = jnp.minimum(jnp.sum((tile_start[:, None] >= ends[None, :]).astype(jnp.int32), axis=1),
                              N_EXPERTS - 1)
    last_expert = tile_expert[jnp.maximum(ends[-1] // tm - 1, 0)]
    tile_expert = jnp.where(tile_active != 0, tile_expert, last_expert).astype(jnp.int32)
    pos_by_choice = pos.reshape(t, 2).T.reshape(-1)
    return pos_by_choice, row_token, tile_expert, tile_active


def kernel(x, rel_bias, mix_norm_g, w_in, lam_qk, subln_g, conv_w, w_o_attn, w_o_conv, w_o, ffn_norm_g,
           dense_w_gate_up, dense_w_down, router_w, expert_w_gate_up, expert_w_down, final_norm_g):
    batch, seq, d = x.shape
    t = batch * seq
    depth = w_in.shape[0]
    qk_w = N_HEADS * 2 * HEAD_DIM
    v_w = N_HEADS * V_HEAD_DIM
    tq = min(ATTN_TQ, seq)
    tk = tq
    tm = min(ROW_TILE, seq)

    xf = x.reshape(t, d)
    bias_vecs = _bias_vectors(rel_bias, seq, tq, tk)
    conv_cols = tuple(range(5))

    for l in range(depth):
        w3 = w_in[l].astype(BF16).reshape(d, -1, v_w).transpose(1, 0, 2)
        q4, k4, vt5, z = _in_proj(xf, mix_norm_g[l], w3, batch, seq, tk)

        lq = lam_qk[l].astype(F32)
        lam_init = 0.8 - 0.6 * math.exp(-0.3 * l)
        lam = (jnp.exp(jnp.sum(lq[0] * lq[1])) - jnp.exp(jnp.sum(lq[2] * lq[3])) + lam_init).reshape(1)
        gsub = subln_g[l].astype(F32) * (1.0 - lam_init)
        gfull = jnp.broadcast_to(gsub[:, None], (V_HEAD_DIM, tq))
        cast_weights = ()
        if l % 2 == 0 and l + 1 < depth:
            e = (l + 1) // 2
            cast_weights = (expert_w_gate_up[e].reshape(-1, expert_w_gate_up.shape[-1]),
                            expert_w_down[e].reshape(-1, expert_w_down.shape[-1]))
        a, cast_bf16 = _attention(q4, k4, vt5, bias_vecs, lam, gfull, tq, tk, cast_weights)
        if cast_weights:
            expert_bf16 = (cast_bf16[0].reshape(expert_w_gate_up.shape[1:]),
                           cast_bf16[1].reshape(expert_w_down.shape[1:]))

        xf = _mixer_out(z, a.reshape(t, v_w), conv_w[l].astype(F32), w_o_attn[l].astype(BF16),
                        w_o_conv[l].astype(BF16), w_o[l].astype(BF16), xf, conv_cols, tm, seq)

        last = l == depth - 1
        if l % 2 == 0:
            wgu = dense_w_gate_up[l // 2]
            ff = wgu.shape[1] // 2
            xf = _dense_ffn(xf, ffn_norm_g[l], wgu[:, :ff].astype(BF16), wgu[:, ff:].astype(BF16),
                            dense_w_down[l // 2].astype(BF16), tm)
            if last:
                xf = _final_norm(xf, final_norm_g, tm)
        else:
            mt = min(MOE_TILE, seq)
            rw_pad = jnp.zeros((d, LANES), F32).at[:, :N_EXPERTS].set(router_w[l // 2].astype(F32))
            hb, route = _router(xf, ffn_norm_g[l], rw_pad, tm)
            pos_by_choice, row_token, tile_expert, tile_active = _moe_dispatch(route, mt)
            ff = expert_w_down.shape[2]
            n_tiles = row_token.shape[0] // mt
            part_sizes = ([n_tiles // 8 * e for e in MOE_PART_EIGHTHS] if n_tiles % 8 == 0 else [n_tiles])
            ys = jnp.zeros((n_tiles * mt, d), BF16)
            tile0 = 0
            order = jnp.zeros((), jnp.int32)
            for part_tiles in part_sizes:
                rows = slice(tile0 * mt, (tile0 + part_tiles) * mt)
                xs_part = hb[row_token[rows] + order]
                order = (xs_part[0, 0] != xs_part[0, 0]).astype(jnp.int32)
                ys = _grouped_ffn(tile_expert, tile_active, xs_part, expert_bf16[0], expert_bf16[1],
                                  ys, tile0, n_tiles, mt, ff // 2)
                tile0 += part_tiles
            xf = _combine(xf, route, ys[pos_by_choice], final_norm_g, tm, final_norm=last)

    return xf.reshape(batch, seq, d)
```

```python
import functools
import math

import jax
import jax.numpy as jnp
from jax import lax
from jax.experimental import pallas as pl
from jax.experimental.pallas import tpu as pltpu

F32 = jnp.float32
BF16 = jnp.bfloat16

N_HEADS = 8
HEAD_DIM = 64
V_HEAD_DIM = 2 * HEAD_DIM
NUM_BUCKETS = 32
MAX_DISTANCE = 128
N_EXPERTS = 8
RMS_EPS = 1e-6

LANES = 128
BF16_SUBLANES = 16
VMEM_LIMIT_BYTES = 56 * 1024 * 1024

V_ROWS = V_HEAD_DIM + BF16_SUBLANES

LOG2E = math.log2(math.e)
NEG_BIG = -1e30

ATTN_TQ = 512
ATTN_QCHAIN = 256
ATTN_BLOCKS_PER_STEP = 2
ROW_TILE = 512
MOE_TILE = 512
MOE_PART_EIGHTHS = (1, 2, 5)


def _cparams(sem):
    return pltpu.CompilerParams(dimension_semantics=sem, vmem_limit_bytes=VMEM_LIMIT_BYTES)


def _sigmoid(x):
    return 0.5 * jnp.tanh(0.5 * x) + 0.5


def _rms(x, g):
    ms = jnp.mean(x * x, axis=-1, keepdims=True)
    return x * lax.rsqrt(ms + RMS_EPS) * g


def _in_proj_kernel(x_ref, g_ref, w_ref, q_ref, k_ref, vt_ref, z_ref):
    h = _rms(x_ref[...], g_ref[...]).astype(BF16)

    def proj(blk):
        return jnp.dot(h, w_ref[blk], preferred_element_type=F32)

    for dst, blk in ((q_ref, 0), (k_ref, 1)):
        y = proj(blk).astype(dst.dtype)
        for hh in range(N_HEADS):
            dst[0, hh] = y[:, hh * LANES:(hh + 1) * LANES]

    yt = proj(2).astype(vt_ref.dtype).T
    pad_rows = V_ROWS - V_HEAD_DIM
    row = lax.broadcasted_iota(jnp.int32, (pad_rows, yt.shape[1]), 0)
    ones_rows = jnp.where(row == 0, 1.0, 0.0).astype(vt_ref.dtype)
    for hh in range(N_HEADS):
        vt_ref[0, hh, 0, :V_HEAD_DIM, :] = yt[hh * V_HEAD_DIM:(hh + 1) * V_HEAD_DIM, :]
        vt_ref[0, hh, 0, V_HEAD_DIM:, :] = ones_rows

    width = w_ref.shape[2]
    for c in range(z_ref.shape[1] // width):
        z_ref[:, c * width:(c + 1) * width] = proj(3 + c).astype(z_ref.dtype)


def _in_proj(x, g, w3, batch, seq, tm):
    t, d = x.shape
    nblk, _, width = w3.shape
    assert width == N_HEADS * 2 * HEAD_DIM == N_HEADS * V_HEAD_DIM
    nt = seq // tm
    n_rest = (nblk - 3) * width
    head_spec = pl.BlockSpec((1, N_HEADS, tm, LANES), lambda i: (i // nt, 0, i % nt, 0))
    return pl.pallas_call(
        _in_proj_kernel,
        grid=(t // tm,),
        in_specs=[
            pl.BlockSpec((tm, d), lambda i: (i, 0)),
            pl.BlockSpec((1, d), lambda i: (0, 0)),
            pl.BlockSpec(w3.shape, lambda i: (0, 0, 0), pipeline_mode=pl.Buffered(1)),
        ],
        out_specs=[
            head_spec, head_spec,
            pl.BlockSpec((1, N_HEADS, 1, V_ROWS, tm), lambda i: (i // nt, 0, i % nt, 0, 0)),
            pl.BlockSpec((tm, n_rest), lambda i: (i, 0)),
        ],
        out_shape=[jax.ShapeDtypeStruct((batch, N_HEADS, seq, LANES), BF16),
                   jax.ShapeDtypeStruct((batch, N_HEADS, seq, LANES), BF16),
                   jax.ShapeDtypeStruct((batch, N_HEADS, nt, V_ROWS, tm), BF16),
                   jax.ShapeDtypeStruct((t, n_rest), BF16)],
        compiler_params=_cparams(("parallel",)),
        name="in_proj",
    )(x, g.reshape(1, d), w3)


def _attn_kernel(*refs, tq, tk, n_cast, n_sub):
    cast_in = refs[7:7 + n_cast]
    cast_out = refs[8 + n_cast:8 + 2 * n_cast]
    for w_ref, wb_ref in zip(cast_in, cast_out):
        wb_ref[...] = w_ref[...].astype(wb_ref.dtype)

    def body(sb, carry):
        _attn_block(sb, *refs, tq=tq, tk=tk, n_cast=n_cast, n_sub=n_sub)
        return carry

    lax.fori_loop(0, n_sub, body, 0)


def _attn_block(sb, lam_ref, q_ref, qn_ref, k_ref, vt_ref, ub_ref, g_ref, *refs, tq, tk, n_cast, n_sub):
    o_ref = refs[n_cast]
    bias_ref, s0_ref, s1_ref, sd_ref, mc0_ref, mc1_ref, mcd_ref, acc_ref, m_ref = refs[2 * n_cast + 1:]
    bufs = ((s0_ref, mc0_ref), (s1_ref, mc1_ref))
    diag_buf = (sd_ref, mcd_ref)
    i = pl.program_id(2) * n_sub + sb
    nq = pl.num_programs(2) * n_sub
    rows = pl.ds(pl.multiple_of(sb * tq, tq), tq)
    rows_next = pl.ds(pl.multiple_of(jnp.minimum(sb + 1, n_sub - 1) * tq, tq), tq)

    def masked_queries(q):
        q = q.astype(F32) * (LOG2E * HEAD_DIM ** -0.5)
        lane = lax.broadcasted_iota(jnp.int32, q.shape, 1)
        return (jnp.where(lane < HEAD_DIM, q, 0.0).astype(BF16),
                jnp.where(lane >= HEAD_DIM, q, 0.0).astype(BF16))

    hk, hq = tk // 2, tq // 2
    nt = (((1,), (1,)), ((), ()))

    def produce(j, bias, buf, queries, diag=False):
        s_ref, mc_ref = buf
        row0 = pl.multiple_of(j * tk, tk)
        k = k_ref[0, 0, pl.ds(row0, tk), :]
        for mp in range(2):
            if diag:
                top = lax.dot_general(k[:hk], queries[mp], nt, preferred_element_type=F32) + bias[:hk]
                low = lax.dot_general(k[hk:], queries[mp][hq:], nt, preferred_element_type=F32) + bias[hk:, hq:]
                s_ref[mp, :hk, :] = top
                s_ref[mp, hk:, hq:] = low
                mc_ref[mp, :, :hq] = jnp.max(top[:, :hq], axis=0, keepdims=True)
                mc_ref[mp, :, hq:] = jnp.maximum(jnp.max(top[:, hq:], axis=0, keepdims=True),
                                                 jnp.max(low, axis=0, keepdims=True))
            else:
                s = lax.dot_general(k, queries[mp], nt, preferred_element_type=F32)
                if bias is not None:
                    s = s + bias
                s_ref[mp] = s
                mc_ref[mp] = jnp.max(s, axis=0, keepdims=True)

    def consume(j, buf, first=False):
        s_ref, mc_ref = buf
        vt = vt_ref[0, 0, j]
        for mp in range(2):
            for h in range(tq // ATTN_QCHAIN):
                cols = slice(h * ATTN_QCHAIN, (h + 1) * ATTN_QCHAIN)
                keys = slice(0, hk) if first and (h + 1) * ATTN_QCHAIN <= hq else slice(0, tk)
                if first:
                    m_new = mc_ref[mp, :, cols]
                else:
                    m_old = m_ref[mp, :, cols]
                    m_new = jnp.maximum(m_old, mc_ref[mp, :, cols])
                p = jnp.exp2(s_ref[mp, keys, cols] - m_new)
                pv = jnp.dot(vt[:, keys], p.astype(BF16), preferred_element_type=F32)
                if first:
                    acc_ref[mp, :, cols] = pv
                else:
                    acc_ref[mp, :, cols] = jnp.exp2(m_old - m_new) * acc_ref[mp, :, cols] + pv
                m_ref[mp, :, cols] = m_new

    qs = masked_queries(q_ref[0, 0, rows, :])

    @pl.when(i == 0)
    def _():
        for t in range(2):
            u = jnp.broadcast_to(ub_ref[0, t], (tk, 2 * tq))
            bias_ref[t] = pltpu.roll(u, 1, 1, stride=1, stride_axis=0)[:, tq:]
        produce(i, bias_ref[0], diag_buf, qs, diag=True)

    def steps(k0, count, from_start=False):
        for t in range(count):
            if from_start and t == 0:
                produce(i - 1, bias_ref[1], bufs[1], qs)
                consume(i, diag_buf, first=True)
                continue
            par = (t + 1) % 2
            f = k0 + t - 1
            prev = (i - 1 if t == 1 else f - 1) if from_start else f - 1
            produce(f, None, bufs[par], qs)
            consume(prev, bufs[1 - par])

    n_quads = i // 4
    rem = i - 4 * n_quads

    @pl.when(n_quads == 1)
    def _():
        steps(0, 4, from_start=True)

    @pl.when(n_quads >= 2)
    def _():
        steps(0, 8, from_start=True)

    k_first = jnp.where(n_quads >= 2, 8, 4)
    n_rest = jnp.maximum(n_quads - 2, 0)
    n_hex = n_rest // 4

    def hexdec(g, carry):
        steps(8 + 16 * g, 16)
        return carry

    lax.fori_loop(0, n_hex, hexdec, 0)
    k_rest = k_first + 16 * n_hex

    @pl.when(n_rest % 4 >= 2)
    def _():
        steps(k_rest, 8)

    @pl.when(n_rest % 2 == 1)
    def _():
        steps(k_rest + 8 * ((n_rest % 4) // 2), 4)

    @pl.when(jnp.logical_and(n_quads == 0, rem >= 2))
    def _():
        steps(0, 2, from_start=True)

    @pl.when(jnp.logical_and(n_quads == 0, rem == 1))
    def _():
        steps(0, 1, from_start=True)

    @pl.when(jnp.logical_and(n_quads >= 1, rem >= 2))
    def _():
        steps(4 * n_quads, 2)

    @pl.when(jnp.logical_and(i >= 3, rem % 2 == 1))
    def _():
        steps(i - 1, 1)

    last = jnp.where(i >= 2, i - 2, 0)
    nxt = jnp.minimum(i + 1, nq - 1)

    def finish():
        den = slice(V_HEAD_DIM, V_HEAD_DIM + 1)
        o = (acc_ref[0, :V_HEAD_DIM] / acc_ref[0, den]
             - lam_ref[0] * (acc_ref[1, :V_HEAD_DIM] / acc_ref[1, den]))
        ms = jnp.mean(o * o, axis=0, keepdims=True)
        o = o * lax.rsqrt(ms + RMS_EPS) * g_ref[...]
        o_ref[0, rows, :] = o.T.astype(o_ref.dtype)

    def last_step(j, buf, first=False):
        qs_next = masked_queries(jnp.where(sb == n_sub - 1, qn_ref[0, 0], q_ref[0, 0, rows_next, :]))
        consume(j, buf, first=first)
        finish()
        produce(nxt, bias_ref[0], diag_buf, qs_next, diag=True)

    @pl.when(i == 0)
    def _():
        last_step(0, diag_buf, first=True)

    for parity in range(2):
        @pl.when(jnp.logical_and(i >= 1, i % 2 == parity))
        def _():
            last_step(last, bufs[parity])


def _attention(q4, k4, vt5, bias_vecs, lam, gfull, tq, tk, cast_weights=()):
    batch, _, seq, _ = q4.shape
    nk = seq // tk
    n_sub = ATTN_BLOCKS_PER_STEP if (seq // tq) % ATTN_BLOCKS_PER_STEP == 0 else 1
    nq = seq // (tq * n_sub)
    n_steps = batch * N_HEADS * nq
    cast_specs = []
    for w in cast_weights:
        rows, cols = w.shape
        rb = BF16_SUBLANES * pl.cdiv(rows, BF16_SUBLANES * n_steps)
        assert rows % rb == 0
        last = rows // rb - 1
        cast_specs.append(pl.BlockSpec(
            (rb, cols), lambda b, h, i, last=last: (jnp.minimum((b * N_HEADS + h) * nq + i, last), 0)))
    assert tq == tk and (tq // 2) % ATTN_QCHAIN == 0
    kern = functools.partial(_attn_kernel, tq=tq, tk=tk, n_cast=len(cast_weights), n_sub=n_sub)
    outs = pl.pallas_call(
        kern,
        grid=(batch, N_HEADS, nq),
        in_specs=[
            pl.BlockSpec(memory_space=pltpu.SMEM),
            pl.BlockSpec((1, 1, n_sub * tq, LANES), lambda b, h, i: (b, h, i, 0)),
            pl.BlockSpec((1, 1, tq, LANES), lambda b, h, i: (b, h, jnp.minimum((i + 1) * n_sub, nq * n_sub - 1), 0)),
            pl.BlockSpec((1, 1, seq, LANES), lambda b, h, i: (b, h, 0, 0)),
            pl.BlockSpec((1, 1, nk, V_ROWS, tk), lambda b, h, i: (b, h, 0, 0, 0)),
            pl.BlockSpec((1, 2, 1, 2 * tq), lambda b, h, i: (h, 0, 0, 0)),
            pl.BlockSpec((V_HEAD_DIM, tq), lambda b, h, i: (0, 0)),
        ] + cast_specs,
        out_specs=[pl.BlockSpec((1, n_sub * tq, V_HEAD_DIM), lambda b, h, i: (b, i, h))] + cast_specs,
        out_shape=[jax.ShapeDtypeStruct((batch, seq, N_HEADS * V_HEAD_DIM), BF16)]
        + [jax.ShapeDtypeStruct(w.shape, BF16) for w in cast_weights],
        scratch_shapes=[
            pltpu.VMEM((2, tk, tq), F32),
            pltpu.VMEM((2, tk, tq), F32),
            pltpu.VMEM((2, tk, tq), F32),
            pltpu.VMEM((2, tk, tq), F32),
            pltpu.VMEM((2, 1, tq), F32),
            pltpu.VMEM((2, 1, tq), F32),
            pltpu.VMEM((2, 1, tq), F32),
            pltpu.VMEM((2, V_ROWS, tq), F32),
            pltpu.VMEM((2, 1, tq), F32),
        ],
        compiler_params=_cparams(("parallel", "parallel", "arbitrary")),
        name="diff_attention",
    )(lam, q4, q4, k4, vt5, bias_vecs, gfull, *cast_weights)
    return outs[0], tuple(outs[1:])


def _mixer_out_kernel(ch_ref, cb_ref, cc_ref, ga_ref, gc_ref, chp_ref, ccp_ref, a_ref, cw_ref,
                      woa_ref, woc_ref, wo_ref, x_ref, o_ref, *, tm, seq):
    i = pl.program_id(0)
    u = cc_ref[...].astype(F32) * ch_ref[...].astype(F32)
    up = ccp_ref[...].astype(F32) * chp_ref[...].astype(F32)
    up = jnp.where((i * tm) % seq == 0, 0.0, up)
    row = lax.broadcasted_iota(jnp.int32, u.shape, 0)
    u1 = jnp.where(row == 0, up[7:8], pltpu.roll(u, 1, 0))
    u2 = jnp.where(row == 0, up[6:7], jnp.where(row == 1, up[7:8], pltpu.roll(u, 2, 0)))
    cw = cw_ref[...]
    y = cw[0:1] * u2 + cw[1:2] * u1 + cw[2:3] * u
    c = (cb_ref[...].astype(F32) * y).astype(BF16)
    pa = jnp.dot(a_ref[...], woa_ref[...], preferred_element_type=F32)
    pc = jnp.dot(c, woc_ref[...], preferred_element_type=F32)
    merged = _sigmoid(ga_ref[...].astype(F32)) * pa + _sigmoid(gc_ref[...].astype(F32)) * pc
    o_ref[...] = x_ref[...] + jnp.dot(merged.astype(BF16), wo_ref[...], preferred_element_type=F32)


def _mixer_out(z, a, conv_w, woa, woc, wo, x, cols, tm, seq):
    t, d = x.shape
    w = conv_w.shape[1]
    c_h, c_b, c_c, c_ga, c_gc = cols
    blk = lambda col: pl.BlockSpec((tm, w), lambda i: (i, col))
    prev = lambda col: pl.BlockSpec((8, w), lambda i: (jnp.maximum(i * (tm // 8) - 1, 0), col))
    full = lambda arr: pl.BlockSpec(arr.shape, lambda i: (0, 0))
    kern = functools.partial(_mixer_out_kernel, tm=tm, seq=seq)
    return pl.pallas_call(
        kern,
        grid=(t // tm,),
        in_specs=[blk(c_h), blk(c_b), blk(c_c), blk(c_ga), blk(c_gc), prev(c_h), prev(c_c),
                  pl.BlockSpec((tm, a.shape[1]), lambda i: (i, 0)),
                  full(conv_w), full(woa), full(woc), full(wo),
                  pl.BlockSpec((tm, d), lambda i: (i, 0))],
        out_specs=pl.BlockSpec((tm, d), lambda i: (i, 0)),
        out_shape=jax.ShapeDtypeStruct((t, d), F32),
        compiler_params=_cparams(("parallel",)),
        name="mixer_out",
    )(z, z, z, z, z, z, z, a, conv_w, woa, woc, wo, x)


def _dense_ffn_kernel(x_ref, g_ref, wg_ref, wu_ref, wd_ref, o_ref):
    x = x_ref[...]
    h = _rms(x, g_ref[...]).astype(BF16)
    gate = jnp.dot(h, wg_ref[...], preferred_element_type=F32)
    up = jnp.dot(h, wu_ref[...], preferred_element_type=F32)
    act = (gate * _sigmoid(gate) * up).astype(BF16)
    o_ref[...] = x + jnp.dot(act, wd_ref[...], preferred_element_type=F32)


def _dense_ffn(x, g, wg, wu, wd, tm):
    t, d = x.shape
    resident = lambda arr: pl.BlockSpec(arr.shape, lambda i: (0, 0), pipeline_mode=pl.Buffered(1))
    return pl.pallas_call(
        _dense_ffn_kernel,
        grid=(t // tm,),
        in_specs=[pl.BlockSpec((tm, d), lambda i: (i, 0)),
                  pl.BlockSpec((1, d), lambda i: (0, 0)),
                  resident(wg), resident(wu), resident(wd)],
        out_specs=pl.BlockSpec((tm, d), lambda i: (i, 0)),
        out_shape=jax.ShapeDtypeStruct((t, d), F32),
        compiler_params=_cparams(("parallel",)),
        name="dense_ffn",
    )(x, g.reshape(1, d), wg, wu, wd)


def _split_bf16(v):
    hi = v.astype(BF16)
    lo = (v - hi.astype(F32)).astype(BF16)
    return hi, lo


def _router_kernel(x_ref, g_ref, rw_ref, h_ref, r_ref):
    h = _rms(x_ref[...], g_ref[...])
    h_ref[...] = h.astype(BF16)
    h_hi, h_lo = _split_bf16(h)
    w_hi, w_lo = _split_bf16(rw_ref[...])
    logits = (jnp.dot(h_hi, w_hi, preferred_element_type=F32)
              + jnp.dot(h_hi, w_lo, preferred_element_type=F32)
              + jnp.dot(h_lo, w_hi, preferred_element_type=F32))
    lane = lax.broadcasted_iota(jnp.int32, logits.shape, 1)
    logits = jnp.where(lane < N_EXPERTS, logits, -jnp.inf)
    v1 = jnp.max(logits, axis=-1, keepdims=True)
    i1 = jnp.min(jnp.where(logits == v1, lane, LANES), axis=-1, keepdims=True)
    rest = jnp.where(lane == i1, -jnp.inf, logits)
    v2 = jnp.max(rest, axis=-1, keepdims=True)
    i2 = jnp.min(jnp.where(rest == v2, lane, LANES), axis=-1, keepdims=True)
    e2 = jnp.exp(v2 - v1)
    g1 = 1.0 / (1.0 + e2)
    g2 = e2 / (1.0 + e2)
    out = jnp.where(lane == 0, i1.astype(F32),
                    jnp.where(lane == 1, i2.astype(F32),
                              jnp.where(lane == 2, g1, jnp.where(lane == 3, g2, 0.0))))
    r_ref[...] = out


def _router(x, g, rw_pad, tm):
    t, d = x.shape
    return pl.pallas_call(
        _router_kernel,
        grid=(t // tm,),
        in_specs=[pl.BlockSpec((tm, d), lambda i: (i, 0)),
                  pl.BlockSpec((1, d), lambda i: (0, 0)),
                  pl.BlockSpec(rw_pad.shape, lambda i: (0, 0))],
        out_specs=[pl.BlockSpec((tm, d), lambda i: (i, 0)),
                   pl.BlockSpec((tm, LANES), lambda i: (i, 0))],
        out_shape=[jax.ShapeDtypeStruct((t, d), BF16),
                   jax.ShapeDtypeStruct((t, LANES), F32)],
        compiler_params=_cparams(("parallel",)),
        name="moe_router",
    )(x, g.reshape(1, d), rw_pad)


def _gmm_kernel(te_ref, ta_ref, xs_ref, wg_ref, wu_ref, wd_ref, *rest, nch, tile0):
    o_ref, acc_ref = rest[-2:]
    i = tile0 + pl.program_id(0)
    c = pl.program_id(1)
    active = ta_ref[i] != 0

    @pl.when(active)
    def _():
        x = xs_ref[...]
        gate = jnp.dot(x, wg_ref[0], preferred_element_type=F32)
        up = jnp.dot(x, wu_ref[0], preferred_element_type=F32)
        act = (gate * _sigmoid(gate) * up).astype(BF16)
        contrib = jnp.dot(act, wd_ref[0], preferred_element_type=F32)

        @pl.when(c == 0)
        def _():
            acc_ref[...] = contrib

        @pl.when(c != 0)
        def _():
            acc_ref[...] += contrib

        @pl.when(c == nch - 1)
        def _():
            o_ref[...] = acc_ref[...].astype(o_ref.dtype)

    @pl.when(jnp.logical_and(jnp.logical_not(active), c == nch - 1))
    def _():
        o_ref[...] = jnp.zeros(o_ref.shape, o_ref.dtype)


def _grouped_ffn(tile_expert, tile_active, xs_part, w_gu, w_down, ys_prev, tile0, n_tiles, tm, tf):
    p_part, d = xs_part.shape
    ff = w_down.shape[1]
    nch = ff // tf
    kern = functools.partial(_gmm_kernel, nch=nch, tile0=tile0)

    def chunk(c, ta, i):
        return jnp.where(ta[tile0 + i] != 0, c, nch - 1)

    in_specs = [
        pl.BlockSpec((tm, d), lambda i, c, te, ta: (i, 0)),
        pl.BlockSpec((1, d, tf), lambda i, c, te, ta: (te[tile0 + i], 0, chunk(c, ta, i))),
        pl.BlockSpec((1, d, tf), lambda i, c, te, ta: (te[tile0 + i], 0, nch + chunk(c, ta, i))),
        pl.BlockSpec((1, tf, d), lambda i, c, te, ta: (te[tile0 + i], chunk(c, ta, i), 0)),
    ]
    in_specs.append(pl.BlockSpec(memory_space=pl.ANY))
    operands = [tile_expert, tile_active, xs_part, w_gu, w_gu, w_down, ys_prev]
    aliases = {len(operands) - 1: 0}
    grid_spec = pltpu.PrefetchScalarGridSpec(
        num_scalar_prefetch=2,
        grid=(p_part // tm, nch),
        in_specs=in_specs,
        out_specs=pl.BlockSpec((tm, d), lambda i, c, te, ta: (tile0 + i, 0)),
        scratch_shapes=[pltpu.VMEM((tm, d), F32)],
    )
    return pl.pallas_call(
        kern,
        grid_spec=grid_spec,
        out_shape=jax.ShapeDtypeStruct((n_tiles * tm, d), BF16),
        input_output_aliases=aliases,
        compiler_params=_cparams(("arbitrary", "arbitrary")),
        name="moe_grouped_ffn",
    )(*operands)


def _combine_kernel(x_ref, r_ref, y0_ref, y1_ref, g_ref, o_ref, *, final_norm):
    route = r_ref[...]
    x = (x_ref[...] + route[:, 2:3] * y0_ref[...].astype(F32)
         + route[:, 3:4] * y1_ref[...].astype(F32))
    o_ref[...] = _rms(x, g_ref[...]) if final_norm else x


def _combine(x, route, ysel, g, tm, final_norm):
    t, d = x.shape
    nt = t // tm
    kern = functools.partial(_combine_kernel, final_norm=final_norm)
    return pl.pallas_call(
        kern,
        grid=(nt,),
        in_specs=[pl.BlockSpec((tm, d), lambda i: (i, 0)),
                  pl.BlockSpec((tm, LANES), lambda i: (i, 0)),
                  pl.BlockSpec((tm, d), lambda i: (i, 0)),
                  pl.BlockSpec((tm, d), lambda i: (nt + i, 0)),
                  pl.BlockSpec((1, d), lambda i: (0, 0))],
        out_specs=pl.BlockSpec((tm, d), lambda i: (i, 0)),
        out_shape=jax.ShapeDtypeStruct((t, d), F32),
        compiler_params=_cparams(("parallel",)),
        name="moe_combine",
    )(x, route, ysel, ysel, g.reshape(1, d))


def _final_norm_kernel(x_ref, g_ref, o_ref):
    o_ref[...] = _rms(x_ref[...], g_ref[...])


def _final_norm(x, g, tm):
    t, d = x.shape
    return pl.pallas_call(
        _final_norm_kernel,
        grid=(t // tm,),
        in_specs=[pl.BlockSpec((tm, d), lambda i: (i, 0)), pl.BlockSpec((1, d), lambda i: (0, 0))],
        out_specs=pl.BlockSpec((tm, d), lambda i: (i, 0)),
        out_shape=jax.ShapeDtypeStruct((t, d), F32),
        compiler_params=_cparams(("parallel",)),
        name="final_norm",
    )(x, g.reshape(1, d))


def _t5_causal_bucket(dist):
    max_exact = NUM_BUCKETS // 2
    is_small = dist < max_exact
    d = jnp.maximum(dist, 1).astype(F32)
    large = max_exact + (jnp.log(d / max_exact) / math.log(MAX_DISTANCE / max_exact)
                         * (NUM_BUCKETS - max_exact)).astype(jnp.int32)
    large = jnp.minimum(large, NUM_BUCKETS - 1)
    return jnp.where(is_small, dist, large)


def _bias_vectors(rel_bias, seq, tq, tk):
    assert tq == tk and tk + 1 >= MAX_DISTANCE
    rb = rel_bias.astype(F32)
    bias_dist = rb[_t5_causal_bucket(jnp.arange(seq, dtype=jnp.int32))].T
    rel = (bias_dist - rb[NUM_BUCKETS - 1][:, None]) * LOG2E
    d_diag = jnp.arange(2 * tq, dtype=jnp.int32) - (tk - 1)
    d_sub = jnp.minimum(d_diag + tk, seq - 1)
    v_diag = jnp.where(d_diag >= 0, rel[:, jnp.clip(d_diag, 0, seq - 1)], NEG_BIG)
    v_sub = rel[:, d_sub]
    return jnp.stack([v_diag, v_sub], axis=1)[:, :, None, :]


def _moe_dispatch(route, tm):
    t = route.shape[0]
    expert = route[:, :2].astype(jnp.int32).reshape(-1)
    onehot = (expert[:, None] == jnp.arange(N_EXPERTS, dtype=jnp.int32)[None, :]).astype(jnp.int32)
    rank = jnp.take_along_axis(jnp.cumsum(onehot, axis=0), expert[:, None], axis=1)[:, 0] - 1
    counts = jnp.sum(onehot, axis=0)
    padded = ((counts + tm - 1) // tm) * tm
    ends = jnp.cumsum(padded)
    starts = ends - padded
    pos = starts[expert] + rank
    p = 2 * t + N_EXPERTS * tm
    row_token = jnp.zeros((p,), jnp.int32).at[pos].set(jnp.arange(2 * t, dtype=jnp.int32) // 2)
    tile_start = jnp.arange(p // tm, dtype=jnp.int32) * tm
    tile_active = (tile_start < ends[-1]).astype(jnp.int32)
    tile_expert = jnp.minimum(jnp.sum((tile_start[:, None] >= ends[None, :]).astype(jnp.int32), axis=1),
                              N_EXPERTS - 1)
    last_expert = tile_expert[jnp.maximum(ends[-1] // tm - 1, 0)]
    tile_expert = jnp.where(tile_active != 0, tile_expert, last_expert).astype(jnp.int32)
    pos_by_choice = pos.reshape(t, 2).T.reshape(-1)
    return pos_by_choice, row_token, tile_expert, tile_active


def kernel(x, rel_bias, mix_norm_g, w_in, lam_qk, subln_g, conv_w, w_o_attn, w_o_conv, w_o, ffn_norm_g,
           dense_w_gate_up, dense_w_down, router_w, expert_w_gate_up, expert_w_down, final_norm_g):
    batch, seq, d = x.shape
    t = batch * seq
    depth = w_in.shape[0]
    qk_w = N_HEADS * 2 * HEAD_DIM
    v_w = N_HEADS * V_HEAD_DIM
    tq = min(ATTN_TQ, seq)
    tk = tq
    tm = min(ROW_TILE, seq)

    xf = x.reshape(t, d)
    bias_vecs = _bias_vectors(rel_bias, seq, tq, tk)
    conv_cols = tuple(range(5))

    for l in range(depth):
        w3 = w_in[l].astype(BF16).reshape(d, -1, v_w).transpose(1, 0, 2)
        q4, k4, vt5, z = _in_proj(xf, mix_norm_g[l], w3, batch, seq, tk)

        lq = lam_qk[l].astype(F32)
        lam_init = 0.8 - 0.6 * math.exp(-0.3 * l)
        lam = (jnp.exp(jnp.sum(lq[0] * lq[1])) - jnp.exp(jnp.sum(lq[2] * lq[3])) + lam_init).reshape(1)
        gsub = subln_g[l].astype(F32) * (1.0 - lam_init)
        gfull = jnp.broadcast_to(gsub[:, None], (V_HEAD_DIM, tq))
        cast_weights = ()
        if l % 2 == 0 and l + 1 < depth:
            e = (l + 1) // 2
            cast_weights = (expert_w_gate_up[e].reshape(-1, expert_w_gate_up.shape[-1]),
                            expert_w_down[e].reshape(-1, expert_w_down.shape[-1]))
        a, cast_bf16 = _attention(q4, k4, vt5, bias_vecs, lam, gfull, tq, tk, cast_weights)
        if cast_weights:
            expert_bf16 = (cast_bf16[0].reshape(expert_w_gate_up.shape[1:]),
                           cast_bf16[1].reshape(expert_w_down.shape[1:]))

        xf = _mixer_out(z, a.reshape(t, v_w), conv_w[l].astype(F32), w_o_attn[l].astype(BF16),
                        w_o_conv[l].astype(BF16), w_o[l].astype(BF16), xf, conv_cols, tm, seq)

        last = l == depth - 1
        if l % 2 == 0:
            wgu = dense_w_gate_up[l // 2]
            ff = wgu.shape[1] // 2
            xf = _dense_ffn(xf, ffn_norm_g[l], wgu[:, :ff].astype(BF16), wgu[:, ff:].astype(BF16),
                            dense_w_down[l // 2].astype(BF16), tm)
            if last:
                xf = _final_norm(xf, final_norm_g, tm)
        else:
            mt = min(MOE_TILE, seq)
            rw_pad = jnp.zeros((d, LANES), F32).at[:, :N_EXPERTS].set(router_w[l // 2].astype(F32))
            hb, route = _router(xf, ffn_norm_g[l], rw_pad, tm)
            pos_by_choice, row_token, tile_expert, tile_active = _moe_dispatch(route, mt)
            ff = expert_w_down.shape[2]
            n_tiles = row_token.shape[0] // mt
            part_sizes = ([n_tiles // 8 * e for e in MOE_PART_EIGHTHS] if n_tiles % 8 == 0 else [n_tiles])
            ys = jnp.zeros((n_tiles * mt, d), BF16)
            tile0 = 0
            order = jnp.zeros((), jnp.int32)
            for part_tiles in part_sizes:
                rows = slice(tile0 * mt, (tile0 + part_tiles) * mt)
                xs_part = hb[row_token[rows] + order]
                order = (xs_part[0, 0] != xs_part[0, 0]).astype(jnp.int32)
                ys = _grouped_ffn(tile_expert, tile_active, xs_part, expert_bf16[0], expert_bf16[1],
                                  ys, tile0, n_tiles, mt, ff // 2)
                tile0 += part_tiles
            xf = _combine(xf, route, ys[pos_by_choice], final_norm_g, tm, final_norm=last)

    return xf.reshape(batch, seq, d)
```
